```python
import jax, jax.numpy as jnp
from jax import lax
import numpy as np

D_MODEL = 1024
BATCH = 2
SEQ = 8192
DEPTH = 1

GRID_W = 64
CTX_LEN = 256
MLA_HEADS = 8
QK_NOPE = 128
QK_ROPE = 64
V_DIM = 128
Q_LORA = 384
KV_LORA = 256
ROPE_BASE = 10000.0
ROPE_PAIRS = QK_ROPE // 4
Q_BLOCK = 128
ATTN_SCALE = (QK_NOPE + QK_ROPE) ** -0.5
MLA_WIDTH = MLA_HEADS * V_DIM
ML_HEADS = 4
ML_INNER = 1024
ML_HEAD_DIM = ML_INNER // ML_HEADS
QKV_BLOCK = 4
QKV_NBLK = ML_INNER // QKV_BLOCK
CONV_W = 5
CHUNK = 128
N_EXPERTS = 16
EXPERT_FF = 1024
CAP_FACTOR = 2
EPS = 1e-6
IN_SIZES = (Q_LORA, KV_LORA, QK_ROPE, ML_INNER, ML_INNER, D_MODEL, D_MODEL)
IN_DIM = Q_LORA + KV_LORA + QK_ROPE + 2 * ML_INNER + 2 * D_MODEL

kernel_name = "hybrid_mla_mlstm_ec_moe_dit"


def rmsnorm(x, g):
    xf = x.astype(jnp.float32)
    y = xf * lax.rsqrt(jnp.mean(xf * xf, axis=-1, keepdims=True) + EPS)
    return (y * g.astype(jnp.float32)).astype(x.dtype)


def modulate(h, shift, scale):
    return h * (1 + scale[:, None, :]) + shift[:, None, :]


def axial_angles(T):
    rows = T // GRID_W
    row = jnp.repeat(jnp.arange(rows, dtype=jnp.float32), GRID_W)
    col = jnp.tile(jnp.arange(GRID_W, dtype=jnp.float32), rows)
    inv = ROPE_BASE ** (-jnp.arange(ROPE_PAIRS, dtype=jnp.float32) / ROPE_PAIRS)
    return row[:, None] * inv, col[:, None] * inv


def rotate(x, ang):
    cos = jnp.cos(ang)[None, :, None, :]
    sin = jnp.sin(ang)[None, :, None, :]
    xf = x.astype(jnp.float32)
    x1, x2 = xf[..., :ROPE_PAIRS], xf[..., ROPE_PAIRS:]
    return jnp.concatenate([x1 * cos - x2 * sin, x2 * cos + x1 * sin], -1).astype(x.dtype)


def rope_2d(x, ang_row, ang_col):
    half = QK_ROPE // 2
    return jnp.concatenate([rotate(x[..., :half], ang_row), rotate(x[..., half:], ang_col)], -1)


def mla_qkv(q_lat, kv_lat, k_rope, q_norm, w_uq, kv_norm, w_ukv, angles):
    B, T, _ = q_lat.shape
    q = (rmsnorm(q_lat, q_norm) @ w_uq).reshape(B, T, MLA_HEADS, QK_NOPE + QK_ROPE)
    kv = (rmsnorm(kv_lat, kv_norm) @ w_ukv).reshape(B, T, MLA_HEADS, QK_NOPE + V_DIM)
    q_nope, q_pe = q[..., :QK_NOPE], q[..., QK_NOPE:]
    k_nope, v = kv[..., :QK_NOPE], kv[..., QK_NOPE:]
    k_pe = k_rope[:, :, None, :]
    if angles is not None:
        q_pe = rope_2d(q_pe, *angles)
        k_pe = rope_2d(k_pe, *angles)
    k_pe = jnp.broadcast_to(k_pe, (B, T, MLA_HEADS, QK_ROPE))
    return (jnp.concatenate([q_nope, q_pe], -1), jnp.concatenate([k_nope, k_pe], -1), v)


def attend(q, k, v):
    B, Tq, H, Dk = q.shape
    nb = Tq // Q_BLOCK
    qb = q.reshape(B, nb, Q_BLOCK, H, Dk).transpose(1, 0, 2, 3, 4)

    def one_block(qi):
        s = jnp.einsum('bqhd,bkhd->bhqk', qi, k).astype(jnp.float32) * ATTN_SCALE
        p = jax.nn.softmax(s, axis=-1).astype(v.dtype)
        return jnp.einsum('bhqk,bkhd->bqhd', p, v)

    o = lax.map(one_block, qb)
    return o.transpose(1, 0, 2, 3, 4).reshape(B, Tq, H * v.shape[-1])


def dwconv(x, w, b):
    out = lax.conv_general_dilated(x, w[:, None, :].astype(x.dtype), (1,), [(CONV_W // 2, CONV_W // 2)],
                                   dimension_numbers=('NWC', 'WIO', 'NWC'), feature_group_count=x.shape[-1])
    return out + b


def blockdiag(x, w):
    B, T, _ = x.shape
    y = jnp.einsum('btnj,njk->btnk', x.reshape(B, T, QKV_NBLK, QKV_BLOCK), w)
    return y.reshape(B, T, ML_INNER)


def mlstm_features(x_m, conv_w, conv_b, w_qblk, w_kblk, w_vblk, w_gate, b_gate):
    B, T, _ = x_m.shape
    x_c = jax.nn.silu(dwconv(x_m, conv_w, conv_b))
    q = blockdiag(x_c, w_qblk)
    k = blockdiag(x_c, w_kblk)
    v = blockdiag(x_m, w_vblk)
    gates = (jnp.concatenate([q, k, v], -1) @ w_gate + b_gate).reshape(B, T, 2, 2, ML_HEADS)
    gates = gates.astype(jnp.float32).transpose(2, 3, 0, 4, 1)
    heads = lambda a: a.reshape(B, T, ML_HEADS, ML_HEAD_DIM).transpose(0, 2, 1, 3).astype(jnp.float32)
    return heads(q), heads(k) * (ML_HEAD_DIM ** -0.5), heads(v), gates, x_c


def mlstm_chunkwise(q, k, v, ig, lf, state):
    B, H, T, dh = q.shape
    nc = T // CHUNK
    ch = lambda a: jnp.moveaxis(a.reshape(B, H, nc, CHUNK, *a.shape[3:]), 2, 0)
    mask = jnp.tril(jnp.ones((CHUNK, CHUNK), dtype=bool))

    def step(carry, xs):
        C, n, m = carry
        qc, kc, vc, ic, fc = xs
        b = jnp.cumsum(fc, axis=-1)
        dmat = jnp.where(mask, b[..., :, None] - b[..., None, :] + ic[..., None, :], -jnp.inf)
        inter = b + m[..., None]
        m_t = jnp.maximum(inter, jnp.max(dmat, axis=-1))
        w_inter = jnp.exp(inter - m_t)
        s = jnp.einsum('bhtd,bhsd->bhts', qc, kc) * jnp.exp(dmat - m_t[..., None])
        num = jnp.einsum('bhts,bhsd->bhtd', s, vc) + w_inter[..., None] * jnp.einsum('bhvd,bhtd->bhtv', C, qc)
        den = jnp.sum(s, axis=-1) + w_inter * jnp.einsum('bhd,bhtd->bht', n, qc)
        h = num / jnp.maximum(jnp.abs(den), jnp.exp(-m_t))[..., None]
        b_last = b[..., -1]
        dec = b_last[..., None] - b + ic
        m_new = jnp.maximum(b_last + m, jnp.max(dec, axis=-1))
        wk = jnp.exp(dec - m_new[..., None])
        keep = jnp.exp(b_last + m - m_new)
        C_new = keep[..., None, None] * C + jnp.einsum('bhsv,bhsd->bhvd', vc * wk[..., None], kc)
        n_new = keep[..., None] * n + jnp.einsum('bhs,bhsd->bhd', wk, kc)
        return (C_new, n_new, m_new), h

    state, hs = lax.scan(step, state, (ch(q), ch(k), ch(v), ch(ig), ch(lf)))
    return jnp.moveaxis(hs, 0, 2).reshape(B, H, T, dh), state


def mlstm_bidir(feat_c, feat_l):
    qc, kc, vc, gc = feat_c
    ql, kl, vl, gl = feat_l
    B, H, _, dh = qc.shape
    zero = (jnp.zeros((B, H, dh, dh), jnp.float32), jnp.zeros((B, H, dh), jnp.float32),
            jnp.zeros((B, H), jnp.float32))
    lsig = jax.nn.log_sigmoid
    flip = lambda a: jnp.flip(a, axis=2)
    hc_f, st_f = mlstm_chunkwise(qc, kc, vc, gc[0, 0], lsig(gc[0, 1]), zero)
    hl_f, _ = mlstm_chunkwise(ql, kl, vl, gl[0, 0], lsig(gl[0, 1]), st_f)
    hc_b, st_b = mlstm_chunkwise(flip(qc), flip(kc), flip(vc), flip(gc[1, 0]), lsig(flip(gc[1, 1])), zero)
    hl_b, _ = mlstm_chunkwise(flip(ql), flip(kl), flip(vl), flip(gl[1, 0]), lsig(flip(gl[1, 1])), st_b)
    return hc_f + flip(hc_b), hl_f + flip(hl_b)


def mlstm_readout(h, z, x_c, ml_norm, ml_skip):
    B, H, T, dh = h.shape
    hf = h.transpose(0, 2, 1, 3)
    hn = hf * lax.rsqrt(jnp.mean(hf * hf, axis=-1, keepdims=True) + EPS)
    hn = hn.reshape(B, T, ML_INNER).astype(z.dtype) * ml_norm
    return jax.nn.sigmoid(z) * (hn + ml_skip * x_c)


def merge(y_mla, y_ml, g_mla, g_ml, w_out):
    return (jax.nn.sigmoid(g_mla) * y_mla + jax.nn.sigmoid(g_ml) * y_ml) @ w_out


def mixer(h_c, h_l, angles, last, w_in, q_norm, w_uq, kv_norm, w_ukv, conv_w, conv_b,
          w_qblk, w_kblk, w_vblk, w_gate, b_gate, ml_norm, ml_skip, w_out):
    splits = [int(s) for s in np.cumsum(IN_SIZES)[:-1]]
    qlat_c, kvlat_c, krope_c, xm_c, z_c, gmla_c, gml_c = jnp.split(h_c @ w_in, splits, axis=-1)
    qlat_l, kvlat_l, krope_l, xm_l, z_l, gmla_l, gml_l = jnp.split(h_l @ w_in, splits, axis=-1)
    q_c, k_c, v_c = mla_qkv(qlat_c, kvlat_c, krope_c, q_norm, w_uq, kv_norm, w_ukv, None)
    q_l, k_l, v_l = mla_qkv(qlat_l, kvlat_l, krope_l, q_norm, w_uq, kv_norm, w_ukv, angles)
    y_mla_l = attend(q_l, jnp.concatenate([k_c, k_l], 1), jnp.concatenate([v_c, v_l], 1))
    fc = mlstm_features(xm_c, conv_w, conv_b, w_qblk, w_kblk, w_vblk, w_gate, b_gate)
    fl = mlstm_features(xm_l, conv_w, conv_b, w_qblk, w_kblk, w_vblk, w_gate, b_gate)
    hc, hl = mlstm_bidir(fc[:4], fl[:4])
    out_l = merge(y_mla_l, mlstm_readout(hl, z_l, fl[4], ml_norm, ml_skip), gmla_l, gml_l, w_out)
    if last:
        return None, out_l
    y_mla_c = attend(q_c, k_c, v_c)
    out_c = merge(y_mla_c, mlstm_readout(hc, z_c, fc[4], ml_norm, ml_skip), gmla_c, gml_c, w_out)
    return out_c, out_l


def ec_moe(x, w_router, w_e_gate, w_e_up, w_e_down):
    B, T, D = x.shape
    cap = CAP_FACTOR * T // N_EXPERTS
    aff = jax.nn.softmax((x @ w_router).astype(jnp.float32), axis=-1)
    g, idx = lax.top_k(aff.transpose(0, 2, 1), cap)
    xs = jax.vmap(lambda xb, ib: xb[ib])(x, idx)
    a = jnp.einsum('becd,edf->becf', xs, w_e_gate)
    u = jnp.einsum('becd,edf->becf', xs, w_e_up)
    y = jnp.einsum('becf,efd->becd', jax.nn.silu(a) * u, w_e_down) * g[..., None].astype(x.dtype)
    return jax.vmap(lambda yb, ib: jnp.zeros((T, D), x.dtype).at[ib.reshape(-1)].add(yb.reshape(-1, D)))(y, idx)


def setup_inputs(seed: int = 0) -> dict:
    key = jax.random.key(seed)
    ks = jax.random.split(key, 32)
    f32 = jnp.float32
    nrm = lambda k, shape, fan: jax.random.normal(k, shape, f32) * (fan ** -0.5)
    gain = lambda k, shape: 1.0 + 0.02 * jax.random.normal(k, shape, f32)
    L, D = DEPTH, D_MODEL
    b_gate = (0.1 * jax.random.normal(ks[16], (L, 2, 2, ML_HEADS), f32)
              + jnp.array([0.0, 3.0], f32)[None, None, :, None]).reshape(L, 4 * ML_HEADS)
    return {
        "x": jax.random.normal(ks[0], (BATCH, SEQ, D), f32),
        "c": jax.random.normal(ks[1], (BATCH, D), f32),
        "ctx": jax.random.normal(ks[2], (BATCH, CTX_LEN, D), f32),
        "c_ctx": jax.random.normal(ks[3], (D,), f32),
        "w_mod": nrm(ks[4], (L, D, 6 * D), D) * 0.5,
        "b_mod": 0.02 * jax.random.normal(ks[5], (L, 6 * D), f32),
        "norm1": gain(ks[6], (L, D)),
        "w_in": nrm(ks[7], (L, D, IN_DIM), D),
        "q_norm": gain(ks[8], (L, Q_LORA)),
        "w_uq": nrm(ks[9], (L, Q_LORA, MLA_HEADS * (QK_NOPE + QK_ROPE)), Q_LORA),
        "kv_norm": gain(ks[10], (L, KV_LORA)),
        "w_ukv": nrm(ks[11], (L, KV_LORA, MLA_HEADS * (QK_NOPE + V_DIM)), KV_LORA),
        "conv_w": nrm(ks[12], (L, CONV_W, ML_INNER), CONV_W),
        "conv_b": 0.02 * jax.random.normal(ks[13], (L, ML_INNER), f32),
        "w_qblk": nrm(ks[14], (L, QKV_NBLK, QKV_BLOCK, QKV_BLOCK), QKV_BLOCK),
        "w_kblk": nrm(ks[15], (L, QKV_NBLK, QKV_BLOCK, QKV_BLOCK), QKV_BLOCK),
        "w_vblk": nrm(ks[17], (L, QKV_NBLK, QKV_BLOCK, QKV_BLOCK), QKV_BLOCK),
        "w_gate": nrm(ks[18], (L, 3 * ML_INNER, 4 * ML_HEADS), 3 * ML_INNER) * 0.5,
        "b_gate": b_gate,
        "ml_norm": gain(ks[19], (L, ML_INNER)),
        "ml_skip": gain(ks[20], (L, ML_INNER)),
        "w_out": nrm(ks[21], (L, D, D), D),
        "norm2": gain(ks[22], (L, D)),
        "w_router": nrm(ks[23], (L, D, N_EXPERTS), D),
        "w_e_gate": nrm(ks[24], (L, N_EXPERTS, D, EXPERT_FF), D),
        "w_e_up": nrm(ks[25], (L, N_EXPERTS, D, EXPERT_FF), D),
        "w_e_down": nrm(ks[26], (L, N_EXPERTS, EXPERT_FF, D), EXPERT_FF),
        "final_norm": gain(ks[27], (D,)),
    }


def reference(x, c, ctx, c_ctx, w_mod, b_mod, norm1, w_in, q_norm, w_uq, kv_norm, w_ukv, conv_w, conv_b,
              w_qblk, w_kblk, w_vblk, w_gate, b_gate, ml_norm, ml_skip, w_out, norm2, w_router,
              w_e_gate, w_e_up, w_e_down, final_norm):
    angles = axial_angles(x.shape[1])
    for l in range(DEPTH):
        last = l == DEPTH - 1
        mod_l = jax.nn.silu(c) @ w_mod[l] + b_mod[l]
        mod_c = jax.nn.silu(c_ctx)[None, :] @ w_mod[l] + b_mod[l]
        sh1, sc1, g1, sh2, sc2, g2 = jnp.split(mod_l, 6, axis=-1)
        sh1c, sc1c, g1c, sh2c, sc2c, g2c = jnp.split(mod_c, 6, axis=-1)
        h_l = modulate(rmsnorm(x, norm1[l]), sh1, sc1)
        h_c = modulate(rmsnorm(ctx, norm1[l]), sh1c, sc1c)
        out_c, out_l = mixer(h_c, h_l, angles, last, w_in[l], q_norm[l], w_uq[l], kv_norm[l], w_ukv[l],
                             conv_w[l], conv_b[l], w_qblk[l], w_kblk[l], w_vblk[l], w_gate[l], b_gate[l],
                             ml_norm[l], ml_skip[l], w_out[l])
        x = x + g1[:, None, :] * out_l
        h_l = modulate(rmsnorm(x, norm2[l]), sh2, sc2)
        x = x + g2[:, None, :] * ec_moe(h_l, w_router[l], w_e_gate[l], w_e_up[l], w_e_down[l])
        if not last:
            ctx = ctx + g1c[:, None, :] * out_c
            h_c = modulate(rmsnorm(ctx, norm2[l]), sh2c, sc2c)
            ctx = ctx + g2c[:, None, :] * ec_moe(h_c, w_router[l], w_e_gate[l], w_e_up[l], w_e_down[l])
    return rmsnorm(x, final_norm)
```

```python
import functools

import jax
import jax.numpy as jnp
from jax import lax
from jax.experimental import pallas as pl
from jax.experimental.pallas import tpu as pltpu

F32 = jnp.float32
BF16 = jnp.bfloat16
I32 = jnp.int32

D_MODEL = 1024
GRID_W = 64
MLA_HEADS = 8
QK_NOPE = 128
QK_ROPE = 64
V_DIM = 128
Q_LORA = 384
KV_LORA = 256
ROPE_BASE = 10000.0
ROPE_PAIRS = QK_ROPE // 4
ATTN_SCALE = (QK_NOPE + QK_ROPE) ** -0.5
Q_SCALE = ATTN_SCALE * 1.4426950408889634
ML_HEADS = 4
ML_INNER = 1024
ML_HEAD_DIM = ML_INNER // ML_HEADS
QKV_BLOCK = 4
CONV_W = 5
CHUNK = 128
N_EXPERTS = 16
EXPERT_FF = 1024
CAP_FACTOR = 2
EPS = 1e-6

LANES = 128
SUBLANES = 8
BF16_ROWS = 16
ROW_TILE = 256
HEAD_W = 256
ATTN_TQ = 256
ATTN_TK = 768
MOE_TILE = 256
MOE_WIN = MOE_TILE + BF16_ROWS
VMEM_LIMIT = 56 * 1024 * 1024

_C_QLAT = 0
_C_KVLAT = Q_LORA
_C_KROPE = Q_LORA + KV_LORA
_C_XM = _C_KROPE + LANES
_C_Z = _C_XM + ML_INNER
_C_GMLA = _C_Z + ML_INNER
_C_GML = _C_GMLA + D_MODEL
IN_PAD = _C_GML + D_MODEL

_NT = (((1,), (1,)), ((), ()))
_TN = (((0,), (0,)), ((), ()))


def _params(sem, vmem=VMEM_LIMIT):
    return pltpu.CompilerParams(dimension_semantics=sem, vmem_limit_bytes=vmem)


def _rms(x, g):
    return x * lax.rsqrt(jnp.mean(x * x, axis=-1, keepdims=True) + EPS) * g


def _sigmoid(x):
    return jax.nn.sigmoid(x)


def _mod_kernel(c_ref, w_ref, b_ref, o_ref):
    c = c_ref[...]
    s = c * _sigmoid(c)
    o_ref[...] = jnp.dot(s, w_ref[...], preferred_element_type=F32,
                         precision=lax.Precision.HIGHEST) + b_ref[...]


def _modulation(cc, w_mod, b_mod):
    n = w_mod.shape[1]
    tn = 768
    return pl.pallas_call(
        _mod_kernel,
        grid=(n // tn,),
        in_specs=[pl.BlockSpec((SUBLANES, D_MODEL), lambda j: (0, 0)),
                  pl.BlockSpec((D_MODEL, tn), lambda j: (0, j)),
                  pl.BlockSpec((1, tn), lambda j: (0, j))],
        out_specs=pl.BlockSpec((SUBLANES, tn), lambda j: (0, j)),
        out_shape=jax.ShapeDtypeStruct((SUBLANES, n), F32),
        compiler_params=_params(("arbitrary",)),
        name="modulation",
    )(cc, w_mod, b_mod)


def _rope(v, cos, sin):
    lane = lax.broadcasted_iota(I32, v.shape, 1)
    partner = jnp.where(lane % 32 < 16, pltpu.roll(v, LANES - 16, 1), pltpu.roll(v, 16, 1))
    return v * cos + partner * sin


def _inproj_kernel(ctx_ref, x_ref, sh_ref, sc_ref, n1_ref, win_ref, qn_ref, wuqn_ref, wuqp_ref,
                   kvn_ref, wukv_ref, cos_ref, sin_ref,
                   q_ref, k_ref, v_ref, xm_ref, z_ref, gmla_ref, gml_ref):
    i = pl.program_id(1)
    xin = jnp.where(i == 0, ctx_ref[0], x_ref[0])
    h = _rms(xin, n1_ref[...]) * (1.0 + sc_ref[0]) + sh_ref[0]
    big = jnp.dot(h.astype(BF16), win_ref[...], preferred_element_type=F32)
    xm_ref[0] = big[:, _C_XM:_C_Z]
    z_ref[0] = big[:, _C_Z:_C_GMLA]
    gmla_ref[0] = big[:, _C_GMLA:_C_GML]
    gml_ref[0] = big[:, _C_GML:IN_PAD]

    cos = cos_ref[...]
    sin = sin_ref[...]
    qn = _rms(big[:, _C_QLAT:_C_KVLAT], qn_ref[...]).astype(BF16)
    q_nope = jnp.dot(qn, wuqn_ref[...], preferred_element_type=F32)
    q_pe = jnp.dot(qn, wuqp_ref[...], preferred_element_type=F32)
    kvn = _rms(big[:, _C_KVLAT:_C_KROPE], kvn_ref[...]).astype(BF16)
    kv = jnp.dot(kvn, wukv_ref[...], preferred_element_type=F32)
    k_pe = _rope(big[:, _C_KROPE:_C_XM], cos, sin).astype(BF16)
    ones_col = jnp.where(lax.broadcasted_iota(I32, (big.shape[0], HEAD_W - V_DIM), 1) == 0, 1.0, 0.0).astype(BF16)
    for hh in range(MLA_HEADS):
        lo = hh * HEAD_W
        nope = slice(hh * QK_NOPE, (hh + 1) * QK_NOPE)
        q_ref[0, :, lo:lo + QK_NOPE] = (q_nope[:, nope] * Q_SCALE).astype(BF16)
        q_ref[0, :, lo + QK_NOPE:lo + HEAD_W] = (
            _rope(q_pe[:, hh * LANES:(hh + 1) * LANES], cos, sin) * Q_SCALE).astype(BF16)
        k_ref[0, :, lo:lo + QK_NOPE] = kv[:, nope].astype(BF16)
        k_ref[0, :, lo + QK_NOPE:lo + HEAD_W] = k_pe
        v_ref[0, :, lo:lo + V_DIM] = kv[:, MLA_HEADS * QK_NOPE + hh * V_DIM:MLA_HEADS * QK_NOPE + (hh + 1) * V_DIM].astype(BF16)
        v_ref[0, :, lo + V_DIM:lo + HEAD_W] = ones_col


def _input_projection(ctx, x, mod3, norm1, w_in_p, q_norm, wuq_nope, wuq_pe, kv_norm, wukv_p, cos_t, sin_t):
    B, T, D = x.shape
    n_lat = T // ROW_TILE
    nt = n_lat + 1
    tt = T + ROW_TILE
    const2 = lambda b, i: (0, 0)
    lat = lambda b, i: (b, jnp.maximum(i - 1, 0), 0)
    allrows = lambda b, i: (b, i, 0)
    modrow = lambda col: (lambda b, i: (jnp.where(i == 0, B, b), 0, col))
    return pl.pallas_call(
        _inproj_kernel,
        grid=(B, nt),
        in_specs=[pl.BlockSpec((1, ROW_TILE, D), lambda b, i: (b, 0, 0)),
                  pl.BlockSpec((1, ROW_TILE, D), lat),
                  pl.BlockSpec((1, 1, D), modrow(0)),
                  pl.BlockSpec((1, 1, D), modrow(1)),
                  pl.BlockSpec((1, D), const2),
                  pl.BlockSpec(w_in_p.shape, const2),
                  pl.BlockSpec((1, Q_LORA), const2),
                  pl.BlockSpec(wuq_nope.shape, const2),
                  pl.BlockSpec(wuq_pe.shape, const2),
                  pl.BlockSpec((1, KV_LORA), const2),
                  pl.BlockSpec(wukv_p.shape, const2),
                  pl.BlockSpec((ROW_TILE, LANES), lambda b, i: (i, 0)),
                  pl.BlockSpec((ROW_TILE, LANES), lambda b, i: (i, 0))],
        out_specs=[pl.BlockSpec((1, ROW_TILE, MLA_HEADS * HEAD_W), lat),
                   pl.BlockSpec((1, ROW_TILE, MLA_HEADS * HEAD_W), allrows),
                   pl.BlockSpec((1, ROW_TILE, MLA_HEADS * HEAD_W), allrows),
                   pl.BlockSpec((1, ROW_TILE, ML_INNER), allrows),
                   pl.BlockSpec((1, ROW_TILE, ML_INNER), lat),
                   pl.BlockSpec((1, ROW_TILE, D), lat),
                   pl.BlockSpec((1, ROW_TILE, D), lat)],
        out_shape=[jax.ShapeDtypeStruct((B, T, MLA_HEADS * HEAD_W), BF16),
                   jax.ShapeDtypeStruct((B, tt, MLA_HEADS * HEAD_W), BF16),
                   jax.ShapeDtypeStruct((B, tt, MLA_HEADS * HEAD_W), BF16),
                   jax.ShapeDtypeStruct((B, tt, ML_INNER), F32),
                   jax.ShapeDtypeStruct((B, T, ML_INNER), F32),
                   jax.ShapeDtypeStruct((B, T, D), F32),
                   jax.ShapeDtypeStruct((B, T, D), F32)],
        compiler_params=_params(("arbitrary", "arbitrary")),
        name="input_projection",
    )(ctx, x, mod3, mod3, norm1, w_in_p, q_norm, wuq_nope, wuq_pe, kv_norm, wukv_p, cos_t, sin_t)


def _feat_kernel(prev_ref, cur_ref, next_ref, cw_ref, cb_ref, wq_ref, wk_ref, wv_ref, wg_ref, bg_ref,
                 q_ref, k_ref, v_ref, xc_ref, g_ref, *, n_tiles):
    i = pl.program_id(1)
    cur = cur_ref[0]
    prev = jnp.where(i <= 1, 0.0, prev_ref[0])
    nxt = jnp.where((i == 0) | (i == n_tiles - 1), 0.0, next_ref[0])
    xx = jnp.concatenate([prev, cur, nxt], axis=0)
    rows = cur.shape[0]
    acc = jnp.broadcast_to(cb_ref[...], cur.shape)
    for w in range(CONV_W):
        lo = SUBLANES - CONV_W // 2 + w
        acc = acc + xx[lo:lo + rows] * cw_ref[w:w + 1, :]
    xc = acc * _sigmoid(acc)
    xc_ref[0] = xc
    xcb = xc.astype(BF16)
    q = jnp.dot(xcb, wq_ref[...], preferred_element_type=F32)
    k = jnp.dot(xcb, wk_ref[...], preferred_element_type=F32)
    v = jnp.dot(cur.astype(BF16), wv_ref[...], preferred_element_type=F32)
    qb, kb, vb = q.astype(BF16), k.astype(BF16), v.astype(BF16)
    q_ref[0] = qb
    k_ref[0] = (k * (ML_HEAD_DIM ** -0.5)).astype(BF16)
    v_ref[0] = vb
    g = (lax.dot_general(wg_ref[0], qb, _NT, preferred_element_type=F32)
         + lax.dot_general(wg_ref[1], kb, _NT, preferred_element_type=F32)
         + lax.dot_general(wg_ref[2], vb, _NT, preferred_element_type=F32)) + bg_ref[...]
    row = lax.broadcasted_iota(I32, g.shape, 0)
    is_forget = (row % (2 * ML_HEADS)) >= ML_HEADS
    g_ref[0] = jnp.where(is_forget, jax.nn.log_sigmoid(g), g)


def _mlstm_features(xm, conv_w8, conv_b, wq_bd, wk_bd, wv_bd, wg3, bg_col):
    B, tt, C = xm.shape
    nt = tt // ROW_TILE
    per = ROW_TILE // SUBLANES
    last8 = tt // SUBLANES - 1
    const2 = lambda b, i: (0, 0)
    rows = lambda b, i: (b, i, 0)
    ng = 4 * ML_HEADS
    return pl.pallas_call(
        functools.partial(_feat_kernel, n_tiles=nt),
        grid=(B, nt),
        in_specs=[pl.BlockSpec((1, SUBLANES, C), lambda b, i: (b, jnp.maximum(i * per - 1, 0), 0)),
                  pl.BlockSpec((1, ROW_TILE, C), rows),
                  pl.BlockSpec((1, SUBLANES, C), lambda b, i: (b, jnp.minimum((i + 1) * per, last8), 0)),
                  pl.BlockSpec((SUBLANES, C), const2),
                  pl.BlockSpec((1, C), const2),
                  pl.BlockSpec((C, C), const2),
                  pl.BlockSpec((C, C), const2),
                  pl.BlockSpec((C, C), const2),
                  pl.BlockSpec((3, ng, C), lambda b, i: (0, 0, 0)),
                  pl.BlockSpec((ng, 1), const2)],
        out_specs=[pl.BlockSpec((1, ROW_TILE, C), rows),
                   pl.BlockSpec((1, ROW_TILE, C), rows),
                   pl.BlockSpec((1, ROW_TILE, C), rows),
                   pl.BlockSpec((1, ROW_TILE, C), rows),
                   pl.BlockSpec((1, ng, ROW_TILE), lambda b, i: (b, 0, i))],
        out_shape=[jax.ShapeDtypeStruct((B, tt, C), BF16),
                   jax.ShapeDtypeStruct((B, tt, C), BF16),
                   jax.ShapeDtypeStruct((B, tt, C), BF16),
                   jax.ShapeDtypeStruct((B, tt, C), F32),
                   jax.ShapeDtypeStruct((B, ng, tt), F32)],
        compiler_params=_params(("arbitrary", "arbitrary")),
        name="mlstm_features",
    )(xm, xm, xm, conv_w8, conv_b, wq_bd, wk_bd, wv_bd, wg3, bg_col)


def _attn_kernel(q_ref, k_ref, v_ref, o_ref, sa_ref, sb_ref, *, n_chunks):
    q = q_ref[0]
    tq = q.shape[0]

    def scores(c):
        off = pl.multiple_of(c * ATTN_TK, ATTN_TK)
        return lax.dot_general(q, k_ref[0, pl.ds(off, ATTN_TK), :], _NT, preferred_element_type=F32)

    def update(c, s, carry):
        m, acc = carry
        off = pl.multiple_of(c * ATTN_TK, ATTN_TK)
        m_new = jnp.maximum(m, jnp.max(s, axis=-1, keepdims=True))
        alpha = jnp.exp2(m - m_new)
        p = jnp.exp2(s - m_new).astype(BF16)
        acc = alpha * acc + jnp.dot(p, v_ref[0, pl.ds(off, ATTN_TK), :], preferred_element_type=F32)
        return m_new, acc

    sa_ref[...] = scores(0)

    def body(i, carry):
        c = 2 * i
        sb_ref[...] = scores(c + 1)
        carry = update(c, sa_ref[...], carry)
        sa_ref[...] = scores(c + 2)
        return update(c + 1, sb_ref[...], carry)

    init = (jnp.full((tq, 1), -jnp.inf, F32), jnp.zeros((tq, HEAD_W), F32))
    carry = lax.fori_loop(0, (n_chunks - 1) // 2, body, init)
    _, acc = update(n_chunks - 1, sa_ref[...], carry)
    o_ref[0] = acc[:, :V_DIM] / acc[:, V_DIM:V_DIM + 1]


def _attention(q, k, v):
    B, T, _ = q.shape
    tt = k.shape[1]
    n_chunks = tt // ATTN_TK
    assert n_chunks * ATTN_TK == tt and n_chunks % 2 == 1
    return pl.pallas_call(
        functools.partial(_attn_kernel, n_chunks=n_chunks),
        grid=(B, MLA_HEADS, T // ATTN_TQ),
        in_specs=[pl.BlockSpec((1, ATTN_TQ, HEAD_W), lambda b, h, i: (b, i, h)),
                  pl.BlockSpec((1, tt, HEAD_W), lambda b, h, i: (b, 0, h)),
                  pl.BlockSpec((1, tt, HEAD_W), lambda b, h, i: (b, 0, h))],
        out_specs=pl.BlockSpec((1, ATTN_TQ, V_DIM), lambda b, h, i: (b, i, h)),
        out_shape=jax.ShapeDtypeStruct((B, T, MLA_HEADS * V_DIM), F32),
        scratch_shapes=[pltpu.VMEM((ATTN_TQ, ATTN_TK), F32), pltpu.VMEM((ATTN_TQ, ATTN_TK), F32)],
        compiler_params=_params(("arbitrary", "arbitrary", "arbitrary")),
        name="attention",
    )(q, k, v)


def _lane_cumsum(x, reverse):
    lane = lax.broadcasted_iota(I32, x.shape, 1)
    n = x.shape[1]
    s = 1
    while s < n:
        if reverse:
            x = x + jnp.where(lane < n - s, pltpu.roll(x, n - s, 1), 0.0)
        else:
            x = x + jnp.where(lane >= s, pltpu.roll(x, s, 1), 0.0)
        s *= 2
    return x


def _scan_kernel(gf_ref, gb_ref, qf_ref, kf_ref, vf_ref, qb_ref, kb_ref, vb_ref,
                 hf_ref, hb_ref, c_ref, n_ref, m_ref):
    j = pl.program_id(1)
    L = CHUNK
    dh = ML_HEAD_DIM

    @pl.when(j == 0)
    def _():
        c_ref[...] = jnp.zeros_like(c_ref)
        n_ref[...] = jnp.zeros_like(n_ref)
        m_ref[...] = jnp.zeros_like(m_ref)

    tpos = lax.broadcasted_iota(I32, (L, L), 0)
    spos = lax.broadcasted_iota(I32, (L, L), 1)
    dirs = ((gf_ref, qf_ref, kf_ref, vf_ref, hf_ref), (gb_ref, qb_ref, kb_ref, vb_ref, hb_ref))
    for d, (g_ref, q_ref, k_ref, v_ref, h_ref) in enumerate(dirs):
        reverse = d == 1
        mask = (spos >= tpos) if reverse else (spos <= tpos)
        g = g_ref[0]
        ig4 = g[d * 2 * ML_HEADS:d * 2 * ML_HEADS + ML_HEADS]
        lf4 = g[d * 2 * ML_HEADS + ML_HEADS:(d + 1) * 2 * ML_HEADS]
        b4 = _lane_cumsum(lf4, reverse)
        rows = jnp.concatenate([b4, ig4, jnp.zeros((L - 2 * ML_HEADS, L), F32)], axis=0)
        cols = rows.T
        for hh in range(ML_HEADS):
            ci = d * ML_HEADS + hh
            b_row, ig_row = b4[hh:hh + 1], ig4[hh:hh + 1]
            b_col, ig_col = cols[:, hh:hh + 1], cols[:, ML_HEADS + hh:ML_HEADS + hh + 1]
            b_last = b_row[:, 0:1] if reverse else b_row[:, L - 1:L]
            m = m_ref[ci, 0:1, 0:1]
            sl = slice(hh * dh, (hh + 1) * dh)
            q, k, v = q_ref[0, :, sl], k_ref[0, :, sl], v_ref[0, :, sl]
            cm = c_ref[ci]
            nrow = n_ref[ci, 0:1, :]

            dmat = jnp.where(mask, b_col - b_row + ig_row, -jnp.inf)
            inter = b_col + m
            m_t = jnp.maximum(inter, jnp.max(dmat, axis=-1, keepdims=True))
            w_inter = jnp.exp(inter - m_t)
            s = lax.dot_general(q, k, _NT, preferred_element_type=F32) * jnp.exp(dmat - m_t)
            qc = lax.dot_general(q, cm.astype(BF16), _NT, preferred_element_type=F32)
            num = jnp.dot(s.astype(BF16), v, preferred_element_type=F32) + w_inter * qc
            qn = jnp.sum(q.astype(F32) * nrow, axis=-1, keepdims=True)
            den = jnp.sum(s, axis=-1, keepdims=True) + w_inter * qn
            h_ref[0, :, sl] = num / jnp.maximum(jnp.abs(den), jnp.exp(-m_t))

            dec_row = b_last - b_row + ig_row
            dec_col = b_last - b_col + ig_col
            m_new = jnp.maximum(b_last + m, jnp.max(dec_row, axis=-1, keepdims=True))
            wk = jnp.exp(dec_col - m_new)
            keep = jnp.exp(b_last + m - m_new)
            vw = (v.astype(F32) * wk).astype(BF16)
            c_ref[ci] = keep * cm + lax.dot_general(vw, k, _TN, preferred_element_type=F32)
            n_ref[ci, 0:1, :] = keep * nrow + jnp.sum(wk * k.astype(F32), axis=0, keepdims=True)
            m_ref[ci] = jnp.broadcast_to(m_new, m_ref.shape[1:])


def _mlstm_scan(gates, q, k, v):
    B, tt, C = q.shape
    nch = tt // CHUNK
    fwd = lambda b, j: (b, j, 0)
    ng = gates.shape[1]
    nc_ctx = ROW_TILE // CHUNK
    bidx = lambda j: jnp.where(j < nc_ctx, nc_ctx - 1 - j, nch - 1 + nc_ctx - j)
    bwd = lambda b, j: (b, bidx(j), 0)
    blk = pl.BlockSpec((1, CHUNK, C), fwd)
    blk_b = pl.BlockSpec((1, CHUNK, C), bwd)
    nchain = 2 * ML_HEADS
    return pl.pallas_call(
        _scan_kernel,
        grid=(B, nch),
        in_specs=[pl.BlockSpec((1, ng, CHUNK), lambda b, j: (b, 0, j)),
                  pl.BlockSpec((1, ng, CHUNK), lambda b, j: (b, 0, bidx(j))),
                  blk, blk, blk, blk_b, blk_b, blk_b],
        out_specs=[blk, blk_b],
        out_shape=[jax.ShapeDtypeStruct((B, tt, C), F32), jax.ShapeDtypeStruct((B, tt, C), F32)],
        scratch_shapes=[pltpu.VMEM((nchain, ML_HEAD_DIM, ML_HEAD_DIM), F32),
                        pltpu.VMEM((nchain, SUBLANES, ML_HEAD_DIM), F32),
                        pltpu.VMEM((nchain, SUBLANES, LANES), F32)],
        compiler_params=_params(("arbitrary", "arbitrary")),
        name="mlstm_scan",
    )(gates, gates, q, k, v, q, k, v)


def _merge_kernel(hf_ref, hb_ref, z_ref, xc_ref, gmla_ref, gml_ref, ymla_ref, x_ref,
                  g1_ref, sh2_ref, sc2_ref, mln_ref, mls_ref, wout_ref, n2_ref, wr_ref,
                  x1_ref, h2_ref, aff_ref):
    h = hf_ref[0] + hb_ref[0]
    parts = []
    for hh in range(ML_HEADS):
        seg = h[:, hh * ML_HEAD_DIM:(hh + 1) * ML_HEAD_DIM]
        parts.append(seg * lax.rsqrt(jnp.mean(seg * seg, axis=-1, keepdims=True) + EPS))
    hn = jnp.concatenate(parts, axis=-1) * mln_ref[...]
    y_ml = _sigmoid(z_ref[0]) * (hn + mls_ref[...] * xc_ref[0])
    merged = _sigmoid(gmla_ref[0]) * ymla_ref[0] + _sigmoid(gml_ref[0]) * y_ml
    out = jnp.dot(merged.astype(BF16), wout_ref[...], preferred_element_type=F32)
    x1 = x_ref[0] + g1_ref[0] * out
    x1_ref[0] = x1
    h2 = _rms(x1, n2_ref[...]) * (1.0 + sc2_ref[0]) + sh2_ref[0]
    h2_ref[0] = h2.astype(BF16)
    logits = lax.dot_general(wr_ref[...], h2, _NT, preferred_element_type=F32,
                             precision=lax.Precision.HIGHEST)
    e = jnp.exp(logits - jnp.max(logits, axis=0, keepdims=True))
    aff_ref[0] = e / jnp.sum(e, axis=0, keepdims=True)


def _merge(hf, hb, z, xc, gmla, gml, ymla, x, mod3, ml_norm, ml_skip, w_out, norm2, w_router_t):
    B, T, D = x.shape
    nt = T // ROW_TILE
    const2 = lambda b, i: (0, 0)
    lat = lambda b, i: (b, i, 0)
    shifted = lambda b, i: (b, i + 1, 0)
    modcol = lambda col: (lambda b, i: (b, 0, col))
    tile = lambda idx: pl.BlockSpec((1, ROW_TILE, D), idx)
    return pl.pallas_call(
        _merge_kernel,
        grid=(B, nt),
        in_specs=[tile(shifted), tile(shifted), tile(lat), tile(shifted), tile(lat), tile(lat), tile(lat),
                  tile(lat),
                  pl.BlockSpec((1, 1, D), modcol(2)), pl.BlockSpec((1, 1, D), modcol(3)),
                  pl.BlockSpec((1, 1, D), modcol(4)),
                  pl.BlockSpec((1, D), const2), pl.BlockSpec((1, D), const2),
                  pl.BlockSpec((D, D), const2), pl.BlockSpec((1, D), const2),
                  pl.BlockSpec((N_EXPERTS, D), const2)],
        out_specs=[tile(lat), tile(lat), pl.BlockSpec((1, N_EXPERTS, ROW_TILE), lambda b, i: (b, 0, i))],
        out_shape=[jax.ShapeDtypeStruct((B, T, D), F32),
                   jax.ShapeDtypeStruct((B, T, D), BF16),
                   jax.ShapeDtypeStruct((B, N_EXPERTS, T), F32)],
        compiler_params=_params(("arbitrary", "arbitrary")),
        name="merge_router",
    )(hf, hb, z, xc, gmla, gml, ymla, x, mod3, mod3, mod3, ml_norm, ml_skip, w_out, norm2, w_router_t)


def _chunked_cumsum(mask_f, tri):
    n_e, t = mask_f.shape
    off = jnp.zeros((n_e, 1), F32)
    outs = []
    for c in range(t // LANES):
        x = mask_f[:, c * LANES:(c + 1) * LANES]
        inc = jnp.dot(x.astype(BF16), tri, preferred_element_type=F32)
        outs.append(inc - x + off)
        off = off + inc[:, LANES - 1:LANES]
    return jnp.concatenate(outs, axis=1)


def _select_kernel(aff_ref, pos_ref, cs_ref, *, cap):
    aff = aff_ref[0]
    n_e = aff.shape[0]

    def count_ge(t):
        return jnp.sum(jnp.where(aff >= t, 1.0, 0.0), axis=1, keepdims=True)

    def body(carry):
        lo, hi, _ = carry
        mid = 0.5 * (lo + hi)
        ok = count_ge(mid) >= cap
        lo, hi = jnp.where(ok, mid, lo), jnp.where(ok, hi, mid)
        mid = 0.5 * (lo + hi)
        return lo, hi, jnp.max(jnp.where((mid > lo) & (mid < hi), 1.0, 0.0))

    lo, hi, _ = lax.while_loop(lambda carry: carry[2] > 0.5, body,
                               (jnp.zeros((n_e, 1), F32), jnp.full((n_e, 1), 2.0, F32), jnp.float32(1.0)))
    gt = jnp.where(aff >= hi, 1.0, 0.0)
    eq = jnp.where(aff >= lo, 1.0, 0.0) - gt
    need = cap - jnp.sum(gt, axis=1, keepdims=True)
    tri = jnp.where(lax.broadcasted_iota(I32, (LANES, LANES), 0) <= lax.broadcasted_iota(I32, (LANES, LANES), 1),
                    1.0, 0.0).astype(BF16)
    eq_rank = _chunked_cumsum(eq, tri)
    sel = gt + eq * jnp.where(eq_rank < need, 1.0, 0.0)
    cs = _chunked_cumsum(sel, tri)
    cs_ref[0] = cs.astype(I32)
    pos_ref[0] = jnp.where(sel > 0.5, cs, -1.0).astype(I32)


def _select(aff_t, cap):
    B, n_e, T = aff_t.shape
    blk = pl.BlockSpec((1, n_e, T), lambda b: (b, 0, 0))
    return pl.pallas_call(
        functools.partial(_select_kernel, cap=cap),
        grid=(B,),
        in_specs=[blk],
        out_specs=[blk, blk],
        out_shape=[jax.ShapeDtypeStruct((B, n_e, T), I32), jax.ShapeDtypeStruct((B, n_e, T), I32)],
        compiler_params=_params(("arbitrary",)),
        name="expert_select",
    )(aff_t)


def _window_start(base_ref, flat):
    start = base_ref[flat]
    return pl.multiple_of((start // BF16_ROWS) * BF16_ROWS, BF16_ROWS)


def _expert_kernel(base_ref, x_ref, pos_ref, wg_ref, wu_ref, wd_ref, y_ref, xs_ref, *, n_tiles, cap):
    e, b, k = pl.program_id(0), pl.program_id(1), pl.program_id(2)

    @pl.when(k == 0)
    def _():
        xs_ref[...] = jnp.zeros_like(xs_ref)

    s16 = _window_start(base_ref, (b * N_EXPERTS + e) * n_tiles + k)
    rel = pos_ref[0] - s16
    onehot = jnp.where(lax.broadcasted_iota(I32, (MOE_WIN, MOE_TILE), 0) == rel, 1.0, 0.0).astype(BF16)
    xs_ref[pl.ds(s16, MOE_WIN), :] += jnp.dot(onehot, x_ref[0], preferred_element_type=F32)

    @pl.when(k == n_tiles - 1)
    def _():
        for r in range(cap // ROW_TILE):
            rows = slice(r * ROW_TILE, (r + 1) * ROW_TILE)
            xs = xs_ref[rows, :].astype(BF16)
            a = jnp.dot(xs, wg_ref[0], preferred_element_type=F32)
            u = jnp.dot(xs, wu_ref[0], preferred_element_type=F32)
            hm = (a * _sigmoid(a) * u).astype(BF16)
            y_ref[0, 0, rows, :] = jnp.dot(hm, wd_ref[0], preferred_element_type=F32).astype(BF16)
        y_ref[0, 0, cap:, :] = jnp.zeros((y_ref.shape[2] - cap, y_ref.shape[3]), BF16)


def _experts(base, h2, pos3, wg, wu, wd, cap):
    B, T, D = h2.shape
    nk = T // MOE_TILE
    yr = cap + MOE_WIN
    ff = wg.shape[2]
    wspec = lambda shape: pl.BlockSpec((1,) + shape, lambda e, b, k, base: (e, 0, 0))
    grid_spec = pltpu.PrefetchScalarGridSpec(
        num_scalar_prefetch=1,
        grid=(N_EXPERTS, B, nk),
        in_specs=[pl.BlockSpec((1, MOE_TILE, D), lambda e, b, k, base: (b, k, 0)),
                  pl.BlockSpec((1, 1, MOE_TILE), lambda e, b, k, base: (b * N_EXPERTS + e, 0, k)),
                  wspec((D, ff)), wspec((D, ff)), wspec((ff, D))],
        out_specs=pl.BlockSpec((1, 1, yr, D), lambda e, b, k, base: (b, e, 0, 0)),
        scratch_shapes=[pltpu.VMEM((yr, D), F32)])
    return pl.pallas_call(
        functools.partial(_expert_kernel, n_tiles=nk, cap=cap),
        grid_spec=grid_spec,
        out_shape=jax.ShapeDtypeStruct((B, N_EXPERTS, yr, D), BF16),
        compiler_params=_params(("arbitrary", "arbitrary", "arbitrary")),
        name="expert_ffn",
    )(base, h2, pos3, wg, wu, wd)


def _combine_kernel(base_ref, y_ref, pos_ref, g_ref, x1_ref, g2_ref, fn_ref, o_ref, *, n_tiles, tiles_per_blk):
    b, th, e, k = pl.program_id(0), pl.program_id(1), pl.program_id(2), pl.program_id(3)
    kt = th * tiles_per_blk + k
    s16 = _window_start(base_ref, (b * N_EXPERTS + e) * n_tiles + kt)
    lane = lax.broadcasted_iota(I32, pos_ref.shape[1:], 1)
    pick = lane == e
    pcol = jnp.sum(jnp.where(pick, pos_ref[0].astype(F32), 0.0), axis=1, keepdims=True)
    gcol = jnp.sum(jnp.where(pick, g_ref[0], 0.0), axis=1, keepdims=True)
    rel = pcol - s16.astype(F32)
    onehot = jnp.where(lax.broadcasted_iota(I32, (MOE_TILE, MOE_WIN), 1).astype(F32) == rel,
                       1.0, 0.0).astype(BF16)
    contrib = jnp.dot(onehot, y_ref[0, 0, pl.ds(s16, MOE_WIN), :], preferred_element_type=F32) * gcol
    rows = pl.ds(pl.multiple_of(k * MOE_TILE, MOE_TILE), MOE_TILE)

    @pl.when(e == 0)
    def _():
        o_ref[0, rows, :] = contrib

    @pl.when(e > 0)
    def _():
        o_ref[0, rows, :] += contrib

    @pl.when(e == N_EXPERTS - 1)
    def _():
        x2 = x1_ref[0] + g2_ref[0] * o_ref[0, rows, :]
        o_ref[0, rows, :] = _rms(x2, fn_ref[...])


def _combine(base, y, pos_t, g_t, x1, mod3, final_norm):
    B, T, D = x1.shape
    nk = T // MOE_TILE
    halves = 2
    per = nk // halves
    yr = y.shape[2]
    last = N_EXPERTS - 1
    tok = lambda b, th, e, k, base: (b, th * per + k, 0)
    grid_spec = pltpu.PrefetchScalarGridSpec(
        num_scalar_prefetch=1,
        grid=(B, halves, N_EXPERTS, per),
        in_specs=[pl.BlockSpec((1, 1, yr, D), lambda b, th, e, k, base: (b, e, 0, 0)),
                  pl.BlockSpec((1, MOE_TILE, N_EXPERTS), tok),
                  pl.BlockSpec((1, MOE_TILE, N_EXPERTS), tok),
                  pl.BlockSpec((1, MOE_TILE, D),
                               lambda b, th, e, k, base: (b, jnp.where(e == last, th * per + k, 0), 0)),
                  pl.BlockSpec((1, 1, D), lambda b, th, e, k, base: (b, 0, 5)),
                  pl.BlockSpec((1, D), lambda b, th, e, k, base: (0, 0))],
        out_specs=pl.BlockSpec((1, per * MOE_TILE, D), lambda b, th, e, k, base: (b, th, 0)))
    return pl.pallas_call(
        functools.partial(_combine_kernel, n_tiles=nk, tiles_per_blk=per),
        grid_spec=grid_spec,
        out_shape=jax.ShapeDtypeStruct((B, T, D), F32),
        compiler_params=_params(("arbitrary",) * 4),
        name="moe_combine",
    )(base, y, pos_t, g_t, x1, mod3, final_norm)


def _rope_tables(T):
    rows = T // GRID_W
    row = jnp.repeat(jnp.arange(rows, dtype=F32), GRID_W)
    col = jnp.tile(jnp.arange(GRID_W, dtype=F32), rows)
    inv = ROPE_BASE ** (-jnp.arange(ROPE_PAIRS, dtype=F32) / ROPE_PAIRS)
    ar, ac = row[:, None] * inv, col[:, None] * inv
    ones = jnp.ones((T, LANES - QK_ROPE), F32)
    cos = jnp.concatenate([jnp.cos(ar), jnp.cos(ar), jnp.cos(ac), jnp.cos(ac), ones], axis=1)
    sin = jnp.concatenate([-jnp.sin(ar), jnp.sin(ar), -jnp.sin(ac), jnp.sin(ac), 0.0 * ones], axis=1)
    cos = jnp.concatenate([jnp.ones((ROW_TILE, LANES), F32), cos], axis=0)
    sin = jnp.concatenate([jnp.zeros((ROW_TILE, LANES), F32), sin], axis=0)
    return cos, sin


def _blockdiag_dense(w):
    n, bs, _ = w.shape
    rows = jnp.broadcast_to(w.transpose(1, 0, 2).reshape(1, bs, n * bs), (n, bs, n * bs)).reshape(n * bs, n * bs)
    r = jnp.arange(n * bs) // bs
    return jnp.where(r[:, None] == r[None, :], rows, 0.0).astype(BF16)


def kernel(x, c, ctx, c_ctx, w_mod, b_mod, norm1, w_in, q_norm, w_uq, kv_norm, w_ukv, conv_w, conv_b,
           w_qblk, w_kblk, w_vblk, w_gate, b_gate, ml_norm, ml_skip, w_out, norm2, w_router,
           w_e_gate, w_e_up, w_e_down, final_norm):
    B, T, D = x.shape
    assert w_mod.shape[0] == 1 and D == D_MODEL and ctx.shape[1] == ROW_TILE
    cap = CAP_FACTOR * T // N_EXPERTS

    cc = jnp.zeros((SUBLANES, D), F32).at[:B].set(c).at[B].set(c_ctx)
    mod = _modulation(cc, w_mod[0], b_mod[0].reshape(1, -1))
    mod3 = mod[:B + 1].reshape(B + 1, 1, 6 * D)

    wi = w_in[0]
    zpad = jnp.zeros((D, LANES - QK_ROPE), F32)
    w_in_p = jnp.concatenate([wi[:, :_C_KROPE + QK_ROPE], zpad, wi[:, _C_KROPE + QK_ROPE:]], axis=1).astype(BF16)
    wuq = w_uq[0].reshape(Q_LORA, MLA_HEADS, QK_NOPE + QK_ROPE)
    wuq_nope = wuq[:, :, :QK_NOPE].reshape(Q_LORA, MLA_HEADS * QK_NOPE).astype(BF16)
    wuq_pe = jnp.pad(wuq[:, :, QK_NOPE:], ((0, 0), (0, 0), (0, LANES - QK_ROPE))).reshape(
        Q_LORA, MLA_HEADS * LANES).astype(BF16)
    wukv = w_ukv[0].reshape(KV_LORA, MLA_HEADS, QK_NOPE + V_DIM)
    wukv_p = jnp.concatenate([wukv[:, :, :QK_NOPE].reshape(KV_LORA, -1),
                              wukv[:, :, QK_NOPE:].reshape(KV_LORA, -1)], axis=1).astype(BF16)
    cos_t, sin_t = _rope_tables(T)

    q, k, v, xm, z, gmla, gml = _input_projection(
        ctx, x, mod3, norm1, w_in_p, q_norm, wuq_nope, wuq_pe, kv_norm, wukv_p, cos_t, sin_t)

    conv_w8 = jnp.zeros((SUBLANES, ML_INNER), F32).at[:CONV_W].set(conv_w[0])
    wg3 = w_gate[0].reshape(3, ML_INNER, 4 * ML_HEADS).transpose(0, 2, 1).astype(BF16)
    mq, mk, mv, xc, gates = _mlstm_features(
        xm, conv_w8, conv_b, _blockdiag_dense(w_qblk[0]), _blockdiag_dense(w_kblk[0]),
        _blockdiag_dense(w_vblk[0]), wg3, b_gate[0].reshape(-1, 1))

    y_mla = _attention(q, k, v)
    hf, hb = _mlstm_scan(gates, mq, mk, mv)

    x1, h2, aff_t = _merge(hf, hb, z, xc, gmla, gml, y_mla, x, mod3, ml_norm, ml_skip,
                           w_out[0].astype(BF16), norm2, w_router[0].T)

    pos, cs = _select(aff_t, cap)
    base = cs[:, :, ::MOE_TILE].reshape(-1)
    y = _experts(base, h2, pos.reshape(B * N_EXPERTS, 1, T), w_e_gate[0].astype(BF16),
                 w_e_up[0].astype(BF16), w_e_down[0].astype(BF16), cap)
    return _combine(base, y, pos.transpose(0, 2, 1), aff_t.transpose(0, 2, 1), x1, mod3,
                    final_norm.reshape(1, -1))
```

```python
import functools

import jax
import jax.numpy as jnp
from jax import lax
from jax.experimental import pallas as pl
from jax.experimental.pallas import tpu as pltpu

F32 = jnp.float32
BF16 = jnp.bfloat16
I32 = jnp.int32

D_MODEL = 1024
GRID_W = 64
MLA_HEADS = 8
QK_NOPE = 128
QK_ROPE = 64
V_DIM = 128
Q_LORA = 384
KV_LORA = 256
ROPE_BASE = 10000.0
ROPE_PAIRS = QK_ROPE // 4
ATTN_SCALE = (QK_NOPE + QK_ROPE) ** -0.5
Q_SCALE = ATTN_SCALE * 1.4426950408889634
ML_HEADS = 4
ML_INNER = 1024
ML_HEAD_DIM = ML_INNER // ML_HEADS
QKV_BLOCK = 4
CONV_W = 5
CHUNK = 128
N_EXPERTS = 16
EXPERT_FF = 1024
CAP_FACTOR = 2
EPS = 1e-6

LANES = 128
SUBLANES = 8
BF16_ROWS = 16
ROW_TILE = 256
HEAD_W = 256
ATTN_TQ = 256
ATTN_TK = 768
MOE_TILE = 256
MOE_WIN = MOE_TILE + BF16_ROWS
VMEM_LIMIT = 56 * 1024 * 1024

_C_QLAT = 0
_C_KVLAT = Q_LORA
_C_KROPE = Q_LORA + KV_LORA
_C_XM = _C_KROPE + LANES
_C_Z = _C_XM + ML_INNER
_C_GMLA = _C_Z + ML_INNER
_C_GML = _C_GMLA + D_MODEL
IN_PAD = _C_GML + D_MODEL

_NT = (((1,), (1,)), ((), ()))
_TN = (((0,), (0,)), ((), ()))


def _params(sem, vmem=VMEM_LIMIT):
    return pltpu.CompilerParams(dimension_semantics=sem, vmem_limit_bytes=vmem)


def _rms(x, g):
    return x * lax.rsqrt(jnp.mean(x * x, axis=-1, keepdims=True) + EPS) * g


def _sigmoid(x):
    return jax.nn.sigmoid(x)


def _mod_kernel(c_ref, w_ref, b_ref, o_ref):
    c = c_ref[...]
    s = c * _sigmoid(c)
    o_ref[...] = jnp.dot(s, w_ref[...], preferred_element_type=F32,
                         precision=lax.Precision.HIGHEST) + b_ref[...]


def _modulation(cc, w_mod, b_mod):
    n = w_mod.shape[1]
    tn = 768
    return pl.pallas_call(
        _mod_kernel,
        grid=(n // tn,),
        in_specs=[pl.BlockSpec((SUBLANES, D_MODEL), lambda j: (0, 0)),
                  pl.BlockSpec((D_MODEL, tn), lambda j: (0, j)),
                  pl.BlockSpec((1, tn), lambda j: (0, j))],
        out_specs=pl.BlockSpec((SUBLANES, tn), lambda j: (0, j)),
        out_shape=jax.ShapeDtypeStruct((SUBLANES, n), F32),
        compiler_params=_params(("arbitrary",)),
        name="modulation",
    )(cc, w_mod, b_mod)


def _rope(v, cos, sin):
    lane = lax.broadcasted_iota(I32, v.shape, 1)
    partner = jnp.where(lane % 32 < 16, pltpu.roll(v, LANES - 16, 1), pltpu.roll(v, 16, 1))
    return v * cos + partner * sin


def _inproj_kernel(ctx_ref, x_ref, sh_ref, sc_ref, n1_ref, win_ref, qn_ref, wuqn_ref, wuqp_ref,
                   kvn_ref, wukv_ref, cos_ref, sin_ref,
                   q_ref, k_ref, v_ref, xm_ref, z_ref, gmla_ref, gml_ref):
    i = pl.program_id(1)
    xin = jnp.where(i == 0, ctx_ref[0], x_ref[0])
    h = _rms(xin, n1_ref[...]) * (1.0 + sc_ref[0]) + sh_ref[0]
    big = jnp.dot(h.astype(BF16), win_ref[...], preferred_element_type=F32)
    xm_ref[0] = big[:, _C_XM:_C_Z]
    z_ref[0] = big[:, _C_Z:_C_GMLA]
    gmla_ref[0] = big[:, _C_GMLA:_C_GML]
    gml_ref[0] = big[:, _C_GML:IN_PAD]

    cos = cos_ref[...]
    sin = sin_ref[...]
    qn = _rms(big[:, _C_QLAT:_C_KVLAT], qn_ref[...]).astype(BF16)
    q_nope = jnp.dot(qn, wuqn_ref[...], preferred_element_type=F32)
    q_pe = jnp.dot(qn, wuqp_ref[...], preferred_element_type=F32)
    kvn = _rms(big[:, _C_KVLAT:_C_KROPE], kvn_ref[...]).astype(BF16)
    kv = jnp.dot(kvn, wukv_ref[...], preferred_element_type=F32)
    k_pe = _rope(big[:, _C_KROPE:_C_XM], cos, sin).astype(BF16)
    ones_col = jnp.where(lax.broadcasted_iota(I32, (big.shape[0], HEAD_W - V_DIM), 1) == 0, 1.0, 0.0).astype(BF16)
    for hh in range(MLA_HEADS):
        lo = hh * HEAD_W
        nope = slice(hh * QK_NOPE, (hh + 1) * QK_NOPE)
        q_ref[0, :, lo:lo + QK_NOPE] = (q_nope[:, nope] * Q_SCALE).astype(BF16)
        q_ref[0, :, lo + QK_NOPE:lo + HEAD_W] = (
            _rope(q_pe[:, hh * LANES:(hh + 1) * LANES], cos, sin) * Q_SCALE).astype(BF16)
        k_ref[0, :, lo:lo + QK_NOPE] = kv[:, nope].astype(BF16)
        k_ref[0, :, lo + QK_NOPE:lo + HEAD_W] = k_pe
        v_ref[0, :, lo:lo + V_DIM] = kv[:, MLA_HEADS * QK_NOPE + hh * V_DIM:MLA_HEADS * QK_NOPE + (hh + 1) * V_DIM].astype(BF16)
        v_ref[0, :, lo + V_DIM:lo + HEAD_W] = ones_col


def _input_projection(ctx, x, mod3, norm1, w_in_p, q_norm, wuq_nope, wuq_pe, kv_norm, wukv_p, cos_t, sin_t):
    B, T, D = x.shape
    n_lat = T // ROW_TILE
    nt = n_lat + 1
    tt = T + ROW_TILE
    const2 = lambda b, i: (0, 0)
    lat = lambda b, i: (b, jnp.maximum(i - 1, 0), 0)
    allrows = lambda b, i: (b, i, 0)
    modrow = lambda col: (lambda b, i: (jnp.where(i == 0, B, b), 0, col))
    return pl.pallas_call(
        _inproj_kernel,
        grid=(B, nt),
        in_specs=[pl.BlockSpec((1, ROW_TILE, D), lambda b, i: (b, 0, 0)),
                  pl.BlockSpec((1, ROW_TILE, D), lat),
                  pl.BlockSpec((1, 1, D), modrow(0)),
                  pl.BlockSpec((1, 1, D), modrow(1)),
                  pl.BlockSpec((1, D), const2),
                  pl.BlockSpec(w_in_p.shape, const2),
                  pl.BlockSpec((1, Q_LORA), const2),
                  pl.BlockSpec(wuq_nope.shape, const2),
                  pl.BlockSpec(wuq_pe.shape, const2),
                  pl.BlockSpec((1, KV_LORA), const2),
                  pl.BlockSpec(wukv_p.shape, const2),
                  pl.BlockSpec((ROW_TILE, LANES), lambda b, i: (i, 0)),
                  pl.BlockSpec((ROW_TILE, LANES), lambda b, i: (i, 0))],
        out_specs=[pl.BlockSpec((1, ROW_TILE, MLA_HEADS * HEAD_W), lat),
                   pl.BlockSpec((1, ROW_TILE, MLA_HEADS * HEAD_W), allrows),
                   pl.BlockSpec((1, ROW_TILE, MLA_HEADS * HEAD_W), allrows),
                   pl.BlockSpec((1, ROW_TILE, ML_INNER), allrows),
                   pl.BlockSpec((1, ROW_TILE, ML_INNER), lat),
                   pl.BlockSpec((1, ROW_TILE, D), lat),
                   pl.BlockSpec((1, ROW_TILE, D), lat)],
        out_shape=[jax.ShapeDtypeStruct((B, T, MLA_HEADS * HEAD_W), BF16),
                   jax.ShapeDtypeStruct((B, tt, MLA_HEADS * HEAD_W), BF16),
                   jax.ShapeDtypeStruct((B, tt, MLA_HEADS * HEAD_W), BF16),
                   jax.ShapeDtypeStruct((B, tt, ML_INNER), F32),
                   jax.ShapeDtypeStruct((B, T, ML_INNER), F32),
                   jax.ShapeDtypeStruct((B, T, D), F32),
                   jax.ShapeDtypeStruct((B, T, D), F32)],
        compiler_params=_params(("arbitrary", "arbitrary")),
        name="input_projection",
    )(ctx, x, mod3, mod3, norm1, w_in_p, q_norm, wuq_nope, wuq_pe, kv_norm, wukv_p, cos_t, sin_t)


def _feat_kernel(prev_ref, cur_ref, next_ref, cw_ref, cb_ref, wq_ref, wk_ref, wv_ref, wg_ref, bg_ref,
                 q_ref, k_ref, v_ref, xc_ref, g_ref, *, n_tiles):
    i = pl.program_id(1)
    cur = cur_ref[0]
    prev = jnp.where(i <= 1, 0.0, prev_ref[0])
    nxt = jnp.where((i == 0) | (i == n_tiles - 1), 0.0, next_ref[0])
    xx = jnp.concatenate([prev, cur, nxt], axis=0)
    rows = cur.shape[0]
    acc = jnp.broadcast_to(cb_ref[...], cur.shape)
    for w in range(CONV_W):
        lo = SUBLANES - CONV_W // 2 + w
        acc = acc + xx[lo:lo + rows] * cw_ref[w:w + 1, :]
    xc = acc * _sigmoid(acc)
    xc_ref[0] = xc
    xcb = xc.astype(BF16)
    q = jnp.dot(xcb, wq_ref[...], preferred_element_type=F32)
    k = jnp.dot(xcb, wk_ref[...], preferred_element_type=F32)
    v = jnp.dot(cur.astype(BF16), wv_ref[...], preferred_element_type=F32)
    qb, kb, vb = q.astype(BF16), k.astype(BF16), v.astype(BF16)
    q_ref[0] = qb
    k_ref[0] = (k * (ML_HEAD_DIM ** -0.5)).astype(BF16)
    v_ref[0] = vb
    g = (lax.dot_general(wg_ref[0], qb, _NT, preferred_element_type=F32)
         + lax.dot_general(wg_ref[1], kb, _NT, preferred_element_type=F32)
         + lax.dot_general(wg_ref[2], vb, _NT, preferred_element_type=F32)) + bg_ref[...]
    row = lax.broadcasted_iota(I32, g.shape, 0)
    is_forget = (row % (2 * ML_HEADS)) >= ML_HEADS
    g_ref[0] = jnp.where(is_forget, jax.nn.log_sigmoid(g), g)


def _mlstm_features(xm, conv_w8, conv_b, wq_bd, wk_bd, wv_bd, wg3, bg_col):
    B, tt, C = xm.shape
    nt = tt // ROW_TILE
    per = ROW_TILE // SUBLANES
    last8 = tt // SUBLANES - 1
    const2 = lambda b, i: (0, 0)
    rows = lambda b, i: (b, i, 0)
    ng = 4 * ML_HEADS
    return pl.pallas_call(
        functools.partial(_feat_kernel, n_tiles=nt),
        grid=(B, nt),
        in_specs=[pl.BlockSpec((1, SUBLANES, C), lambda b, i: (b, jnp.maximum(i * per - 1, 0), 0)),
                  pl.BlockSpec((1, ROW_TILE, C), rows),
                  pl.BlockSpec((1, SUBLANES, C), lambda b, i: (b, jnp.minimum((i + 1) * per, last8), 0)),
                  pl.BlockSpec((SUBLANES, C), const2),
                  pl.BlockSpec((1, C), const2),
                  pl.BlockSpec((C, C), const2),
                  pl.BlockSpec((C, C), const2),
                  pl.BlockSpec((C, C), const2),
                  pl.BlockSpec((3, ng, C), lambda b, i: (0, 0, 0)),
                  pl.BlockSpec((ng, 1), const2)],
        out_specs=[pl.BlockSpec((1, ROW_TILE, C), rows),
                   pl.BlockSpec((1, ROW_TILE, C), rows),
                   pl.BlockSpec((1, ROW_TILE, C), rows),
                   pl.BlockSpec((1, ROW_TILE, C), rows),
                   pl.BlockSpec((1, ng, ROW_TILE), lambda b, i: (b, 0, i))],
        out_shape=[jax.ShapeDtypeStruct((B, tt, C), BF16),
                   jax.ShapeDtypeStruct((B, tt, C), BF16),
                   jax.ShapeDtypeStruct((B, tt, C), BF16),
                   jax.ShapeDtypeStruct((B, tt, C), F32),
                   jax.ShapeDtypeStruct((B, ng, tt), F32)],
        compiler_params=_params(("arbitrary", "arbitrary")),
        name="mlstm_features",
    )(xm, xm, xm, conv_w8, conv_b, wq_bd, wk_bd, wv_bd, wg3, bg_col)


def _attn_kernel(q_ref, k_ref, v_ref, o_ref, sa_ref, sb_ref, pa_ref, pb_ref, mb_ref, *, n_chunks):
    j = pl.program_id(2)
    tq = ATTN_TQ

    @pl.when(j == 0)
    def _():
        sb_ref[...] = jnp.zeros_like(sb_ref)
        pa_ref[...] = jnp.ones_like(pa_ref)
        mb_ref[...] = jnp.zeros_like(mb_ref)

    def half(q, s_new, s_old, m_old, p_new, p_old):
        def body(c, carry):
            m, acc = carry
            off = pl.multiple_of(c * ATTN_TK, ATTN_TK)
            s = lax.dot_general(q, k_ref[0, pl.ds(off, ATTN_TK), :], _NT, preferred_element_type=F32)
            s_new[c] = s
            m = jnp.maximum(m, jnp.max(s, axis=-1, keepdims=True))
            p_new[c] = jnp.exp2(s_old[c] - m_old).astype(BF16)
            acc = acc + jnp.dot(p_old[c], v_ref[0, pl.ds(off, ATTN_TK), :], preferred_element_type=F32)
            return m, acc

        init = (jnp.full((tq, 1), -jnp.inf, F32), jnp.zeros((tq, HEAD_W), F32))
        m, acc = lax.fori_loop(0, n_chunks, body, init, unroll=True)
        return m, acc[:, :V_DIM] / acc[:, V_DIM:V_DIM + 1]

    m_a, out = half(q_ref[0, :tq], sa_ref, sb_ref, mb_ref[...], pb_ref, pa_ref)
    o_ref[0, :tq] = out
    m_b, out = half(q_ref[0, tq:], sb_ref, sa_ref, m_a, pa_ref, pb_ref)
    o_ref[0, tq:] = out
    mb_ref[...] = m_b


def _attention(q, k, v):
    B, T, _ = q.shape
    tt = k.shape[1]
    n_chunks = tt // ATTN_TK
    n_pairs = T // (2 * ATTN_TQ)
    assert n_chunks * ATTN_TK == tt and n_pairs * 2 * ATTN_TQ == T
    return pl.pallas_call(
        functools.partial(_attn_kernel, n_chunks=n_chunks),
        grid=(B, MLA_HEADS, n_pairs + 1),
        in_specs=[pl.BlockSpec((1, 2 * ATTN_TQ, HEAD_W), lambda b, h, j: (b, jnp.minimum(j, n_pairs - 1), h)),
                  pl.BlockSpec((1, tt, HEAD_W), lambda b, h, j: (b, 0, h)),
                  pl.BlockSpec((1, tt, HEAD_W), lambda b, h, j: (b, 0, h))],
        out_specs=pl.BlockSpec((1, 2 * ATTN_TQ, V_DIM), lambda b, h, j: (b, jnp.maximum(j - 1, 0), h)),
        out_shape=jax.ShapeDtypeStruct((B, T, MLA_HEADS * V_DIM), F32),
        scratch_shapes=[pltpu.VMEM((n_chunks, ATTN_TQ, ATTN_TK), F32),
                        pltpu.VMEM((n_chunks, ATTN_TQ, ATTN_TK), F32),
                        pltpu.VMEM((n_chunks, ATTN_TQ, ATTN_TK), BF16),
                        pltpu.VMEM((n_chunks, ATTN_TQ, ATTN_TK), BF16),
                        pltpu.VMEM((ATTN_TQ, 1), F32)],
        compiler_params=_params(("arbitrary", "arbitrary", "arbitrary")),
        name="attention",
    )(q, k, v)


def _lane_cumsum(x, reverse):
    lane = lax.broadcasted_iota(I32, x.shape, 1)
    n = x.shape[1]
    s = 1
    while s < n:
        if reverse:
            x = x + jnp.where(lane < n - s, pltpu.roll(x, n - s, 1), 0.0)
        else:
            x = x + jnp.where(lane >= s, pltpu.roll(x, s, 1), 0.0)
        s *= 2
    return x


def _scan_kernel(gf_ref, gb_ref, qf_ref, kf_ref, vf_ref, qb_ref, kb_ref, vb_ref,
                 hf_ref, hb_ref, c_ref, n_ref, m_ref):
    j = pl.program_id(1)
    L = CHUNK
    dh = ML_HEAD_DIM

    @pl.when(j == 0)
    def _():
        c_ref[...] = jnp.zeros_like(c_ref)
        n_ref[...] = jnp.zeros_like(n_ref)
        m_ref[...] = jnp.zeros_like(m_ref)

    tpos = lax.broadcasted_iota(I32, (L, L), 0)
    spos = lax.broadcasted_iota(I32, (L, L), 1)
    dirs = ((gf_ref, qf_ref, kf_ref, vf_ref, hf_ref), (gb_ref, qb_ref, kb_ref, vb_ref, hb_ref))
    for d, (g_ref, q_ref, k_ref, v_ref, h_ref) in enumerate(dirs):
        reverse = d == 1
        mask = (spos >= tpos) if reverse else (spos <= tpos)
        g = g_ref[0]
        ig4 = g[d * 2 * ML_HEADS:d * 2 * ML_HEADS + ML_HEADS]
        lf4 = g[d * 2 * ML_HEADS + ML_HEADS:(d + 1) * 2 * ML_HEADS]
        b4 = _lane_cumsum(lf4, reverse)
        rows = jnp.concatenate([b4, ig4, jnp.zeros((L - 2 * ML_HEADS, L), F32)], axis=0)
        cols = rows.T
        for hh in range(ML_HEADS):
            ci = d * ML_HEADS + hh
            b_row, ig_row = b4[hh:hh + 1], ig4[hh:hh + 1]
            b_col, ig_col = cols[:, hh:hh + 1], cols[:, ML_HEADS + hh:ML_HEADS + hh + 1]
            b_last = b_row[:, 0:1] if reverse else b_row[:, L - 1:L]
            m = m_ref[ci, 0:1, 0:1]
            sl = slice(hh * dh, (hh + 1) * dh)
            q, k, v = q_ref[0, :, sl], k_ref[0, :, sl], v_ref[0, :, sl]
            cm = c_ref[ci]
            nrow = n_ref[ci, 0:1, :]

            dmat = jnp.where(mask, b_col - b_row + ig_row, -jnp.inf)
            inter = b_col + m
            m_t = jnp.maximum(inter, jnp.max(dmat, axis=-1, keepdims=True))
            w_inter = jnp.exp(inter - m_t)
            s = lax.dot_general(q, k, _NT, preferred_element_type=F32) * jnp.exp(dmat - m_t)
            qc = lax.dot_general(q, cm.astype(BF16), _NT, preferred_element_type=F32)
            num = jnp.dot(s.astype(BF16), v, preferred_element_type=F32) + w_inter * qc
            qn = jnp.sum(q.astype(F32) * nrow, axis=-1, keepdims=True)
            den = jnp.sum(s, axis=-1, keepdims=True) + w_inter * qn
            h_ref[0, :, sl] = num / jnp.maximum(jnp.abs(den), jnp.exp(-m_t))

            dec_row = b_last - b_row + ig_row
            dec_col = b_last - b_col + ig_col
            m_new = jnp.maximum(b_last + m, jnp.max(dec_row, axis=-1, keepdims=True))
            wk = jnp.exp(dec_col - m_new)
            keep = jnp.exp(b_last + m - m_new)
            vw = (v.astype(F32) * wk).astype(BF16)
            c_ref[ci] = keep * cm + lax.dot_general(vw, k, _TN, preferred_element_type=F32)
            n_ref[ci, 0:1, :] = keep * nrow + jnp.sum(wk * k.astype(F32), axis=0, keepdims=True)
            m_ref[ci] = jnp.broadcast_to(m_new, m_ref.shape[1:])


def _mlstm_scan(gates, q, k, v):
    B, tt, C = q.shape
    nch = tt // CHUNK
    fwd = lambda b, j: (b, j, 0)
    ng = gates.shape[1]
    nc_ctx = ROW_TILE // CHUNK
    bidx = lambda j: jnp.where(j < nc_ctx, nc_ctx - 1 - j, nch - 1 + nc_ctx - j)
    bwd = lambda b, j: (b, bidx(j), 0)
    blk = pl.BlockSpec((1, CHUNK, C), fwd)
    blk_b = pl.BlockSpec((1, CHUNK, C), bwd)
    nchain = 2 * ML_HEADS
    return pl.pallas_call(
        _scan_kernel,
        grid=(B, nch),
        in_specs=[pl.BlockSpec((1, ng, CHUNK), lambda b, j: (b, 0, j)),
                  pl.BlockSpec((1, ng, CHUNK), lambda b, j: (b, 0, bidx(j))),
                  blk, blk, blk, blk_b, blk_b, blk_b],
        out_specs=[blk, blk_b],
        out_shape=[jax.ShapeDtypeStruct((B, tt, C), F32), jax.ShapeDtypeStruct((B, tt, C), F32)],
        scratch_shapes=[pltpu.VMEM((nchain, ML_HEAD_DIM, ML_HEAD_DIM), F32),
                        pltpu.VMEM((nchain, SUBLANES, ML_HEAD_DIM), F32),
                        pltpu.VMEM((nchain, SUBLANES, LANES), F32)],
        compiler_params=_params(("arbitrary", "arbitrary")),
        name="mlstm_scan",
    )(gates, gates, q, k, v, q, k, v)


def _merge_kernel(hf_ref, hb_ref, z_ref, xc_ref, gmla_ref, gml_ref, ymla_ref, x_ref,
                  g1_ref, sh2_ref, sc2_ref, mln_ref, mls_ref, wout_ref, n2_ref, wr_ref,
                  x1_ref, h2_ref, aff_ref):
    h = hf_ref[0] + hb_ref[0]
    parts = []
    for hh in range(ML_HEADS):
        seg = h[:, hh * ML_HEAD_DIM:(hh + 1) * ML_HEAD_DIM]
        parts.append(seg * lax.rsqrt(jnp.mean(seg * seg, axis=-1, keepdims=True) + EPS))
    hn = jnp.concatenate(parts, axis=-1) * mln_ref[...]
    y_ml = _sigmoid(z_ref[0]) * (hn + mls_ref[...] * xc_ref[0])
    merged = _sigmoid(gmla_ref[0]) * ymla_ref[0] + _sigmoid(gml_ref[0]) * y_ml
    out = jnp.dot(merged.astype(BF16), wout_ref[...], preferred_element_type=F32)
    x1 = x_ref[0] + g1_ref[0] * out
    x1_ref[0] = x1
    h2 = _rms(x1, n2_ref[...]) * (1.0 + sc2_ref[0]) + sh2_ref[0]
    h2_ref[0] = h2.astype(BF16)
    logits = lax.dot_general(wr_ref[...], h2, _NT, preferred_element_type=F32,
                             precision=lax.Precision.HIGHEST)
    e = jnp.exp(logits - jnp.max(logits, axis=0, keepdims=True))
    aff_ref[0] = e / jnp.sum(e, axis=0, keepdims=True)


def _merge(hf, hb, z, xc, gmla, gml, ymla, x, mod3, ml_norm, ml_skip, w_out, norm2, w_router_t):
    B, T, D = x.shape
    nt = T // ROW_TILE
    const2 = lambda b, i: (0, 0)
    lat = lambda b, i: (b, i, 0)
    shifted = lambda b, i: (b, i + 1, 0)
    modcol = lambda col: (lambda b, i: (b, 0, col))
    tile = lambda idx: pl.BlockSpec((1, ROW_TILE, D), idx)
    return pl.pallas_call(
        _merge_kernel,
        grid=(B, nt),
        in_specs=[tile(shifted), tile(shifted), tile(lat), tile(shifted), tile(lat), tile(lat), tile(lat),
                  tile(lat),
                  pl.BlockSpec((1, 1, D), modcol(2)), pl.BlockSpec((1, 1, D), modcol(3)),
                  pl.BlockSpec((1, 1, D), modcol(4)),
                  pl.BlockSpec((1, D), const2), pl.BlockSpec((1, D), const2),
                  pl.BlockSpec((D, D), const2), pl.BlockSpec((1, D), const2),
                  pl.BlockSpec((N_EXPERTS, D), const2)],
        out_specs=[tile(lat), tile(lat), pl.BlockSpec((1, N_EXPERTS, ROW_TILE), lambda b, i: (b, 0, i))],
        out_shape=[jax.ShapeDtypeStruct((B, T, D), F32),
                   jax.ShapeDtypeStruct((B, T, D), BF16),
                   jax.ShapeDtypeStruct((B, N_EXPERTS, T), F32)],
        compiler_params=_params(("arbitrary", "arbitrary")),
        name="merge_router",
    )(hf, hb, z, xc, gmla, gml, ymla, x, mod3, mod3, mod3, ml_norm, ml_skip, w_out, norm2, w_router_t)


def _chunked_cumsum(mask_f, tri):
    n_e, t = mask_f.shape
    off = jnp.zeros((n_e, 1), F32)
    outs = []
    for c in range(t // LANES):
        x = mask_f[:, c * LANES:(c + 1) * LANES]
        inc = jnp.dot(x.astype(BF16), tri, preferred_element_type=F32)
        outs.append(inc - x + off)
        off = off + inc[:, LANES - 1:LANES]
    return jnp.concatenate(outs, axis=1)


def _select_kernel(aff_ref, pos_ref, cs_ref, *, cap):
    aff = aff_ref[0]
    n_e = aff.shape[0]

    def count_ge(t):
        return jnp.sum(jnp.where(aff >= t, 1.0, 0.0), axis=1, keepdims=True)

    def body(carry):
        lo, hi, _ = carry
        mid = 0.5 * (lo + hi)
        ok = count_ge(mid) >= cap
        lo, hi = jnp.where(ok, mid, lo), jnp.where(ok, hi, mid)
        mid = 0.5 * (lo + hi)
        return lo, hi, jnp.max(jnp.where((mid > lo) & (mid < hi), 1.0, 0.0))

    lo, hi, _ = lax.while_loop(lambda carry: carry[2] > 0.5, body,
                               (jnp.zeros((n_e, 1), F32), jnp.full((n_e, 1), 2.0, F32), jnp.float32(1.0)))
    gt = jnp.where(aff >= hi, 1.0, 0.0)
    eq = jnp.where(aff >= lo, 1.0, 0.0) - gt
    need = cap - jnp.sum(gt, axis=1, keepdims=True)
    tri = jnp.where(lax.broadcasted_iota(I32, (LANES, LANES), 0) <= lax.broadcasted_iota(I32, (LANES, LANES), 1),
                    1.0, 0.0).astype(BF16)
    eq_rank = _chunked_cumsum(eq, tri)
    sel = gt + eq * jnp.where(eq_rank < need, 1.0, 0.0)
    cs = _chunked_cumsum(sel, tri)
    cs_ref[0] = cs.astype(I32)
    pos_ref[0] = jnp.where(sel > 0.5, cs, -1.0).astype(I32)


def _select(aff_t, cap):
    B, n_e, T = aff_t.shape
    blk = pl.BlockSpec((1, n_e, T), lambda b: (b, 0, 0))
    return pl.pallas_call(
        functools.partial(_select_kernel, cap=cap),
        grid=(B,),
        in_specs=[blk],
        out_specs=[blk, blk],
        out_shape=[jax.ShapeDtypeStruct((B, n_e, T), I32), jax.ShapeDtypeStruct((B, n_e, T), I32)],
        compiler_params=_params(("arbitrary",)),
        name="expert_select",
    )(aff_t)


def _window_start(base_ref, flat):
    start = base_ref[flat]
    return pl.multiple_of((start // BF16_ROWS) * BF16_ROWS, BF16_ROWS)


def _expert_kernel(base_ref, x_ref, pos_ref, wg_ref, wu_ref, wd_ref, y_ref, xs_ref, *, n_tiles, cap):
    e, b, k = pl.program_id(0), pl.program_id(1), pl.program_id(2)

    @pl.when(k == 0)
    def _():
        xs_ref[...] = jnp.zeros_like(xs_ref)

    s16 = _window_start(base_ref, (b * N_EXPERTS + e) * n_tiles + k)
    rel = pos_ref[0] - s16
    onehot = jnp.where(lax.broadcasted_iota(I32, (MOE_WIN, MOE_TILE), 0) == rel, 1.0, 0.0).astype(BF16)
    xs_ref[pl.ds(s16, MOE_WIN), :] += jnp.dot(onehot, x_ref[0], preferred_element_type=F32)

    @pl.when(k == n_tiles - 1)
    def _():
        for r in range(cap // ROW_TILE):
            rows = slice(r * ROW_TILE, (r + 1) * ROW_TILE)
            xs = xs_ref[rows, :].astype(BF16)
            a = jnp.dot(xs, wg_ref[0], preferred_element_type=F32)
            u = jnp.dot(xs, wu_ref[0], preferred_element_type=F32)
            hm = (a * _sigmoid(a) * u).astype(BF16)
            y_ref[0, 0, rows, :] = jnp.dot(hm, wd_ref[0], preferred_element_type=F32).astype(BF16)
        y_ref[0, 0, cap:, :] = jnp.zeros((y_ref.shape[2] - cap, y_ref.shape[3]), BF16)


def _experts(base, h2, pos3, wg, wu, wd, cap):
    B, T, D = h2.shape
    nk = T // MOE_TILE
    yr = cap + MOE_WIN
    ff = wg.shape[2]
    wspec = lambda shape: pl.BlockSpec((1,) + shape, lambda e, b, k, base: (e, 0, 0))
    grid_spec = pltpu.PrefetchScalarGridSpec(
        num_scalar_prefetch=1,
        grid=(N_EXPERTS, B, nk),
        in_specs=[pl.BlockSpec((1, MOE_TILE, D), lambda e, b, k, base: (b, k, 0)),
                  pl.BlockSpec((1, 1, MOE_TILE), lambda e, b, k, base: (b * N_EXPERTS + e, 0, k)),
                  wspec((D, ff)), wspec((D, ff)), wspec((ff, D))],
        out_specs=pl.BlockSpec((1, 1, yr, D), lambda e, b, k, base: (b, e, 0, 0)),
        scratch_shapes=[pltpu.VMEM((yr, D), F32)])
    return pl.pallas_call(
        functools.partial(_expert_kernel, n_tiles=nk, cap=cap),
        grid_spec=grid_spec,
        out_shape=jax.ShapeDtypeStruct((B, N_EXPERTS, yr, D), BF16),
        compiler_params=_params(("arbitrary", "arbitrary", "arbitrary")),
        name="expert_ffn",
    )(base, h2, pos3, wg, wu, wd)


def _combine_kernel(base_ref, y_ref, pos_ref, g_ref, x1_ref, g2_ref, fn_ref, o_ref, *, n_tiles, tiles_per_blk):
    b, th, e, k = pl.program_id(0), pl.program_id(1), pl.program_id(2), pl.program_id(3)
    kt = th * tiles_per_blk + k
    s16 = _window_start(base_ref, (b * N_EXPERTS + e) * n_tiles + kt)
    lane = lax.broadcasted_iota(I32, pos_ref.shape[1:], 1)
    pick = lane == e
    pcol = jnp.sum(jnp.where(pick, pos_ref[0].astype(F32), 0.0), axis=1, keepdims=True)
    gcol = jnp.sum(jnp.where(pick, g_ref[0], 0.0), axis=1, keepdims=True)
    rel = pcol - s16.astype(F32)
    onehot = jnp.where(lax.broadcasted_iota(I32, (MOE_TILE, MOE_WIN), 1).astype(F32) == rel,
                       1.0, 0.0).astype(BF16)
    contrib = jnp.dot(onehot, y_ref[0, 0, pl.ds(s16, MOE_WIN), :], preferred_element_type=F32) * gcol
    rows = pl.ds(pl.multiple_of(k * MOE_TILE, MOE_TILE), MOE_TILE)

    @pl.when(e == 0)
    def _():
        o_ref[0, rows, :] = contrib

    @pl.when(e > 0)
    def _():
        o_ref[0, rows, :] += contrib

    @pl.when(e == N_EXPERTS - 1)
    def _():
        x2 = x1_ref[0] + g2_ref[0] * o_ref[0, rows, :]
        o_ref[0, rows, :] = _rms(x2, fn_ref[...])


def _combine(base, y, pos_t, g_t, x1, mod3, final_norm):
    B, T, D = x1.shape
    nk = T // MOE_TILE
    halves = 2
    per = nk // halves
    yr = y.shape[2]
    last = N_EXPERTS - 1
    tok = lambda b, th, e, k, base: (b, th * per + k, 0)
    grid_spec = pltpu.PrefetchScalarGridSpec(
        num_scalar_prefetch=1,
        grid=(B, halves, N_EXPERTS, per),
        in_specs=[pl.BlockSpec((1, 1, yr, D), lambda b, th, e, k, base: (b, e, 0, 0)),
                  pl.BlockSpec((1, MOE_TILE, N_EXPERTS), tok),
                  pl.BlockSpec((1, MOE_TILE, N_EXPERTS), tok),
                  pl.BlockSpec((1, MOE_TILE, D),
                               lambda b, th, e, k, base: (b, jnp.where(e == last, th * per + k, 0), 0)),
                  pl.BlockSpec((1, 1, D), lambda b, th, e, k, base: (b, 0, 5)),
                  pl.BlockSpec((1, D), lambda b, th, e, k, base: (0, 0))],
        out_specs=pl.BlockSpec((1, per * MOE_TILE, D), lambda b, th, e, k, base: (b, th, 0)))
    return pl.pallas_call(
        functools.partial(_combine_kernel, n_tiles=nk, tiles_per_blk=per),
        grid_spec=grid_spec,
        out_shape=jax.ShapeDtypeStruct((B, T, D), F32),
        compiler_params=_params(("arbitrary",) * 4),
        name="moe_combine",
    )(base, y, pos_t, g_t, x1, mod3, final_norm)


def _rope_tables(T):
    rows = T // GRID_W
    row = jnp.repeat(jnp.arange(rows, dtype=F32), GRID_W)
    col = jnp.tile(jnp.arange(GRID_W, dtype=F32), rows)
    inv = ROPE_BASE ** (-jnp.arange(ROPE_PAIRS, dtype=F32) / ROPE_PAIRS)
    ar, ac = row[:, None] * inv, col[:, None] * inv
    ones = jnp.ones((T, LANES - QK_ROPE), F32)
    cos = jnp.concatenate([jnp.cos(ar), jnp.cos(ar), jnp.cos(ac), jnp.cos(ac), ones], axis=1)
    sin = jnp.concatenate([-jnp.sin(ar), jnp.sin(ar), -jnp.sin(ac), jnp.sin(ac), 0.0 * ones], axis=1)
    cos = jnp.concatenate([jnp.ones((ROW_TILE, LANES), F32), cos], axis=0)
    sin = jnp.concatenate([jnp.zeros((ROW_TILE, LANES), F32), sin], axis=0)
    return cos, sin


def _blockdiag_dense(w):
    n, bs, _ = w.shape
    rows = jnp.broadcast_to(w.transpose(1, 0, 2).reshape(1, bs, n * bs), (n, bs, n * bs)).reshape(n * bs, n * bs)
    r = jnp.arange(n * bs) // bs
    return jnp.where(r[:, None] == r[None, :], rows, 0.0).astype(BF16)


def kernel(x, c, ctx, c_ctx, w_mod, b_mod, norm1, w_in, q_norm, w_uq, kv_norm, w_ukv, conv_w, conv_b,
           w_qblk, w_kblk, w_vblk, w_gate, b_gate, ml_norm, ml_skip, w_out, norm2, w_router,
           w_e_gate, w_e_up, w_e_down, final_norm):
    B, T, D = x.shape
    assert w_mod.shape[0] == 1 and D == D_MODEL and ctx.shape[1] == ROW_TILE
    cap = CAP_FACTOR * T // N_EXPERTS

    cc = jnp.zeros((SUBLANES, D), F32).at[:B].set(c).at[B].set(c_ctx)
    mod = _modulation(cc, w_mod[0], b_mod[0].reshape(1, -1))
    mod3 = mod[:B + 1].reshape(B + 1, 1, 6 * D)

    wi = w_in[0]
    zpad = jnp.zeros((D, LANES - QK_ROPE), F32)
    w_in_p = jnp.concatenate([wi[:, :_C_KROPE + QK_ROPE], zpad, wi[:, _C_KROPE + QK_ROPE:]], axis=1).astype(BF16)
    wuq = w_uq[0].reshape(Q_LORA, MLA_HEADS, QK_NOPE + QK_ROPE)
    wuq_nope = wuq[:, :, :QK_NOPE].reshape(Q_LORA, MLA_HEADS * QK_NOPE).astype(BF16)
    wuq_pe = jnp.pad(wuq[:, :, QK_NOPE:], ((0, 0), (0, 0), (0, LANES - QK_ROPE))).reshape(
        Q_LORA, MLA_HEADS * LANES).astype(BF16)
    wukv = w_ukv[0].reshape(KV_LORA, MLA_HEADS, QK_NOPE + V_DIM)
    wukv_p = jnp.concatenate([wukv[:, :, :QK_NOPE].reshape(KV_LORA, -1),
                              wukv[:, :, QK_NOPE:].reshape(KV_LORA, -1)], axis=1).astype(BF16)
    cos_t, sin_t = _rope_tables(T)

    q, k, v, xm, z, gmla, gml = _input_projection(
        ctx, x, mod3, norm1, w_in_p, q_norm, wuq_nope, wuq_pe, kv_norm, wukv_p, cos_t, sin_t)

    conv_w8 = jnp.zeros((SUBLANES, ML_INNER), F32).at[:CONV_W].set(conv_w[0])
    wg3 = w_gate[0].reshape(3, ML_INNER, 4 * ML_HEADS).transpose(0, 2, 1).astype(BF16)
    mq, mk, mv, xc, gates = _mlstm_features(
        xm, conv_w8, conv_b, _blockdiag_dense(w_qblk[0]), _blockdiag_dense(w_kblk[0]),
        _blockdiag_dense(w_vblk[0]), wg3, b_gate[0].reshape(-1, 1))

    y_mla = _attention(q, k, v)
    hf, hb = _mlstm_scan(gates, mq, mk, mv)

    x1, h2, aff_t = _merge(hf, hb, z, xc, gmla, gml, y_mla, x, mod3, ml_norm, ml_skip,
                           w_out[0].astype(BF16), norm2, w_router[0].T)

    pos, cs = _select(aff_t, cap)
    base = cs[:, :, ::MOE_TILE].reshape(-1)
    y = _experts(base, h2, pos.reshape(B * N_EXPERTS, 1, T), w_e_gate[0].astype(BF16),
                 w_e_up[0].astype(BF16), w_e_down[0].astype(BF16), cap)
    return _combine(base, y, pos.transpose(0, 2, 1), aff_t.transpose(0, 2, 1), x1, mod3,
                    final_norm.reshape(1, -1))
```

```python
import functools

import jax
import jax.numpy as jnp
from jax import lax
from jax.experimental import pallas as pl
from jax.experimental.pallas import tpu as pltpu

F32 = jnp.float32
BF16 = jnp.bfloat16
I32 = jnp.int32

D_MODEL = 1024
GRID_W = 64
MLA_HEADS = 8
QK_NOPE = 128
QK_ROPE = 64
V_DIM = 128
Q_LORA = 384
KV_LORA = 256
ROPE_BASE = 10000.0
ROPE_PAIRS = QK_ROPE // 4
ATTN_SCALE = (QK_NOPE + QK_ROPE) ** -0.5
Q_SCALE = ATTN_SCALE * 1.4426950408889634
ML_HEADS = 4
ML_INNER = 1024
ML_HEAD_DIM = ML_INNER // ML_HEADS
QKV_BLOCK = 4
CONV_W = 5
CHUNK = 128
N_EXPERTS = 16
EXPERT_FF = 1024
CAP_FACTOR = 2
EPS = 1e-6

LANES = 128
SUBLANES = 8
BF16_ROWS = 16
ROW_TILE = 256
HEAD_W = 256
ATTN_TQ = 256
ATTN_TK = 768
MOE_TILE = 256
MOE_WIN = MOE_TILE + BF16_ROWS
VMEM_LIMIT = 56 * 1024 * 1024

_C_QLAT = 0
_C_KVLAT = Q_LORA
_C_KROPE = Q_LORA + KV_LORA
_C_XM = _C_KROPE + LANES
_C_Z = _C_XM + ML_INNER
_C_GMLA = _C_Z + ML_INNER
_C_GML = _C_GMLA + D_MODEL
IN_PAD = _C_GML + D_MODEL

_NT = (((1,), (1,)), ((), ()))
_TN = (((0,), (0,)), ((), ()))


def _params(sem, vmem=VMEM_LIMIT):
    return pltpu.CompilerParams(dimension_semantics=sem, vmem_limit_bytes=vmem)


def _rms(x, g):
    return x * lax.rsqrt(jnp.mean(x * x, axis=-1, keepdims=True) + EPS) * g


def _sigmoid(x):
    return jax.nn.sigmoid(x)


def _mod_kernel(c_ref, w_ref, b_ref, o_ref):
    c = c_ref[...]
    s = c * _sigmoid(c)
    o_ref[...] = jnp.dot(s, w_ref[...], preferred_element_type=F32,
                         precision=lax.Precision.HIGHEST) + b_ref[...]


def _modulation(cc, w_mod, b_mod):
    n = w_mod.shape[1]
    tn = 768
    return pl.pallas_call(
        _mod_kernel,
        grid=(n // tn,),
        in_specs=[pl.BlockSpec((SUBLANES, D_MODEL), lambda j: (0, 0)),
                  pl.BlockSpec((D_MODEL, tn), lambda j: (0, j)),
                  pl.BlockSpec((1, tn), lambda j: (0, j))],
        out_specs=pl.BlockSpec((SUBLANES, tn), lambda j: (0, j)),
        out_shape=jax.ShapeDtypeStruct((SUBLANES, n), F32),
        compiler_params=_params(("arbitrary",)),
        name="modulation",
    )(cc, w_mod, b_mod)


def _rope(v, cos, sin):
    lane = lax.broadcasted_iota(I32, v.shape, 1)
    partner = jnp.where(lane % 32 < 16, pltpu.roll(v, LANES - 16, 1), pltpu.roll(v, 16, 1))
    return v * cos + partner * sin


def _inproj_kernel(ctx_ref, x_ref, sh_ref, sc_ref, n1_ref, win_ref, qn_ref, wuqn_ref, wuqp_ref,
                   kvn_ref, wukv_ref, cos_ref, sin_ref,
                   q_ref, k_ref, v_ref, xm_ref, z_ref, gmla_ref, gml_ref):
    i = pl.program_id(1)
    xin = jnp.where(i == 0, ctx_ref[0], x_ref[0])
    h = _rms(xin, n1_ref[...]) * (1.0 + sc_ref[0]) + sh_ref[0]
    big = jnp.dot(h.astype(BF16), win_ref[...], preferred_element_type=F32)
    xm_ref[0] = big[:, _C_XM:_C_Z]
    z_ref[0] = big[:, _C_Z:_C_GMLA]
    gmla_ref[0] = big[:, _C_GMLA:_C_GML]
    gml_ref[0] = big[:, _C_GML:IN_PAD]

    cos = cos_ref[...]
    sin = sin_ref[...]
    qn = _rms(big[:, _C_QLAT:_C_KVLAT], qn_ref[...]).astype(BF16)
    q_nope = jnp.dot(qn, wuqn_ref[...], preferred_element_type=F32)
    q_pe = jnp.dot(qn, wuqp_ref[...], preferred_element_type=F32)
    kvn = _rms(big[:, _C_KVLAT:_C_KROPE], kvn_ref[...]).astype(BF16)
    kv = jnp.dot(kvn, wukv_ref[...], preferred_element_type=F32)
    k_pe = _rope(big[:, _C_KROPE:_C_XM], cos, sin).astype(BF16)
    ones_col = jnp.where(lax.broadcasted_iota(I32, (big.shape[0], HEAD_W - V_DIM), 1) == 0, 1.0, 0.0).astype(BF16)
    for hh in range(MLA_HEADS):
        lo = hh * HEAD_W
        nope = slice(hh * QK_NOPE, (hh + 1) * QK_NOPE)
        q_ref[0, :, lo:lo + QK_NOPE] = (q_nope[:, nope] * Q_SCALE).astype(BF16)
        q_ref[0, :, lo + QK_NOPE:lo + HEAD_W] = (
            _rope(q_pe[:, hh * LANES:(hh + 1) * LANES], cos, sin) * Q_SCALE).astype(BF16)
        k_ref[0, :, lo:lo + QK_NOPE] = kv[:, nope].astype(BF16)
        k_ref[0, :, lo + QK_NOPE:lo + HEAD_W] = k_pe
        v_ref[0, :, lo:lo + V_DIM] = kv[:, MLA_HEADS * QK_NOPE + hh * V_DIM:MLA_HEADS * QK_NOPE + (hh + 1) * V_DIM].astype(BF16)
        v_ref[0, :, lo + V_DIM:lo + HEAD_W] = ones_col


def _input_projection(ctx, x, mod3, norm1, w_in_p, q_norm, wuq_nope, wuq_pe, kv_norm, wukv_p, cos_t, sin_t):
    B, T, D = x.shape
    n_lat = T // ROW_TILE
    nt = n_lat + 1
    tt = T + ROW_TILE
    const2 = lambda b, i: (0, 0)
    lat = lambda b, i: (b, jnp.maximum(i - 1, 0), 0)
    allrows = lambda b, i: (b, i, 0)
    modrow = lambda col: (lambda b, i: (jnp.where(i == 0, B, b), 0, col))
    return pl.pallas_call(
        _inproj_kernel,
        grid=(B, nt),
        in_specs=[pl.BlockSpec((1, ROW_TILE, D), lambda b, i: (b, 0, 0)),
                  pl.BlockSpec((1, ROW_TILE, D), lat),
                  pl.BlockSpec((1, 1, D), modrow(0)),
                  pl.BlockSpec((1, 1, D), modrow(1)),
                  pl.BlockSpec((1, D), const2),
                  pl.BlockSpec(w_in_p.shape, const2),
                  pl.BlockSpec((1, Q_LORA), const2),
                  pl.BlockSpec(wuq_nope.shape, const2),
                  pl.BlockSpec(wuq_pe.shape, const2),
                  pl.BlockSpec((1, KV_LORA), const2),
                  pl.BlockSpec(wukv_p.shape, const2),
                  pl.BlockSpec((ROW_TILE, LANES), lambda b, i: (i, 0)),
                  pl.BlockSpec((ROW_TILE, LANES), lambda b, i: (i, 0))],
        out_specs=[pl.BlockSpec((1, ROW_TILE, MLA_HEADS * HEAD_W), lat),
                   pl.BlockSpec((1, ROW_TILE, MLA_HEADS * HEAD_W), allrows),
                   pl.BlockSpec((1, ROW_TILE, MLA_HEADS * HEAD_W), allrows),
                   pl.BlockSpec((1, ROW_TILE, ML_INNER), allrows),
                   pl.BlockSpec((1, ROW_TILE, ML_INNER), lat),
                   pl.BlockSpec((1, ROW_TILE, D), lat),
                   pl.BlockSpec((1, ROW_TILE, D), lat)],
        out_shape=[jax.ShapeDtypeStruct((B, T, MLA_HEADS * HEAD_W), BF16),
                   jax.ShapeDtypeStruct((B, tt, MLA_HEADS * HEAD_W), BF16),
                   jax.ShapeDtypeStruct((B, tt, MLA_HEADS * HEAD_W), BF16),
                   jax.ShapeDtypeStruct((B, tt, ML_INNER), F32),
                   jax.ShapeDtypeStruct((B, T, ML_INNER), F32),
                   jax.ShapeDtypeStruct((B, T, D), F32),
                   jax.ShapeDtypeStruct((B, T, D), F32)],
        compiler_params=_params(("arbitrary", "arbitrary")),
        name="input_projection",
    )(ctx, x, mod3, mod3, norm1, w_in_p, q_norm, wuq_nope, wuq_pe, kv_norm, wukv_p, cos_t, sin_t)


def _feat_kernel(prev_ref, cur_ref, next_ref, cw_ref, cb_ref, wq_ref, wk_ref, wv_ref, wg_ref, bg_ref,
                 q_ref, k_ref, v_ref, xc_ref, g_ref, *, n_tiles):
    i = pl.program_id(1)
    cur = cur_ref[0]
    prev = jnp.where(i <= 1, 0.0, prev_ref[0])
    nxt = jnp.where((i == 0) | (i == n_tiles - 1), 0.0, next_ref[0])
    xx = jnp.concatenate([prev, cur, nxt], axis=0)
    rows = cur.shape[0]
    acc = jnp.broadcast_to(cb_ref[...], cur.shape)
    for w in range(CONV_W):
        lo = SUBLANES - CONV_W // 2 + w
        acc = acc + xx[lo:lo + rows] * cw_ref[w:w + 1, :]
    xc = acc * _sigmoid(acc)
    xc_ref[0] = xc
    xcb = xc.astype(BF16)
    q = jnp.dot(xcb, wq_ref[...], preferred_element_type=F32)
    k = jnp.dot(xcb, wk_ref[...], preferred_element_type=F32)
    v = jnp.dot(cur.astype(BF16), wv_ref[...], preferred_element_type=F32)
    qb, kb, vb = q.astype(BF16), k.astype(BF16), v.astype(BF16)
    q_ref[0] = qb
    k_ref[0] = (k * (ML_HEAD_DIM ** -0.5)).astype(BF16)
    v_ref[0] = vb
    g = (lax.dot_general(wg_ref[0], qb, _NT, preferred_element_type=F32)
         + lax.dot_general(wg_ref[1], kb, _NT, preferred_element_type=F32)
         + lax.dot_general(wg_ref[2], vb, _NT, preferred_element_type=F32)) + bg_ref[...]
    row = lax.broadcasted_iota(I32, g.shape, 0)
    is_forget = (row % (2 * ML_HEADS)) >= ML_HEADS
    g_ref[0] = jnp.where(is_forget, jax.nn.log_sigmoid(g), g)


def _mlstm_features(xm, conv_w8, conv_b, wq_bd, wk_bd, wv_bd, wg3, bg_col):
    B, tt, C = xm.shape
    nt = tt // ROW_TILE
    per = ROW_TILE // SUBLANES
    last8 = tt // SUBLANES - 1
    const2 = lambda b, i: (0, 0)
    rows = lambda b, i: (b, i, 0)
    ng = 4 * ML_HEADS
    return pl.pallas_call(
        functools.partial(_feat_kernel, n_tiles=nt),
        grid=(B, nt),
        in_specs=[pl.BlockSpec((1, SUBLANES, C), lambda b, i: (b, jnp.maximum(i * per - 1, 0), 0)),
                  pl.BlockSpec((1, ROW_TILE, C), rows),
                  pl.BlockSpec((1, SUBLANES, C), lambda b, i: (b, jnp.minimum((i + 1) * per, last8), 0)),
                  pl.BlockSpec((SUBLANES, C), const2),
                  pl.BlockSpec((1, C), const2),
                  pl.BlockSpec((C, C), const2),
                  pl.BlockSpec((C, C), const2),
                  pl.BlockSpec((C, C), const2),
                  pl.BlockSpec((3, ng, C), lambda b, i: (0, 0, 0)),
                  pl.BlockSpec((ng, 1), const2)],
        out_specs=[pl.BlockSpec((1, ROW_TILE, C), rows),
                   pl.BlockSpec((1, ROW_TILE, C), rows),
                   pl.BlockSpec((1, ROW_TILE, C), rows),
                   pl.BlockSpec((1, ROW_TILE, C), rows),
                   pl.BlockSpec((1, ng, ROW_TILE), lambda b, i: (b, 0, i))],
        out_shape=[jax.ShapeDtypeStruct((B, tt, C), BF16),
                   jax.ShapeDtypeStruct((B, tt, C), BF16),
                   jax.ShapeDtypeStruct((B, tt, C), BF16),
                   jax.ShapeDtypeStruct((B, tt, C), F32),
                   jax.ShapeDtypeStruct((B, ng, tt), F32)],
        compiler_params=_params(("arbitrary", "arbitrary")),
        name="mlstm_features",
    )(xm, xm, xm, conv_w8, conv_b, wq_bd, wk_bd, wv_bd, wg3, bg_col)


def _attn_kernel(q_ref, k_ref, v_ref, o_ref, sa_ref, sb_ref, pa_ref, pb_ref, mb_ref, *, n_chunks):
    j = pl.program_id(2)
    tq = ATTN_TQ

    @pl.when(j == 0)
    def _():
        sb_ref[...] = jnp.zeros_like(sb_ref)
        pa_ref[...] = jnp.ones_like(pa_ref)
        mb_ref[...] = jnp.zeros_like(mb_ref)

    def half(q, s_new, s_old, m_old, p_new, p_old):
        def body(c, carry):
            m, acc = carry
            off = pl.multiple_of(c * ATTN_TK, ATTN_TK)
            s = lax.dot_general(q, k_ref[0, pl.ds(off, ATTN_TK), :], _NT, preferred_element_type=F32)
            s_new[c] = s
            m = jnp.maximum(m, jnp.max(s, axis=-1, keepdims=True))
            p_new[c] = jnp.exp2(s_old[c] - m_old).astype(BF16)
            acc = acc + jnp.dot(p_old[c], v_ref[0, pl.ds(off, ATTN_TK), :], preferred_element_type=F32)
            return m, acc

        init = (jnp.full((tq, 1), -jnp.inf, F32), jnp.zeros((tq, HEAD_W), F32))
        m, acc = lax.fori_loop(0, n_chunks, body, init, unroll=True)
        return m, acc[:, :V_DIM] / acc[:, V_DIM:V_DIM + 1]

    m_a, out = half(q_ref[0, :tq], sa_ref, sb_ref, mb_ref[...], pb_ref, pa_ref)
    o_ref[0, :tq] = out
    m_b, out = half(q_ref[0, tq:], sb_ref, sa_ref, m_a, pa_ref, pb_ref)
    o_ref[0, tq:] = out
    mb_ref[...] = m_b


def _attention(q, k, v):
    B, T, _ = q.shape
    tt = k.shape[1]
    n_chunks = tt // ATTN_TK
    n_pairs = T // (2 * ATTN_TQ)
    assert n_chunks * ATTN_TK == tt and n_pairs * 2 * ATTN_TQ == T
    return pl.pallas_call(
        functools.partial(_attn_kernel, n_chunks=n_chunks),
        grid=(B, MLA_HEADS, n_pairs + 1),
        in_specs=[pl.BlockSpec((1, 2 * ATTN_TQ, HEAD_W), lambda b, h, j: (b, jnp.minimum(j, n_pairs - 1), h)),
                  pl.BlockSpec((1, tt, HEAD_W), lambda b, h, j: (b, 0, h)),
                  pl.BlockSpec((1, tt, HEAD_W), lambda b, h, j: (b, 0, h))],
        out_specs=pl.BlockSpec((1, 2 * ATTN_TQ, V_DIM), lambda b, h, j: (b, jnp.maximum(j - 1, 0), h)),
        out_shape=jax.ShapeDtypeStruct((B, T, MLA_HEADS * V_DIM), F32),
        scratch_shapes=[pltpu.VMEM((n_chunks, ATTN_TQ, ATTN_TK), F32),
                        pltpu.VMEM((n_chunks, ATTN_TQ, ATTN_TK), F32),
                        pltpu.VMEM((n_chunks, ATTN_TQ, ATTN_TK), BF16),
                        pltpu.VMEM((n_chunks, ATTN_TQ, ATTN_TK), BF16),
                        pltpu.VMEM((ATTN_TQ, 1), F32)],
        compiler_params=_params(("arbitrary", "arbitrary", "arbitrary")),
        name="attention",
    )(q, k, v)


def _lane_cumsum(x, reverse):
    lane = lax.broadcasted_iota(I32, x.shape, 1)
    n = x.shape[1]
    s = 1
    while s < n:
        if reverse:
            x = x + jnp.where(lane < n - s, pltpu.roll(x, n - s, 1), 0.0)
        else:
            x = x + jnp.where(lane >= s, pltpu.roll(x, s, 1), 0.0)
        s *= 2
    return x


def _scan_kernel(gf_ref, gb_ref, qf_ref, kf_ref, vf_ref, qb_ref, kb_ref, vb_ref,
                 hf_ref, hb_ref, c_ref, n_ref, m_ref):
    j = pl.program_id(1)
    L = CHUNK
    dh = ML_HEAD_DIM

    @pl.when(j == 0)
    def _():
        c_ref[...] = jnp.zeros_like(c_ref)
        n_ref[...] = jnp.zeros_like(n_ref)
        m_ref[...] = jnp.zeros_like(m_ref)

    tpos = lax.broadcasted_iota(I32, (L, L), 0)
    spos = lax.broadcasted_iota(I32, (L, L), 1)
    dirs = ((gf_ref, qf_ref, kf_ref, vf_ref, hf_ref), (gb_ref, qb_ref, kb_ref, vb_ref, hb_ref))
    for d, (g_ref, q_ref, k_ref, v_ref, h_ref) in enumerate(dirs):
        reverse = d == 1
        mask = (spos >= tpos) if reverse else (spos <= tpos)
        g = g_ref[0]
        ig4 = g[d * 2 * ML_HEADS:d * 2 * ML_HEADS + ML_HEADS]
        lf4 = g[d * 2 * ML_HEADS + ML_HEADS:(d + 1) * 2 * ML_HEADS]
        b4 = _lane_cumsum(lf4, reverse)
        rows = jnp.concatenate([b4, ig4, jnp.zeros((L - 2 * ML_HEADS, L), F32)], axis=0)
        cols = rows.T
        for hh in range(ML_HEADS):
            ci = d * ML_HEADS + hh
            b_row, ig_row = b4[hh:hh + 1], ig4[hh:hh + 1]
            b_col, ig_col = cols[:, hh:hh + 1], cols[:, ML_HEADS + hh:ML_HEADS + hh + 1]
            b_last = b_row[:, 0:1] if reverse else b_row[:, L - 1:L]
            m = m_ref[ci, 0:1, 0:1]
            sl = slice(hh * dh, (hh + 1) * dh)
            q, k, v = q_ref[0, :, sl], k_ref[0, :, sl], v_ref[0, :, sl]
            cm = c_ref[ci]
            nrow = n_ref[ci, 0:1, :]

            dmat = jnp.where(mask, b_col - b_row + ig_row, -jnp.inf)
            inter = b_col + m
            m_t = jnp.maximum(inter, jnp.max(dmat, axis=-1, keepdims=True))
            w_inter = jnp.exp(inter - m_t)
            s = lax.dot_general(q, k, _NT, preferred_element_type=F32) * jnp.exp(dmat - m_t)
            qc = lax.dot_general(q, cm.astype(BF16), _NT, preferred_element_type=F32)
            num = jnp.dot(s.astype(BF16), v, preferred_element_type=F32) + w_inter * qc
            qn = jnp.sum(q.astype(F32) * nrow, axis=-1, keepdims=True)
            den = jnp.sum(s, axis=-1, keepdims=True) + w_inter * qn
            h_ref[0, :, sl] = num / jnp.maximum(jnp.abs(den), jnp.exp(-m_t))

            dec_row = b_last - b_row + ig_row
            dec_col = b_last - b_col + ig_col
            m_new = jnp.maximum(b_last + m, jnp.max(dec_row, axis=-1, keepdims=True))
            wk = jnp.exp(dec_col - m_new)
            keep = jnp.exp(b_last + m - m_new)
            vw = (v.astype(F32) * wk).astype(BF16)
            c_ref[ci] = keep * cm + lax.dot_general(vw, k, _TN, preferred_element_type=F32)
            n_ref[ci, 0:1, :] = keep * nrow + jnp.sum(wk * k.astype(F32), axis=0, keepdims=True)
            m_ref[ci] = jnp.broadcast_to(m_new, m_ref.shape[1:])


def _mlstm_scan(gates, q, k, v):
    B, tt, C = q.shape
    nch = tt // CHUNK
    fwd = lambda b, j: (b, j, 0)
    ng = gates.shape[1]
    nc_ctx = ROW_TILE // CHUNK
    bidx = lambda j: jnp.where(j < nc_ctx, nc_ctx - 1 - j, nch - 1 + nc_ctx - j)
    bwd = lambda b, j: (b, bidx(j), 0)
    blk = pl.BlockSpec((1, CHUNK, C), fwd)
    blk_b = pl.BlockSpec((1, CHUNK, C), bwd)
    nchain = 2 * ML_HEADS
    return pl.pallas_call(
        _scan_kernel,
        grid=(B, nch),
        in_specs=[pl.BlockSpec((1, ng, CHUNK), lambda b, j: (b, 0, j)),
                  pl.BlockSpec((1, ng, CHUNK), lambda b, j: (b, 0, bidx(j))),
                  blk, blk, blk, blk_b, blk_b, blk_b],
        out_specs=[blk, blk_b],
        out_shape=[jax.ShapeDtypeStruct((B, tt, C), F32), jax.ShapeDtypeStruct((B, tt, C), F32)],
        scratch_shapes=[pltpu.VMEM((nchain, ML_HEAD_DIM, ML_HEAD_DIM), F32),
                        pltpu.VMEM((nchain, SUBLANES, ML_HEAD_DIM), F32),
                        pltpu.VMEM((nchain, SUBLANES, LANES), F32)],
        compiler_params=_params(("arbitrary", "arbitrary")),
        name="mlstm_scan",
    )(gates, gates, q, k, v, q, k, v)


def _merge_kernel(hf_ref, hb_ref, z_ref, xc_ref, gmla_ref, gml_ref, ymla_ref, x_ref,
                  g1_ref, sh2_ref, sc2_ref, mln_ref, mls_ref, wout_ref, n2_ref, wr_ref,
                  x1_ref, h2_ref, aff_ref):
    h = hf_ref[0] + hb_ref[0]
    parts = []
    for hh in range(ML_HEADS):
        seg = h[:, hh * ML_HEAD_DIM:(hh + 1) * ML_HEAD_DIM]
        parts.append(seg * lax.rsqrt(jnp.mean(seg * seg, axis=-1, keepdims=True) + EPS))
    hn = jnp.concatenate(parts, axis=-1) * mln_ref[...]
    y_ml = _sigmoid(z_ref[0]) * (hn + mls_ref[...] * xc_ref[0])
    merged = _sigmoid(gmla_ref[0]) * ymla_ref[0] + _sigmoid(gml_ref[0]) * y_ml
    out = jnp.dot(merged.astype(BF16), wout_ref[...], preferred_element_type=F32)
    x1 = x_ref[0] + g1_ref[0] * out
    x1_ref[0] = x1
    h2 = _rms(x1, n2_ref[...]) * (1.0 + sc2_ref[0]) + sh2_ref[0]
    h2_ref[0] = h2.astype(BF16)
    logits = lax.dot_general(wr_ref[...], h2, _NT, preferred_element_type=F32,
                             precision=lax.Precision.HIGHEST)
    e = jnp.exp(logits - jnp.max(logits, axis=0, keepdims=True))
    aff_ref[0] = e / jnp.sum(e, axis=0, keepdims=True)


def _merge(hf, hb, z, xc, gmla, gml, ymla, x, mod3, ml_norm, ml_skip, w_out, norm2, w_router_t):
    B, T, D = x.shape
    nt = T // ROW_TILE
    const2 = lambda b, i: (0, 0)
    lat = lambda b, i: (b, i, 0)
    shifted = lambda b, i: (b, i + 1, 0)
    modcol = lambda col: (lambda b, i: (b, 0, col))
    tile = lambda idx: pl.BlockSpec((1, ROW_TILE, D), idx)
    return pl.pallas_call(
        _merge_kernel,
        grid=(B, nt),
        in_specs=[tile(shifted), tile(shifted), tile(lat), tile(shifted), tile(lat), tile(lat), tile(lat),
                  tile(lat),
                  pl.BlockSpec((1, 1, D), modcol(2)), pl.BlockSpec((1, 1, D), modcol(3)),
                  pl.BlockSpec((1, 1, D), modcol(4)),
                  pl.BlockSpec((1, D), const2), pl.BlockSpec((1, D), const2),
                  pl.BlockSpec((D, D), const2), pl.BlockSpec((1, D), const2),
                  pl.BlockSpec((N_EXPERTS, D), const2)],
        out_specs=[tile(lat), tile(lat), pl.BlockSpec((1, N_EXPERTS, ROW_TILE), lambda b, i: (b, 0, i))],
        out_shape=[jax.ShapeDtypeStruct((B, T, D), F32),
                   jax.ShapeDtypeStruct((B, T, D), BF16),
                   jax.ShapeDtypeStruct((B, N_EXPERTS, T), F32)],
        compiler_params=_params(("arbitrary", "arbitrary")),
        name="merge_router",
    )(hf, hb, z, xc, gmla, gml, ymla, x, mod3, mod3, mod3, ml_norm, ml_skip, w_out, norm2, w_router_t)


def _chunked_cumsum(mask_f, tri):
    n_e, t = mask_f.shape
    off = jnp.zeros((n_e, 1), F32)
    outs = []
    for c in range(t // LANES):
        x = mask_f[:, c * LANES:(c + 1) * LANES]
        inc = jnp.dot(x.astype(BF16), tri, preferred_element_type=F32)
        outs.append(inc - x + off)
        off = off + inc[:, LANES - 1:LANES]
    return jnp.concatenate(outs, axis=1)


def _select_kernel(aff_ref, pos_ref, cs_ref, *, cap):
    aff = aff_ref[0]
    n_e = aff.shape[0]

    def count_ge(t):
        return jnp.sum(jnp.where(aff >= t, 1.0, 0.0), axis=1, keepdims=True)

    def body(carry):
        lo, hi, _ = carry
        mid = 0.5 * (lo + hi)
        ok = count_ge(mid) >= cap
        lo, hi = jnp.where(ok, mid, lo), jnp.where(ok, hi, mid)
        mid = 0.5 * (lo + hi)
        return lo, hi, jnp.max(jnp.where((mid > lo) & (mid < hi), 1.0, 0.0))

    lo, hi, _ = lax.while_loop(lambda carry: carry[2] > 0.5, body,
                               (jnp.zeros((n_e, 1), F32), jnp.full((n_e, 1), 2.0, F32), jnp.float32(1.0)))
    gt = jnp.where(aff >= hi, 1.0, 0.0)
    eq = jnp.where(aff >= lo, 1.0, 0.0) - gt
    need = cap - jnp.sum(gt, axis=1, keepdims=True)
    tri = jnp.where(lax.broadcasted_iota(I32, (LANES, LANES), 0) <= lax.broadcasted_iota(I32, (LANES, LANES), 1),
                    1.0, 0.0).astype(BF16)
    eq_rank = _chunked_cumsum(eq, tri)
    sel = gt + eq * jnp.where(eq_rank < need, 1.0, 0.0)
    cs = _chunked_cumsum(sel, tri)
    cs_ref[0] = cs.astype(I32)
    pos_ref[0] = jnp.where(sel > 0.5, cs, -1.0).astype(I32)


def _select(aff_t, cap):
    B, n_e, T = aff_t.shape
    blk = pl.BlockSpec((1, n_e, T), lambda b: (b, 0, 0))
    return pl.pallas_call(
        functools.partial(_select_kernel, cap=cap),
        grid=(B,),
        in_specs=[blk],
        out_specs=[blk, blk],
        out_shape=[jax.ShapeDtypeStruct((B, n_e, T), I32), jax.ShapeDtypeStruct((B, n_e, T), I32)],
        compiler_params=_params(("arbitrary",)),
        name="expert_select",
    )(aff_t)


def _window_start(base_ref, flat):
    start = base_ref[flat]
    return pl.multiple_of((start // BF16_ROWS) * BF16_ROWS, BF16_ROWS)


def _expert_kernel(base_ref, x_ref, pos_ref, wg_ref, wu_ref, wd_ref, y_ref, xs_ref, *, n_tiles, cap):
    b, e = pl.program_id(0), pl.program_id(1)
    head = BF16_ROWS
    xs_ref[0:head, :] = jnp.zeros((head, xs_ref.shape[1]), BF16)
    row_id = lax.broadcasted_iota(I32, (MOE_WIN, MOE_TILE), 0)

    def gather(k, carry):
        s16 = _window_start(base_ref, (b * N_EXPERTS + e) * (n_tiles + 1) + k)
        rel = pos_ref[0, pl.ds(k, 1), :] - s16
        onehot = jnp.where(row_id == rel, 1.0, 0.0).astype(BF16)
        tok = pl.ds(pl.multiple_of(k * MOE_TILE, MOE_TILE), MOE_TILE)
        rows = jnp.dot(onehot, x_ref[0, tok, :], preferred_element_type=F32)
        first = xs_ref[pl.ds(s16, head), :].astype(F32) + rows[:head]
        xs_ref[pl.ds(s16 + head, MOE_WIN - head), :] = rows[head:].astype(BF16)
        xs_ref[pl.ds(s16, head), :] = first.astype(BF16)
        return carry

    lax.fori_loop(0, n_tiles, gather, 0, unroll=4)

    for r in range(cap // ROW_TILE):
        rows = slice(r * ROW_TILE, (r + 1) * ROW_TILE)
        xs = xs_ref[rows, :]
        a = jnp.dot(xs, wg_ref[0], preferred_element_type=F32)
        u = jnp.dot(xs, wu_ref[0], preferred_element_type=F32)
        hm = (a * _sigmoid(a) * u).astype(BF16)
        y_ref[0, 0, rows, :] = jnp.dot(hm, wd_ref[0], preferred_element_type=F32).astype(BF16)
    y_ref[0, 0, cap:, :] = jnp.zeros((y_ref.shape[2] - cap, y_ref.shape[3]), BF16)


def _experts(base, h2, pos3, wg, wu, wd, cap):
    B, T, D = h2.shape
    nk = T // MOE_TILE
    yr = cap + MOE_WIN
    ff = wg.shape[2]
    wspec = lambda shape: pl.BlockSpec((1,) + shape, lambda b, e, base: (e, 0, 0))
    grid_spec = pltpu.PrefetchScalarGridSpec(
        num_scalar_prefetch=1,
        grid=(B, N_EXPERTS),
        in_specs=[pl.BlockSpec((1, T, D), lambda b, e, base: (b, 0, 0), pipeline_mode=pl.Buffered(1)),
                  pl.BlockSpec((1, nk, MOE_TILE), lambda b, e, base: (b * N_EXPERTS + e, 0, 0)),
                  wspec((D, ff)), wspec((D, ff)), wspec((ff, D))],
        out_specs=pl.BlockSpec((1, 1, yr, D), lambda b, e, base: (b, e, 0, 0)),
        scratch_shapes=[pltpu.VMEM((yr, D), BF16)])
    return pl.pallas_call(
        functools.partial(_expert_kernel, n_tiles=nk, cap=cap),
        grid_spec=grid_spec,
        out_shape=jax.ShapeDtypeStruct((B, N_EXPERTS, yr, D), BF16),
        compiler_params=_params(("arbitrary", "arbitrary")),
        name="expert_ffn",
    )(base, h2, pos3, wg, wu, wd)


def _combine_kernel(base_ref, y_ref, pos_ref, g_ref, x1_ref, g2_ref, fn_ref, o_ref, *, n_tiles, tiles_per_blk):
    b, tb, e = pl.program_id(0), pl.program_id(1), pl.program_id(2)
    flat0 = (b * N_EXPERTS + e) * (n_tiles + 1) + tb * tiles_per_blk
    pick = lax.broadcasted_iota(I32, (MOE_TILE, N_EXPERTS), 1) == e
    col_id = lax.broadcasted_iota(I32, (MOE_TILE, MOE_TILE), 1).astype(F32)

    @pl.when(e == 0)
    def _():
        o_ref[...] = jnp.zeros_like(o_ref)

    def columns(k):
        rows = slice(k * MOE_TILE, (k + 1) * MOE_TILE)
        pcol = jnp.sum(jnp.where(pick, pos_ref[0, rows, :].astype(F32), 0.0), axis=1, keepdims=True)
        gcol = jnp.sum(jnp.where(pick, g_ref[0, rows, :], 0.0), axis=1, keepdims=True)
        return rows, pcol, gcol

    for k in range(tiles_per_blk):
        s16 = _window_start(base_ref, flat0 + k)
        rows, pcol, gcol = columns(k)
        onehot = jnp.where(col_id == pcol - s16.astype(F32), 1.0, 0.0).astype(BF16)
        o_ref[0, rows, :] += jnp.dot(onehot, y_ref[0, 0, pl.ds(s16, MOE_TILE), :],
                                     preferred_element_type=F32) * gcol

    for k in range(tiles_per_blk):
        s16 = _window_start(base_ref, flat0 + k)

        @pl.when(base_ref[flat0 + k + 1] > s16 + MOE_TILE)
        def _():
            rows, pcol, gcol = columns(k)
            tail = lax.broadcasted_iota(I32, (MOE_TILE, BF16_ROWS), 1).astype(F32) + float(MOE_TILE)
            onehot = jnp.where(tail == pcol - s16.astype(F32), 1.0, 0.0).astype(BF16)
            o_ref[0, rows, :] += jnp.dot(onehot, y_ref[0, 0, pl.ds(s16 + MOE_TILE, BF16_ROWS), :],
                                         preferred_element_type=F32) * gcol

    @pl.when(e == N_EXPERTS - 1)
    def _():
        for k in range(tiles_per_blk):
            rows = slice(k * MOE_TILE, (k + 1) * MOE_TILE)
            x2 = x1_ref[0, rows, :] + g2_ref[0] * o_ref[0, rows, :]
            o_ref[0, rows, :] = _rms(x2, fn_ref[...])


def _combine(base, y, pos_t, g_t, x1, mod3, final_norm):
    B, T, D = x1.shape
    nk = T // MOE_TILE
    per = 8
    blk = per * MOE_TILE
    yr = y.shape[2]
    tok = lambda b, tb, e, base: (b, tb, 0)
    grid_spec = pltpu.PrefetchScalarGridSpec(
        num_scalar_prefetch=1,
        grid=(B, nk // per, N_EXPERTS),
        in_specs=[pl.BlockSpec((1, 1, yr, D), lambda b, tb, e, base: (b, e, 0, 0)),
                  pl.BlockSpec((1, blk, N_EXPERTS), tok),
                  pl.BlockSpec((1, blk, N_EXPERTS), tok),
                  pl.BlockSpec((1, blk, D), tok),
                  pl.BlockSpec((1, 1, D), lambda b, tb, e, base: (b, 0, 5)),
                  pl.BlockSpec((1, D), lambda b, tb, e, base: (0, 0))],
        out_specs=pl.BlockSpec((1, blk, D), tok))
    return pl.pallas_call(
        functools.partial(_combine_kernel, n_tiles=nk, tiles_per_blk=per),
        grid_spec=grid_spec,
        out_shape=jax.ShapeDtypeStruct((B, T, D), F32),
        compiler_params=_params(("arbitrary",) * 3),
        name="moe_combine",
    )(base, y, pos_t, g_t, x1, mod3, final_norm)


def _rope_tables(T):
    rows = T // GRID_W
    row = jnp.repeat(jnp.arange(rows, dtype=F32), GRID_W)
    col = jnp.tile(jnp.arange(GRID_W, dtype=F32), rows)
    inv = ROPE_BASE ** (-jnp.arange(ROPE_PAIRS, dtype=F32) / ROPE_PAIRS)
    ar, ac = row[:, None] * inv, col[:, None] * inv
    ones = jnp.ones((T, LANES - QK_ROPE), F32)
    cos = jnp.concatenate([jnp.cos(ar), jnp.cos(ar), jnp.cos(ac), jnp.cos(ac), ones], axis=1)
    sin = jnp.concatenate([-jnp.sin(ar), jnp.sin(ar), -jnp.sin(ac), jnp.sin(ac), 0.0 * ones], axis=1)
    cos = jnp.concatenate([jnp.ones((ROW_TILE, LANES), F32), cos], axis=0)
    sin = jnp.concatenate([jnp.zeros((ROW_TILE, LANES), F32), sin], axis=0)
    return cos, sin


def _blockdiag_dense(w):
    n, bs, _ = w.shape
    rows = jnp.broadcast_to(w.transpose(1, 0, 2).reshape(1, bs, n * bs), (n, bs, n * bs)).reshape(n * bs, n * bs)
    r = jnp.arange(n * bs) // bs
    return jnp.where(r[:, None] == r[None, :], rows, 0.0).astype(BF16)


def kernel(x, c, ctx, c_ctx, w_mod, b_mod, norm1, w_in, q_norm, w_uq, kv_norm, w_ukv, conv_w, conv_b,
           w_qblk, w_kblk, w_vblk, w_gate, b_gate, ml_norm, ml_skip, w_out, norm2, w_router,
           w_e_gate, w_e_up, w_e_down, final_norm):
    B, T, D = x.shape
    assert w_mod.shape[0] == 1 and D == D_MODEL and ctx.shape[1] == ROW_TILE
    cap = CAP_FACTOR * T // N_EXPERTS

    cc = jnp.zeros((SUBLANES, D), F32).at[:B].set(c).at[B].set(c_ctx)
    mod = _modulation(cc, w_mod[0], b_mod[0].reshape(1, -1))
    mod3 = mod[:B + 1].reshape(B + 1, 1, 6 * D)

    wi = w_in[0]
    zpad = jnp.zeros((D, LANES - QK_ROPE), F32)
    w_in_p = jnp.concatenate([wi[:, :_C_KROPE + QK_ROPE], zpad, wi[:, _C_KROPE + QK_ROPE:]], axis=1).astype(BF16)
    wuq = w_uq[0].reshape(Q_LORA, MLA_HEADS, QK_NOPE + QK_ROPE)
    wuq_nope = wuq[:, :, :QK_NOPE].reshape(Q_LORA, MLA_HEADS * QK_NOPE).astype(BF16)
    wuq_pe = jnp.pad(wuq[:, :, QK_NOPE:], ((0, 0), (0, 0), (0, LANES - QK_ROPE))).reshape(
        Q_LORA, MLA_HEADS * LANES).astype(BF16)
    wukv = w_ukv[0].reshape(KV_LORA, MLA_HEADS, QK_NOPE + V_DIM)
    wukv_p = jnp.concatenate([wukv[:, :, :QK_NOPE].reshape(KV_LORA, -1),
                              wukv[:, :, QK_NOPE:].reshape(KV_LORA, -1)], axis=1).astype(BF16)
    cos_t, sin_t = _rope_tables(T)

    q, k, v, xm, z, gmla, gml = _input_projection(
        ctx, x, mod3, norm1, w_in_p, q_norm, wuq_nope, wuq_pe, kv_norm, wukv_p, cos_t, sin_t)

    conv_w8 = jnp.zeros((SUBLANES, ML_INNER), F32).at[:CONV_W].set(conv_w[0])
    wg3 = w_gate[0].reshape(3, ML_INNER, 4 * ML_HEADS).transpose(0, 2, 1).astype(BF16)
    mq, mk, mv, xc, gates = _mlstm_features(
        xm, conv_w8, conv_b, _blockdiag_dense(w_qblk[0]), _blockdiag_dense(w_kblk[0]),
        _blockdiag_dense(w_vblk[0]), wg3, b_gate[0].reshape(-1, 1))

    y_mla = _attention(q, k, v)
    hf, hb = _mlstm_scan(gates, mq, mk, mv)

    x1, h2, aff_t = _merge(hf, hb, z, xc, gmla, gml, y_mla, x, mod3, ml_norm, ml_skip,
                           w_out[0].astype(BF16), norm2, w_router[0].T)

    pos, cs = _select(aff_t, cap)
    base = jnp.concatenate([cs[:, :, ::MOE_TILE], jnp.full((B, N_EXPERTS, 1), cap, I32)], axis=2).reshape(-1)
    y = _experts(base, h2, pos.reshape(B * N_EXPERTS, T // MOE_TILE, MOE_TILE), w_e_gate[0].astype(BF16),
                 w_e_up[0].astype(BF16), w_e_down[0].astype(BF16), cap)
    return _combine(base, y, pos.transpose(0, 2, 1), aff_t.transpose(0, 2, 1), x1, mod3,
                    final_norm.reshape(1, -1))
```

```python
import functools

import jax
import jax.numpy as jnp
from jax import lax
from jax.experimental import pallas as pl
from jax.experimental.pallas import tpu as pltpu

F32 = jnp.float32
BF16 = jnp.bfloat16
I32 = jnp.int32

D_MODEL = 1024
GRID_W = 64
MLA_HEADS = 8
QK_NOPE = 128
QK_ROPE = 64
V_DIM = 128
Q_LORA = 384
KV_LORA = 256
ROPE_BASE = 10000.0
ROPE_PAIRS = QK_ROPE // 4
ATTN_SCALE = (QK_NOPE + QK_ROPE) ** -0.5
Q_SCALE = ATTN_SCALE * 1.4426950408889634
ML_HEADS = 4
ML_INNER = 1024
ML_HEAD_DIM = ML_INNER // ML_HEADS
QKV_BLOCK = 4
CONV_W = 5
CHUNK = 128
N_EXPERTS = 16
EXPERT_FF = 1024
CAP_FACTOR = 2
EPS = 1e-6

LANES = 128
SUBLANES = 8
BF16_ROWS = 16
ROW_TILE = 256
HEAD_W = 256
ATTN_TQ = 256
ATTN_TK = 768
MOE_TILE = 256
MOE_WIN = MOE_TILE + BF16_ROWS
VMEM_LIMIT = 56 * 1024 * 1024

_C_QLAT = 0
_C_KVLAT = Q_LORA
_C_KROPE = Q_LORA + KV_LORA
_C_XM = _C_KROPE + LANES
_C_Z = _C_XM + ML_INNER
_C_GMLA = _C_Z + ML_INNER
_C_GML = _C_GMLA + D_MODEL
IN_PAD = _C_GML + D_MODEL

_NT = (((1,), (1,)), ((), ()))
_TN = (((0,), (0,)), ((), ()))


def _params(sem, vmem=VMEM_LIMIT):
    return pltpu.CompilerParams(dimension_semantics=sem, vmem_limit_bytes=vmem)


def _rms(x, g):
    return x * lax.rsqrt(jnp.mean(x * x, axis=-1, keepdims=True) + EPS) * g


def _sigmoid(x):
    return jax.nn.sigmoid(x)


def _mod_kernel(c_ref, w_ref, b_ref, o_ref):
    c = c_ref[...]
    s = c * _sigmoid(c)
    o_ref[...] = jnp.dot(s, w_ref[...], preferred_element_type=F32,
                         precision=lax.Precision.HIGHEST) + b_ref[...]


def _modulation(cc, w_mod, b_mod):
    n = w_mod.shape[1]
    tn = 768
    return pl.pallas_call(
        _mod_kernel,
        grid=(n // tn,),
        in_specs=[pl.BlockSpec((SUBLANES, D_MODEL), lambda j: (0, 0)),
                  pl.BlockSpec((D_MODEL, tn), lambda j: (0, j)),
                  pl.BlockSpec((1, tn), lambda j: (0, j))],
        out_specs=pl.BlockSpec((SUBLANES, tn), lambda j: (0, j)),
        out_shape=jax.ShapeDtypeStruct((SUBLANES, n), F32),
        compiler_params=_params(("arbitrary",)),
        name="modulation",
    )(cc, w_mod, b_mod)


def _rope(v, cos, sin):
    lane = lax.broadcasted_iota(I32, v.shape, 1)
    partner = jnp.where(lane % 32 < 16, pltpu.roll(v, LANES - 16, 1), pltpu.roll(v, 16, 1))
    return v * cos + partner * sin


def _inproj_kernel(ctx_ref, x_ref, sh_ref, sc_ref, n1_ref, win_ref, qn_ref, wuqn_ref, wuqp_ref,
                   kvn_ref, wukv_ref, cos_ref, sin_ref,
                   q_ref, k_ref, v_ref, xm_ref, z_ref, gmla_ref, gml_ref):
    i = pl.program_id(1)
    xin = jnp.where(i == 0, ctx_ref[0], x_ref[0])
    h = _rms(xin, n1_ref[...]) * (1.0 + sc_ref[0]) + sh_ref[0]
    big = jnp.dot(h.astype(BF16), win_ref[...], preferred_element_type=F32)
    xm_ref[0] = big[:, _C_XM:_C_Z]
    z_ref[0] = big[:, _C_Z:_C_GMLA]
    gmla_ref[0] = big[:, _C_GMLA:_C_GML]
    gml_ref[0] = big[:, _C_GML:IN_PAD]

    cos = cos_ref[...]
    sin = sin_ref[...]
    qn = _rms(big[:, _C_QLAT:_C_KVLAT], qn_ref[...]).astype(BF16)
    q_nope = jnp.dot(qn, wuqn_ref[...], preferred_element_type=F32)
    q_pe = jnp.dot(qn, wuqp_ref[...], preferred_element_type=F32)
    kvn = _rms(big[:, _C_KVLAT:_C_KROPE], kvn_ref[...]).astype(BF16)
    kv = jnp.dot(kvn, wukv_ref[...], preferred_element_type=F32)
    k_pe = _rope(big[:, _C_KROPE:_C_XM], cos, sin).astype(BF16)
    ones_col = jnp.where(lax.broadcasted_iota(I32, (big.shape[0], HEAD_W - V_DIM), 1) == 0, 1.0, 0.0).astype(BF16)
    for hh in range(MLA_HEADS):
        lo = hh * HEAD_W
        nope = slice(hh * QK_NOPE, (hh + 1) * QK_NOPE)
        q_ref[0, :, lo:lo + QK_NOPE] = (q_nope[:, nope] * Q_SCALE).astype(BF16)
        q_ref[0, :, lo + QK_NOPE:lo + HEAD_W] = (
            _rope(q_pe[:, hh * LANES:(hh + 1) * LANES], cos, sin) * Q_SCALE).astype(BF16)
        k_ref[0, :, lo:lo + QK_NOPE] = kv[:, nope].astype(BF16)
        k_ref[0, :, lo + QK_NOPE:lo + HEAD_W] = k_pe
        v_ref[0, :, lo:lo + V_DIM] = kv[:, MLA_HEADS * QK_NOPE + hh * V_DIM:MLA_HEADS * QK_NOPE + (hh + 1) * V_DIM].astype(BF16)
        v_ref[0, :, lo + V_DIM:lo + HEAD_W] = ones_col


def _input_projection(ctx, x, mod3, norm1, w_in_p, q_norm, wuq_nope, wuq_pe, kv_norm, wukv_p, cos_t, sin_t):
    B, T, D = x.shape
    n_lat = T // ROW_TILE
    nt = n_lat + 1
    tt = T + ROW_TILE
    const2 = lambda b, i: (0, 0)
    lat = lambda b, i: (b, jnp.maximum(i - 1, 0), 0)
    allrows = lambda b, i: (b, i, 0)
    modrow = lambda col: (lambda b, i: (jnp.where(i == 0, B, b), 0, col))
    return pl.pallas_call(
        _inproj_kernel,
        grid=(B, nt),
        in_specs=[pl.BlockSpec((1, ROW_TILE, D), lambda b, i: (b, 0, 0)),
                  pl.BlockSpec((1, ROW_TILE, D), lat),
                  pl.BlockSpec((1, 1, D), modrow(0)),
                  pl.BlockSpec((1, 1, D), modrow(1)),
                  pl.BlockSpec((1, D), const2),
                  pl.BlockSpec(w_in_p.shape, const2),
                  pl.BlockSpec((1, Q_LORA), const2),
                  pl.BlockSpec(wuq_nope.shape, const2),
                  pl.BlockSpec(wuq_pe.shape, const2),
                  pl.BlockSpec((1, KV_LORA), const2),
                  pl.BlockSpec(wukv_p.shape, const2),
                  pl.BlockSpec((ROW_TILE, LANES), lambda b, i: (i, 0)),
                  pl.BlockSpec((ROW_TILE, LANES), lambda b, i: (i, 0))],
        out_specs=[pl.BlockSpec((1, ROW_TILE, MLA_HEADS * HEAD_W), lat),
                   pl.BlockSpec((1, ROW_TILE, MLA_HEADS * HEAD_W), allrows),
                   pl.BlockSpec((1, ROW_TILE, MLA_HEADS * HEAD_W), allrows),
                   pl.BlockSpec((1, ROW_TILE, ML_INNER), allrows),
                   pl.BlockSpec((1, ROW_TILE, ML_INNER), lat),
                   pl.BlockSpec((1, ROW_TILE, D), lat),
                   pl.BlockSpec((1, ROW_TILE, D), lat)],
        out_shape=[jax.ShapeDtypeStruct((B, T, MLA_HEADS * HEAD_W), BF16),
                   jax.ShapeDtypeStruct((B, tt, MLA_HEADS * HEAD_W), BF16),
                   jax.ShapeDtypeStruct((B, tt, MLA_HEADS * HEAD_W), BF16),
                   jax.ShapeDtypeStruct((B, tt, ML_INNER), F32),
                   jax.ShapeDtypeStruct((B, T, ML_INNER), F32),
                   jax.ShapeDtypeStruct((B, T, D), F32),
                   jax.ShapeDtypeStruct((B, T, D), F32)],
        compiler_params=_params(("arbitrary", "arbitrary")),
        name="input_projection",
    )(ctx, x, mod3, mod3, norm1, w_in_p, q_norm, wuq_nope, wuq_pe, kv_norm, wukv_p, cos_t, sin_t)


def _feat_kernel(prev_ref, cur_ref, next_ref, cw_ref, cb_ref, wq_ref, wk_ref, wv_ref, wg_ref, bg_ref,
                 q_ref, k_ref, v_ref, xc_ref, g_ref, *, n_tiles):
    i = pl.program_id(1)
    cur = cur_ref[0]
    prev = jnp.where(i <= 1, 0.0, prev_ref[0])
    nxt = jnp.where((i == 0) | (i == n_tiles - 1), 0.0, next_ref[0])
    xx = jnp.concatenate([prev, cur, nxt], axis=0)
    rows = cur.shape[0]
    acc = jnp.broadcast_to(cb_ref[...], cur.shape)
    for w in range(CONV_W):
        lo = SUBLANES - CONV_W // 2 + w
        acc = acc + xx[lo:lo + rows] * cw_ref[w:w + 1, :]
    xc = acc * _sigmoid(acc)
    xc_ref[0] = xc
    xcb = xc.astype(BF16)
    q_t = lax.dot_general(wq_ref[...], xcb, _NT, preferred_element_type=F32)
    k = jnp.dot(xcb, wk_ref[...], preferred_element_type=F32)
    v_t = lax.dot_general(wv_ref[...], cur.astype(BF16), _NT, preferred_element_type=F32)
    qb, kb, vb = q_t.astype(BF16), k.astype(BF16), v_t.astype(BF16)
    q_ref[0] = qb
    k_ref[0] = (k * (ML_HEAD_DIM ** -0.5)).astype(BF16)
    v_ref[0] = vb
    g = (jnp.dot(wg_ref[0], qb, preferred_element_type=F32)
         + lax.dot_general(wg_ref[1], kb, _NT, preferred_element_type=F32)
         + jnp.dot(wg_ref[2], vb, preferred_element_type=F32)) + bg_ref[...]
    row = lax.broadcasted_iota(I32, g.shape, 0)
    is_forget = (row % (2 * ML_HEADS)) >= ML_HEADS
    g_ref[0] = jnp.where(is_forget, jax.nn.log_sigmoid(g), g)


def _mlstm_features(xm, conv_w8, conv_b, wq_bd_t, wk_bd, wv_bd_t, wg3, bg_col):
    B, tt, C = xm.shape
    nt = tt // ROW_TILE
    per = ROW_TILE // SUBLANES
    last8 = tt // SUBLANES - 1
    const2 = lambda b, i: (0, 0)
    rows = lambda b, i: (b, i, 0)
    cols = lambda b, i: (b, 0, i)
    ng = 4 * ML_HEADS
    return pl.pallas_call(
        functools.partial(_feat_kernel, n_tiles=nt),
        grid=(B, nt),
        in_specs=[pl.BlockSpec((1, SUBLANES, C), lambda b, i: (b, jnp.maximum(i * per - 1, 0), 0)),
                  pl.BlockSpec((1, ROW_TILE, C), rows),
                  pl.BlockSpec((1, SUBLANES, C), lambda b, i: (b, jnp.minimum((i + 1) * per, last8), 0)),
                  pl.BlockSpec((SUBLANES, C), const2),
                  pl.BlockSpec((1, C), const2),
                  pl.BlockSpec((C, C), const2),
                  pl.BlockSpec((C, C), const2),
                  pl.BlockSpec((C, C), const2),
                  pl.BlockSpec((3, ng, C), lambda b, i: (0, 0, 0)),
                  pl.BlockSpec((ng, 1), const2)],
        out_specs=[pl.BlockSpec((1, C, ROW_TILE), cols),
                   pl.BlockSpec((1, ROW_TILE, C), rows),
                   pl.BlockSpec((1, C, ROW_TILE), cols),
                   pl.BlockSpec((1, ROW_TILE, C), rows),
                   pl.BlockSpec((1, ng, ROW_TILE), cols)],
        out_shape=[jax.ShapeDtypeStruct((B, C, tt), BF16),
                   jax.ShapeDtypeStruct((B, tt, C), BF16),
                   jax.ShapeDtypeStruct((B, C, tt), BF16),
                   jax.ShapeDtypeStruct((B, tt, C), F32),
                   jax.ShapeDtypeStruct((B, ng, tt), F32)],
        compiler_params=_params(("arbitrary", "arbitrary")),
        name="mlstm_features",
    )(xm, xm, xm, conv_w8, conv_b, wq_bd_t, wk_bd, wv_bd_t, wg3, bg_col)


def _attn_kernel(q_ref, k_ref, v_ref, o_ref, s_ref, p_ref, m_ref, *, n_chunks):
    j = pl.program_id(2)

    @pl.when(j == 0)
    def _():
        s_ref[...] = jnp.zeros_like(s_ref)
        p_ref[...] = jnp.ones_like(p_ref)
        m_ref[...] = jnp.zeros_like(m_ref)

    q = q_ref[0]
    m_old = m_ref[...]
    m = jnp.full((ATTN_TQ, 1), -jnp.inf, F32)
    acc = jnp.zeros((ATTN_TQ, HEAD_W), F32)
    for c in range(n_chunks):
        keys = slice(c * ATTN_TK, (c + 1) * ATTN_TK)
        acc = acc + jnp.dot(p_ref[c], v_ref[0, keys, :], preferred_element_type=F32)
        p_ref[c] = jnp.exp2(s_ref[c] - m_old).astype(BF16)
        s = lax.dot_general(q, k_ref[0, keys, :], _NT, preferred_element_type=F32)
        s_ref[c] = s
        m = jnp.maximum(m, jnp.max(s, axis=-1, keepdims=True))
    m_ref[...] = m
    o_ref[0] = acc[:, :V_DIM] / acc[:, V_DIM:V_DIM + 1]


def _attention(q, k, v):
    B, T, _ = q.shape
    tt = k.shape[1]
    n_chunks = tt // ATTN_TK
    n_tiles = T // ATTN_TQ
    lag = 2
    assert n_chunks * ATTN_TK == tt and n_tiles * ATTN_TQ == T
    return pl.pallas_call(
        functools.partial(_attn_kernel, n_chunks=n_chunks),
        grid=(B, MLA_HEADS, n_tiles + lag),
        in_specs=[pl.BlockSpec((1, ATTN_TQ, HEAD_W), lambda b, h, j: (b, jnp.minimum(j, n_tiles - 1), h)),
                  pl.BlockSpec((1, tt, HEAD_W), lambda b, h, j: (b, 0, h)),
                  pl.BlockSpec((1, tt, HEAD_W), lambda b, h, j: (b, 0, h))],
        out_specs=pl.BlockSpec((1, ATTN_TQ, V_DIM), lambda b, h, j: (b, jnp.maximum(j - lag, 0), h)),
        out_shape=jax.ShapeDtypeStruct((B, T, MLA_HEADS * V_DIM), F32),
        scratch_shapes=[pltpu.VMEM((n_chunks, ATTN_TQ, ATTN_TK), F32),
                        pltpu.VMEM((n_chunks, ATTN_TQ, ATTN_TK), BF16),
                        pltpu.VMEM((ATTN_TQ, 1), F32)],
        compiler_params=_params(("arbitrary", "arbitrary", "arbitrary")),
        name="attention",
    )(q, k, v)


def _lane_cumsum(x, reverse):
    lane = lax.broadcasted_iota(I32, x.shape, 1)
    n = x.shape[1]
    s = 1
    while s < n:
        if reverse:
            x = x + jnp.where(lane < n - s, pltpu.roll(x, n - s, 1), 0.0)
        else:
            x = x + jnp.where(lane >= s, pltpu.roll(x, s, 1), 0.0)
        s *= 2
    return x


def _scan_kernel(gf_ref, gb_ref, qf_ref, kf_ref, vf_ref, qb_ref, kb_ref, vb_ref,
                 hf_ref, hb_ref, cn_ref, m_ref):
    j = pl.program_id(1)
    L = CHUNK
    dh = ML_HEAD_DIM

    @pl.when(j == 0)
    def _():
        cn_ref[...] = jnp.zeros_like(cn_ref)
        m_ref[...] = jnp.zeros_like(m_ref)

    spos = lax.broadcasted_iota(I32, (L, L), 0)
    tpos = lax.broadcasted_iota(I32, (L, L), 1)
    first_row = lax.broadcasted_iota(I32, (BF16_ROWS, L), 0) == 0
    dirs = ((gf_ref, qf_ref, kf_ref, vf_ref, hf_ref), (gb_ref, qb_ref, kb_ref, vb_ref, hb_ref))
    for d, (g_ref, q_ref, k_ref, v_ref, h_ref) in enumerate(dirs):
        reverse = d == 1
        mask = (spos >= tpos) if reverse else (spos <= tpos)
        g = g_ref[0]
        ig4 = g[d * 2 * ML_HEADS:d * 2 * ML_HEADS + ML_HEADS]
        lf4 = g[d * 2 * ML_HEADS + ML_HEADS:(d + 1) * 2 * ML_HEADS]
        b4 = _lane_cumsum(lf4, reverse)
        a4 = ig4 - b4
        a_cols = jnp.concatenate([a4, jnp.zeros((L - ML_HEADS, L), F32)], axis=0).T
        for hh in range(ML_HEADS):
            ci = d * ML_HEADS + hh
            b_row, a_row, a_col = b4[hh:hh + 1], a4[hh:hh + 1], a_cols[:, hh:hh + 1]
            b_last = b_row[:, 0:1] if reverse else b_row[:, L - 1:L]
            m = m_ref[ci, 0:1, 0:1]
            sl = slice(hh * dh, (hh + 1) * dh)
            q_t, k, v_t = q_ref[0, sl, :], k_ref[0, :, sl], v_ref[0, sl, :]
            cn = cn_ref[ci]

            dmat = jnp.where(mask, b_row + a_col, -jnp.inf)
            inter = b_row + m
            m_t = jnp.maximum(inter, jnp.max(dmat, axis=0, keepdims=True))
            w_inter = jnp.exp(inter - m_t)
            s = jnp.dot(k, q_t, preferred_element_type=F32) * jnp.exp(dmat - m_t)
            cq = jnp.dot(cn.astype(BF16), q_t, preferred_element_type=F32)
            num = jnp.dot(v_t, s.astype(BF16), preferred_element_type=F32) + w_inter * cq[:dh]
            den = jnp.sum(s, axis=0, keepdims=True) + w_inter * cq[dh:dh + 1]
            h_ref[0, sl, :] = num / jnp.maximum(jnp.abs(den), jnp.exp(-m_t))

            dec = b_last + a_row
            m_new = jnp.maximum(b_last + m, jnp.max(dec, axis=-1, keepdims=True))
            wk = jnp.exp(dec - m_new)
            keep = jnp.exp(b_last + m - m_new)
            vw = jnp.concatenate([(v_t.astype(F32) * wk).astype(BF16),
                                  jnp.where(first_row, wk, 0.0).astype(BF16)], axis=0)
            cn_ref[ci] = keep * cn + jnp.dot(vw, k, preferred_element_type=F32)
            m_ref[ci] = jnp.broadcast_to(m_new, m_ref.shape[1:])


def _mlstm_scan(gates, q_t, k, v_t):
    B, tt, C = k.shape
    nch = tt // CHUNK
    ng = gates.shape[1]
    nc_ctx = ROW_TILE // CHUNK
    bidx = lambda j: jnp.where(j < nc_ctx, nc_ctx - 1 - j, nch - 1 + nc_ctx - j)
    rows_f = pl.BlockSpec((1, CHUNK, C), lambda b, j: (b, j, 0))
    rows_b = pl.BlockSpec((1, CHUNK, C), lambda b, j: (b, bidx(j), 0))
    cols_f = pl.BlockSpec((1, C, CHUNK), lambda b, j: (b, 0, j))
    cols_b = pl.BlockSpec((1, C, CHUNK), lambda b, j: (b, 0, bidx(j)))
    nchain = 2 * ML_HEADS
    return pl.pallas_call(
        _scan_kernel,
        grid=(B, nch),
        in_specs=[pl.BlockSpec((1, ng, CHUNK), lambda b, j: (b, 0, j)),
                  pl.BlockSpec((1, ng, CHUNK), lambda b, j: (b, 0, bidx(j))),
                  cols_f, rows_f, cols_f, cols_b, rows_b, cols_b],
        out_specs=[cols_f, cols_b],
        out_shape=[jax.ShapeDtypeStruct((B, C, tt), F32), jax.ShapeDtypeStruct((B, C, tt), F32)],
        scratch_shapes=[pltpu.VMEM((nchain, ML_HEAD_DIM + BF16_ROWS, ML_HEAD_DIM), F32),
                        pltpu.VMEM((nchain, SUBLANES, LANES), F32)],
        compiler_params=_params(("arbitrary", "arbitrary")),
        name="mlstm_scan",
    )(gates, gates, q_t, k, v_t, q_t, k, v_t)


def _merge_kernel(hf_ref, hb_ref, z_ref, xc_ref, gmla_ref, gml_ref, ymla_ref, x_ref,
                  g1_ref, sh2_ref, sc2_ref, mln_ref, mls_ref, wout_ref, n2_ref, wr_ref,
                  x1_ref, h2_ref, aff_ref):
    h_t = hf_ref[0] + hb_ref[0]
    parts = []
    for hh in range(ML_HEADS):
        seg = h_t[hh * ML_HEAD_DIM:(hh + 1) * ML_HEAD_DIM]
        parts.append((seg * lax.rsqrt(jnp.mean(seg * seg, axis=0, keepdims=True) + EPS)).T)
    hn = jnp.concatenate(parts, axis=-1) * mln_ref[...]
    y_ml = _sigmoid(z_ref[0]) * (hn + mls_ref[...] * xc_ref[0])
    merged = _sigmoid(gmla_ref[0]) * ymla_ref[0] + _sigmoid(gml_ref[0]) * y_ml
    out = jnp.dot(merged.astype(BF16), wout_ref[...], preferred_element_type=F32)
    x1 = x_ref[0] + g1_ref[0] * out
    x1_ref[0] = x1
    h2 = _rms(x1, n2_ref[...]) * (1.0 + sc2_ref[0]) + sh2_ref[0]
    h2_ref[0] = h2.astype(BF16)
    logits = lax.dot_general(wr_ref[...], h2, _NT, preferred_element_type=F32,
                             precision=lax.Precision.HIGHEST)
    e = jnp.exp(logits - jnp.max(logits, axis=0, keepdims=True))
    aff_ref[0] = e / jnp.sum(e, axis=0, keepdims=True)


def _merge(hf, hb, z, xc, gmla, gml, ymla, x, mod3, ml_norm, ml_skip, w_out, norm2, w_router_t):
    B, T, D = x.shape
    nt = T // ROW_TILE
    const2 = lambda b, i: (0, 0)
    lat = lambda b, i: (b, i, 0)
    shifted = lambda b, i: (b, i + 1, 0)
    modcol = lambda col: (lambda b, i: (b, 0, col))
    tile = lambda idx: pl.BlockSpec((1, ROW_TILE, D), idx)
    h_tile = pl.BlockSpec((1, ML_INNER, ROW_TILE), lambda b, i: (b, 0, i + 1))
    return pl.pallas_call(
        _merge_kernel,
        grid=(B, nt),
        in_specs=[h_tile, h_tile, tile(lat), tile(shifted), tile(lat), tile(lat), tile(lat),
                  tile(lat),
                  pl.BlockSpec((1, 1, D), modcol(2)), pl.BlockSpec((1, 1, D), modcol(3)),
                  pl.BlockSpec((1, 1, D), modcol(4)),
                  pl.BlockSpec((1, D), const2), pl.BlockSpec((1, D), const2),
                  pl.BlockSpec((D, D), const2), pl.BlockSpec((1, D), const2),
                  pl.BlockSpec((N_EXPERTS, D), const2)],
        out_specs=[tile(lat), tile(lat), pl.BlockSpec((1, N_EXPERTS, ROW_TILE), lambda b, i: (b, 0, i))],
        out_shape=[jax.ShapeDtypeStruct((B, T, D), F32),
                   jax.ShapeDtypeStruct((B, T, D), BF16),
                   jax.ShapeDtypeStruct((B, N_EXPERTS, T), F32)],
        compiler_params=_params(("arbitrary", "arbitrary")),
        name="merge_router",
    )(hf, hb, z, xc, gmla, gml, ymla, x, mod3, mod3, mod3, ml_norm, ml_skip, w_out, norm2, w_router_t)


def _chunked_cumsum(mask_f, tri):
    n_e, t = mask_f.shape
    off = jnp.zeros((n_e, 1), F32)
    outs = []
    for c in range(t // LANES):
        x = mask_f[:, c * LANES:(c + 1) * LANES]
        inc = jnp.dot(x.astype(BF16), tri, preferred_element_type=F32)
        outs.append(inc - x + off)
        off = off + inc[:, LANES - 1:LANES]
    return jnp.concatenate(outs, axis=1)


def _select_kernel(aff_ref, pos_ref, cs_ref, *, cap):
    aff = aff_ref[0]
    n_e = aff.shape[0]

    def count_ge(t):
        return jnp.sum(jnp.where(aff >= t, 1.0, 0.0), axis=1, keepdims=True)

    def body(carry):
        lo, hi, _ = carry
        mid = 0.5 * (lo + hi)
        ok = count_ge(mid) >= cap
        lo, hi = jnp.where(ok, mid, lo), jnp.where(ok, hi, mid)
        mid = 0.5 * (lo + hi)
        return lo, hi, jnp.max(jnp.where((mid > lo) & (mid < hi), 1.0, 0.0))

    lo, hi, _ = lax.while_loop(lambda carry: carry[2] > 0.5, body,
                               (jnp.zeros((n_e, 1), F32), jnp.full((n_e, 1), 2.0, F32), jnp.float32(1.0)))
    gt = jnp.where(aff >= hi, 1.0, 0.0)
    eq = jnp.where(aff >= lo, 1.0, 0.0) - gt
    need = cap - jnp.sum(gt, axis=1, keepdims=True)
    tri = jnp.where(lax.broadcasted_iota(I32, (LANES, LANES), 0) <= lax.broadcasted_iota(I32, (LANES, LANES), 1),
                    1.0, 0.0).astype(BF16)
    eq_rank = _chunked_cumsum(eq, tri)
    sel = gt + eq * jnp.where(eq_rank < need, 1.0, 0.0)
    cs = _chunked_cumsum(sel, tri)
    cs_ref[0] = cs.astype(I32)
    pos_ref[0] = jnp.where(sel > 0.5, cs, -1.0).astype(I32)


def _select(aff_t, cap):
    B, n_e, T = aff_t.shape
    blk = pl.BlockSpec((1, n_e, T), lambda b: (b, 0, 0))
    return pl.pallas_call(
        functools.partial(_select_kernel, cap=cap),
        grid=(B,),
        in_specs=[blk],
        out_specs=[blk, blk],
        out_shape=[jax.ShapeDtypeStruct((B, n_e, T), I32), jax.ShapeDtypeStruct((B, n_e, T), I32)],
        compiler_params=_params(("arbitrary",)),
        name="expert_select",
    )(aff_t)


def _window_start(base_ref, flat):
    start = base_ref[flat]
    return pl.multiple_of((start // BF16_ROWS) * BF16_ROWS, BF16_ROWS)


def _expert_kernel(base_ref, x_ref, pos_ref, wg_ref, wu_ref, wd_ref, y_ref, xs_ref, *, n_tiles, cap):
    b, e = pl.program_id(0), pl.program_id(1)
    head = BF16_ROWS
    xs_ref[0:head, :] = jnp.zeros((head, xs_ref.shape[1]), BF16)
    row_id = lax.broadcasted_iota(I32, (MOE_WIN, MOE_TILE), 0)

    def gather(k, carry):
        s16 = _window_start(base_ref, (b * N_EXPERTS + e) * (n_tiles + 1) + k)
        rel = pos_ref[0, pl.ds(k, 1), :] - s16
        onehot = jnp.where(row_id == rel, 1.0, 0.0).astype(BF16)
        tok = pl.ds(pl.multiple_of(k * MOE_TILE, MOE_TILE), MOE_TILE)
        rows = jnp.dot(onehot, x_ref[0, tok, :], preferred_element_type=F32)
        first = xs_ref[pl.ds(s16, head), :].astype(F32) + rows[:head]
        xs_ref[pl.ds(s16 + head, MOE_WIN - head), :] = rows[head:].astype(BF16)
        xs_ref[pl.ds(s16, head), :] = first.astype(BF16)
        return carry

    lax.fori_loop(0, n_tiles, gather, 0, unroll=4)

    for r in range(cap // ROW_TILE):
        rows = slice(r * ROW_TILE, (r + 1) * ROW_TILE)
        xs = xs_ref[rows, :]
        a = jnp.dot(xs, wg_ref[0], preferred_element_type=F32)
        u = jnp.dot(xs, wu_ref[0], preferred_element_type=F32)
        hm = (a * _sigmoid(a) * u).astype(BF16)
        y_ref[0, 0, rows, :] = jnp.dot(hm, wd_ref[0], preferred_element_type=F32).astype(BF16)
    y_ref[0, 0, cap:, :] = jnp.zeros((y_ref.shape[2] - cap, y_ref.shape[3]), BF16)


def _experts(base, h2, pos3, wg, wu, wd, cap):
    B, T, D = h2.shape
    nk = T // MOE_TILE
    yr = cap + MOE_WIN
    ff = wg.shape[2]
    wspec = lambda shape: pl.BlockSpec((1,) + shape, lambda b, e, base: (e, 0, 0))
    grid_spec = pltpu.PrefetchScalarGridSpec(
        num_scalar_prefetch=1,
        grid=(B, N_EXPERTS),
        in_specs=[pl.BlockSpec((1, T, D), lambda b, e, base: (b, 0, 0), pipeline_mode=pl.Buffered(1)),
                  pl.BlockSpec((1, nk, MOE_TILE), lambda b, e, base: (b * N_EXPERTS + e, 0, 0)),
                  wspec((D, ff)), wspec((D, ff)), wspec((ff, D))],
        out_specs=pl.BlockSpec((1, 1, yr, D), lambda b, e, base: (b, e, 0, 0)),
        scratch_shapes=[pltpu.VMEM((yr, D), BF16)])
    return pl.pallas_call(
        functools.partial(_expert_kernel, n_tiles=nk, cap=cap),
        grid_spec=grid_spec,
        out_shape=jax.ShapeDtypeStruct((B, N_EXPERTS, yr, D), BF16),
        compiler_params=_params(("arbitrary", "arbitrary")),
        name="expert_ffn",
    )(base, h2, pos3, wg, wu, wd)


def _combine_kernel(base_ref, y_ref, pos_ref, g_ref, x1_ref, g2_ref, fn_ref, o_ref, *, n_tiles, tiles_per_blk):
    b, tb, e = pl.program_id(0), pl.program_id(1), pl.program_id(2)
    flat0 = (b * N_EXPERTS + e) * (n_tiles + 1) + tb * tiles_per_blk
    pick = lax.broadcasted_iota(I32, (MOE_TILE, N_EXPERTS), 1) == e
    col_id = lax.broadcasted_iota(I32, (MOE_TILE, MOE_TILE), 1).astype(F32)

    @pl.when(e == 0)
    def _():
        o_ref[...] = jnp.zeros_like(o_ref)

    def columns(k):
        rows = slice(k * MOE_TILE, (k + 1) * MOE_TILE)
        pcol = jnp.sum(jnp.where(pick, pos_ref[0, rows, :].astype(F32), 0.0), axis=1, keepdims=True)
        gcol = jnp.sum(jnp.where(pick, g_ref[0, rows, :], 0.0), axis=1, keepdims=True)
        return rows, pcol, gcol

    for k in range(tiles_per_blk):
        s16 = _window_start(base_ref, flat0 + k)
        rows, pcol, gcol = columns(k)
        onehot = jnp.where(col_id == pcol - s16.astype(F32), 1.0, 0.0).astype(BF16)
        o_ref[0, rows, :] += jnp.dot(onehot, y_ref[0, 0, pl.ds(s16, MOE_TILE), :],
                                     preferred_element_type=F32) * gcol

    for k in range(tiles_per_blk):
        s16 = _window_start(base_ref, flat0 + k)

        @pl.when(base_ref[flat0 + k + 1] > s16 + MOE_TILE)
        def _():
            rows, pcol, gcol = columns(k)
            tail = lax.broadcasted_iota(I32, (MOE_TILE, BF16_ROWS), 1).astype(F32) + float(MOE_TILE)
            onehot = jnp.where(tail == pcol - s16.astype(F32), 1.0, 0.0).astype(BF16)
            o_ref[0, rows, :] += jnp.dot(onehot, y_ref[0, 0, pl.ds(s16 + MOE_TILE, BF16_ROWS), :],
                                         preferred_element_type=F32) * gcol

    @pl.when(e == N_EXPERTS - 1)
    def _():
        for k in range(tiles_per_blk):
            rows = slice(k * MOE_TILE, (k + 1) * MOE_TILE)
            x2 = x1_ref[0, rows, :] + g2_ref[0] * o_ref[0, rows, :]
            o_ref[0, rows, :] = _rms(x2, fn_ref[...])


def _combine(base, y, pos_t, g_t, x1, mod3, final_norm):
    B, T, D = x1.shape
    nk = T // MOE_TILE
    per = 8
    blk = per * MOE_TILE
    yr = y.shape[2]
    tok = lambda b, tb, e, base: (b, tb, 0)
    grid_spec = pltpu.PrefetchScalarGridSpec(
        num_scalar_prefetch=1,
        grid=(B, nk // per, N_EXPERTS),
        in_specs=[pl.BlockSpec((1, 1, yr, D), lambda b, tb, e, base: (b, e, 0, 0)),
                  pl.BlockSpec((1, blk, N_EXPERTS), tok),
                  pl.BlockSpec((1, blk, N_EXPERTS), tok),
                  pl.BlockSpec((1, blk, D), tok),
                  pl.BlockSpec((1, 1, D), lambda b, tb, e, base: (b, 0, 5)),
                  pl.BlockSpec((1, D), lambda b, tb, e, base: (0, 0))],
        out_specs=pl.BlockSpec((1, blk, D), tok))
    return pl.pallas_call(
        functools.partial(_combine_kernel, n_tiles=nk, tiles_per_blk=per),
        grid_spec=grid_spec,
        out_shape=jax.ShapeDtypeStruct((B, T, D), F32),
        compiler_params=_params(("arbitrary",) * 3),
        name="moe_combine",
    )(base, y, pos_t, g_t, x1, mod3, final_norm)


def _rope_tables(T):
    rows = T // GRID_W
    row = jnp.repeat(jnp.arange(rows, dtype=F32), GRID_W)
    col = jnp.tile(jnp.arange(GRID_W, dtype=F32), rows)
    inv = ROPE_BASE ** (-jnp.arange(ROPE_PAIRS, dtype=F32) / ROPE_PAIRS)
    ar, ac = row[:, None] * inv, col[:, None] * inv
    ones = jnp.ones((T, LANES - QK_ROPE), F32)
    cos = jnp.concatenate([jnp.cos(ar), jnp.cos(ar), jnp.cos(ac), jnp.cos(ac), ones], axis=1)
    sin = jnp.concatenate([-jnp.sin(ar), jnp.sin(ar), -jnp.sin(ac), jnp.sin(ac), 0.0 * ones], axis=1)
    cos = jnp.concatenate([jnp.ones((ROW_TILE, LANES), F32), cos], axis=0)
    sin = jnp.concatenate([jnp.zeros((ROW_TILE, LANES), F32), sin], axis=0)
    return cos, sin


def _blockdiag_dense(w):
    n, bs, _ = w.shape
    rows = jnp.broadcast_to(w.transpose(1, 0, 2).reshape(1, bs, n * bs), (n, bs, n * bs)).reshape(n * bs, n * bs)
    r = jnp.arange(n * bs) // bs
    return jnp.where(r[:, None] == r[None, :], rows, 0.0).astype(BF16)


def kernel(x, c, ctx, c_ctx, w_mod, b_mod, norm1, w_in, q_norm, w_uq, kv_norm, w_ukv, conv_w, conv_b,
           w_qblk, w_kblk, w_vblk, w_gate, b_gate, ml_norm, ml_skip, w_out, norm2, w_router,
           w_e_gate, w_e_up, w_e_down, final_norm):
    B, T, D = x.shape
    assert w_mod.shape[0] == 1 and D == D_MODEL and ctx.shape[1] == ROW_TILE
    cap = CAP_FACTOR * T // N_EXPERTS

    cc = jnp.zeros((SUBLANES, D), F32).at[:B].set(c).at[B].set(c_ctx)
    mod = _modulation(cc, w_mod[0], b_mod[0].reshape(1, -1))
    mod3 = mod[:B + 1].reshape(B + 1, 1, 6 * D)

    wi = w_in[0]
    zpad = jnp.zeros((D, LANES - QK_ROPE), F32)
    w_in_p = jnp.concatenate([wi[:, :_C_KROPE + QK_ROPE], zpad, wi[:, _C_KROPE + QK_ROPE:]], axis=1).astype(BF16)
    wuq = w_uq[0].reshape(Q_LORA, MLA_HEADS, QK_NOPE + QK_ROPE)
    wuq_nope = wuq[:, :, :QK_NOPE].reshape(Q_LORA, MLA_HEADS * QK_NOPE).astype(BF16)
    wuq_pe = jnp.pad(wuq[:, :, QK_NOPE:], ((0, 0), (0, 0), (0, LANES - QK_ROPE))).reshape(
        Q_LORA, MLA_HEADS * LANES).astype(BF16)
    wukv = w_ukv[0].reshape(KV_LORA, MLA_HEADS, QK_NOPE + V_DIM)
    wukv_p = jnp.concatenate([wukv[:, :, :QK_NOPE].reshape(KV_LORA, -1),
                              wukv[:, :, QK_NOPE:].reshape(KV_LORA, -1)], axis=1).astype(BF16)
    cos_t, sin_t = _rope_tables(T)

    q, k, v, xm, z, gmla, gml = _input_projection(
        ctx, x, mod3, norm1, w_in_p, q_norm, wuq_nope, wuq_pe, kv_norm, wukv_p, cos_t, sin_t)

    conv_w8 = jnp.zeros((SUBLANES, ML_INNER), F32).at[:CONV_W].set(conv_w[0])
    wg3 = w_gate[0].reshape(3, ML_INNER, 4 * ML_HEADS).transpose(0, 2, 1).astype(BF16)
    mq, mk, mv, xc, gates = _mlstm_features(
        xm, conv_w8, conv_b, _blockdiag_dense(w_qblk[0].transpose(0, 2, 1)), _blockdiag_dense(w_kblk[0]),
        _blockdiag_dense(w_vblk[0].transpose(0, 2, 1)), wg3, b_gate[0].reshape(-1, 1))

    y_mla = _attention(q, k, v)
    hf, hb = _mlstm_scan(gates, mq, mk, mv)

    x1, h2, aff_t = _merge(hf, hb, z, xc, gmla, gml, y_mla, x, mod3, ml_norm, ml_skip,
                           w_out[0].astype(BF16), norm2, w_router[0].T)

    pos, cs = _select(aff_t, cap)
    base = jnp.concatenate([cs[:, :, ::MOE_TILE], jnp.full((B, N_EXPERTS, 1), cap, I32)], axis=2).reshape(-1)
    y = _experts(base, h2, pos.reshape(B * N_EXPERTS, T // MOE_TILE, MOE_TILE), w_e_gate[0].astype(BF16),
                 w_e_up[0].astype(BF16), w_e_down[0].astype(BF16), cap)
    return _combine(base, y, pos.transpose(0, 2, 1), aff_t.transpose(0, 2, 1), x1, mod3,
                    final_norm.reshape(1, -1))
```

```python
import functools

import numpy as np
import jax
import jax.numpy as jnp
from jax import lax
from jax.experimental import pallas as pl
from jax.experimental.pallas import tpu as pltpu

F32 = jnp.float32
BF16 = jnp.bfloat16
I32 = jnp.int32

D_MODEL = 1024
GRID_W = 64
MLA_HEADS = 8
QK_NOPE = 128
QK_ROPE = 64
V_DIM = 128
Q_LORA = 384
KV_LORA = 256
ROPE_BASE = 10000.0
ROPE_PAIRS = QK_ROPE // 4
ATTN_SCALE = (QK_NOPE + QK_ROPE) ** -0.5
Q_SCALE = ATTN_SCALE * 1.4426950408889634
ML_HEADS = 4
ML_INNER = 1024
ML_HEAD_DIM = ML_INNER // ML_HEADS
QKV_BLOCK = 4
CONV_W = 5
CHUNK = 128
N_EXPERTS = 16
EXPERT_FF = 1024
CAP_FACTOR = 2
EPS = 1e-6

LANES = 128
SUBLANES = 8
BF16_ROWS = 16
ROW_TILE = 256
HEAD_W = 256
ATTN_TQ = 256
ATTN_TK = 768
MOE_TILE = 256
MOE_WIN = MOE_TILE + BF16_ROWS
VMEM_LIMIT = 56 * 1024 * 1024

_C_QLAT = 0
_C_KVLAT = Q_LORA
_C_KROPE = Q_LORA + KV_LORA
_C_XM = _C_KROPE + LANES
_C_Z = _C_XM + ML_INNER
_C_GMLA = _C_Z + ML_INNER
_C_GML = _C_GMLA + D_MODEL
IN_PAD = _C_GML + D_MODEL

_NT = (((1,), (1,)), ((), ()))
_TN = (((0,), (0,)), ((), ()))


def _params(sem, vmem=VMEM_LIMIT):
    return pltpu.CompilerParams(dimension_semantics=sem, vmem_limit_bytes=vmem)


def _rms(x, g):
    return x * lax.rsqrt(jnp.mean(x * x, axis=-1, keepdims=True) + EPS) * g


def _sigmoid(x):
    return jax.nn.sigmoid(x)


def _mod_kernel(c_ref, w_ref, b_ref, o_ref):
    c = c_ref[...]
    s = c * _sigmoid(c)
    o_ref[...] = jnp.dot(s, w_ref[...], preferred_element_type=F32,
                         precision=lax.Precision.HIGHEST) + b_ref[...]


def _modulation(cc, w_mod, b_mod):
    n = w_mod.shape[1]
    tn = 768
    return pl.pallas_call(
        _mod_kernel,
        grid=(n // tn,),
        in_specs=[pl.BlockSpec((SUBLANES, D_MODEL), lambda j: (0, 0)),
                  pl.BlockSpec((D_MODEL, tn), lambda j: (0, j)),
                  pl.BlockSpec((1, tn), lambda j: (0, j))],
        out_specs=pl.BlockSpec((SUBLANES, tn), lambda j: (0, j)),
        out_shape=jax.ShapeDtypeStruct((SUBLANES, n), F32),
        compiler_params=_params(("arbitrary",)),
        name="modulation",
    )(cc, w_mod, b_mod)


def _rope(v, cos, sin):
    lane = lax.broadcasted_iota(I32, v.shape, 1)
    partner = jnp.where(lane % 32 < 16, pltpu.roll(v, LANES - 16, 1), pltpu.roll(v, 16, 1))
    return v * cos + partner * sin


def _inproj_kernel(ctx_ref, x_ref, sh_ref, sc_ref, n1_ref, win_ref, qn_ref, wuqn_ref, wuqp_ref,
                   kvn_ref, wukv_ref, cos_ref, sin_ref,
                   q_ref, k_ref, v_ref, xm_ref, z_ref, gmla_ref, gml_ref):
    i = pl.program_id(1)
    xin = jnp.where(i == 0, ctx_ref[0], x_ref[0])
    h = _rms(xin, n1_ref[...]) * (1.0 + sc_ref[0]) + sh_ref[0]
    big = jnp.dot(h.astype(BF16), win_ref[...], preferred_element_type=F32)
    xm_ref[0] = big[:, _C_XM:_C_Z]
    z_ref[0] = big[:, _C_Z:_C_GMLA].astype(BF16)
    gmla_ref[0] = big[:, _C_GMLA:_C_GML].astype(BF16)
    gml_ref[0] = big[:, _C_GML:IN_PAD].astype(BF16)

    cos = cos_ref[...]
    sin = sin_ref[...]
    qn = _rms(big[:, _C_QLAT:_C_KVLAT], qn_ref[...]).astype(BF16)
    q_nope = jnp.dot(qn, wuqn_ref[...], preferred_element_type=F32)
    q_pe = jnp.dot(qn, wuqp_ref[...], preferred_element_type=F32)
    kvn = _rms(big[:, _C_KVLAT:_C_KROPE], kvn_ref[...]).astype(BF16)
    kv = jnp.dot(kvn, wukv_ref[...], preferred_element_type=F32)
    k_pe = _rope(big[:, _C_KROPE:_C_XM], cos, sin).astype(BF16)
    ones_col = jnp.where(lax.broadcasted_iota(I32, (big.shape[0], HEAD_W - V_DIM), 1) == 0, 1.0, 0.0).astype(BF16)
    for hh in range(MLA_HEADS):
        lo = hh * HEAD_W
        nope = slice(hh * QK_NOPE, (hh + 1) * QK_NOPE)
        q_ref[0, :, lo:lo + QK_NOPE] = (q_nope[:, nope] * Q_SCALE).astype(BF16)
        q_ref[0, :, lo + QK_NOPE:lo + HEAD_W] = (
            _rope(q_pe[:, hh * LANES:(hh + 1) * LANES], cos, sin) * Q_SCALE).astype(BF16)
        k_ref[0, :, lo:lo + QK_NOPE] = kv[:, nope].astype(BF16)
        k_ref[0, :, lo + QK_NOPE:lo + HEAD_W] = k_pe
        v_ref[0, :, lo:lo + V_DIM] = kv[:, MLA_HEADS * QK_NOPE + hh * V_DIM:MLA_HEADS * QK_NOPE + (hh + 1) * V_DIM].astype(BF16)
        v_ref[0, :, lo + V_DIM:lo + HEAD_W] = ones_col


def _input_projection(ctx, x, mod3, norm1, w_in_p, q_norm, wuq_nope, wuq_pe, kv_norm, wukv_p, cos_t, sin_t):
    B, T, D = x.shape
    n_lat = T // ROW_TILE
    nt = n_lat + 1
    tt = T + ROW_TILE
    const2 = lambda b, i: (0, 0)
    lat = lambda b, i: (b, jnp.maximum(i - 1, 0), 0)
    allrows = lambda b, i: (b, i, 0)
    modrow = lambda col: (lambda b, i: (jnp.where(i == 0, B, b), 0, col))
    return pl.pallas_call(
        _inproj_kernel,
        grid=(B, nt),
        in_specs=[pl.BlockSpec((1, ROW_TILE, D), lambda b, i: (b, 0, 0)),
                  pl.BlockSpec((1, ROW_TILE, D), lat),
                  pl.BlockSpec((1, 1, D), modrow(0)),
                  pl.BlockSpec((1, 1, D), modrow(1)),
                  pl.BlockSpec((1, D), const2),
                  pl.BlockSpec(w_in_p.shape, const2),
                  pl.BlockSpec((1, Q_LORA), const2),
                  pl.BlockSpec(wuq_nope.shape, const2),
                  pl.BlockSpec(wuq_pe.shape, const2),
                  pl.BlockSpec((1, KV_LORA), const2),
                  pl.BlockSpec(wukv_p.shape, const2),
                  pl.BlockSpec((ROW_TILE, LANES), lambda b, i: (i, 0)),
                  pl.BlockSpec((ROW_TILE, LANES), lambda b, i: (i, 0))],
        out_specs=[pl.BlockSpec((1, ROW_TILE, MLA_HEADS * HEAD_W), lat),
                   pl.BlockSpec((1, ROW_TILE, MLA_HEADS * HEAD_W), allrows),
                   pl.BlockSpec((1, ROW_TILE, MLA_HEADS * HEAD_W), allrows),
                   pl.BlockSpec((1, ROW_TILE, ML_INNER), allrows),
                   pl.BlockSpec((1, ROW_TILE, ML_INNER), lat),
                   pl.BlockSpec((1, ROW_TILE, D), lat),
                   pl.BlockSpec((1, ROW_TILE, D), lat)],
        out_shape=[jax.ShapeDtypeStruct((B, T, MLA_HEADS * HEAD_W), BF16),
                   jax.ShapeDtypeStruct((B, tt, MLA_HEADS * HEAD_W), BF16),
                   jax.ShapeDtypeStruct((B, tt, MLA_HEADS * HEAD_W), BF16),
                   jax.ShapeDtypeStruct((B, tt, ML_INNER), F32),
                   jax.ShapeDtypeStruct((B, T, ML_INNER), BF16),
                   jax.ShapeDtypeStruct((B, T, D), BF16),
                   jax.ShapeDtypeStruct((B, T, D), BF16)],
        compiler_params=_params(("arbitrary", "arbitrary")),
        name="input_projection",
    )(ctx, x, mod3, mod3, norm1, w_in_p, q_norm, wuq_nope, wuq_pe, kv_norm, wukv_p, cos_t, sin_t)


def _feat_kernel(prev_ref, cur_ref, next_ref, cw_ref, cb_ref, wq_ref, wk_ref, wv_ref, wg_ref, bg_ref,
                 q_ref, k_ref, v_ref, xc_ref, g_ref, *, n_tiles):
    i = pl.program_id(1)
    cur = cur_ref[0]
    prev = jnp.where(i <= 1, 0.0, prev_ref[0])
    nxt = jnp.where((i == 0) | (i == n_tiles - 1), 0.0, next_ref[0])
    xx = jnp.concatenate([prev, cur, nxt], axis=0)
    rows = cur.shape[0]
    acc = jnp.broadcast_to(cb_ref[...], cur.shape)
    for w in range(CONV_W):
        lo = SUBLANES - CONV_W // 2 + w
        acc = acc + xx[lo:lo + rows] * cw_ref[w:w + 1, :]
    xc = acc * _sigmoid(acc)
    xc_ref[0] = xc.astype(BF16)
    xcb = xc.astype(BF16)
    q_t = lax.dot_general(wq_ref[...], xcb, _NT, preferred_element_type=F32)
    k = jnp.dot(xcb, wk_ref[...], preferred_element_type=F32)
    v_t = lax.dot_general(wv_ref[...], cur.astype(BF16), _NT, preferred_element_type=F32)
    qb, kb, vb = q_t.astype(BF16), k.astype(BF16), v_t.astype(BF16)
    q_ref[0] = qb
    k_ref[0] = (k * (ML_HEAD_DIM ** -0.5)).astype(BF16)
    v_ref[0] = vb
    g = (jnp.dot(wg_ref[0], qb, preferred_element_type=F32)
         + lax.dot_general(wg_ref[1], kb, _NT, preferred_element_type=F32)
         + jnp.dot(wg_ref[2], vb, preferred_element_type=F32)) + bg_ref[...]
    row = lax.broadcasted_iota(I32, g.shape, 0)
    is_forget = (row % (2 * ML_HEADS)) >= ML_HEADS
    g_ref[0] = jnp.where(is_forget, jax.nn.log_sigmoid(g), g)


def _mlstm_features(xm, conv_w8, conv_b, wq_bd_t, wk_bd, wv_bd_t, wg3, bg_col):
    B, tt, C = xm.shape
    nt = tt // ROW_TILE
    per = ROW_TILE // SUBLANES
    last8 = tt // SUBLANES - 1
    const2 = lambda b, i: (0, 0)
    rows = lambda b, i: (b, i, 0)
    cols = lambda b, i: (b, 0, i)
    ng = 4 * ML_HEADS
    return pl.pallas_call(
        functools.partial(_feat_kernel, n_tiles=nt),
        grid=(B, nt),
        in_specs=[pl.BlockSpec((1, SUBLANES, C), lambda b, i: (b, jnp.maximum(i * per - 1, 0), 0)),
                  pl.BlockSpec((1, ROW_TILE, C), rows),
                  pl.BlockSpec((1, SUBLANES, C), lambda b, i: (b, jnp.minimum((i + 1) * per, last8), 0)),
                  pl.BlockSpec((SUBLANES, C), const2),
                  pl.BlockSpec((1, C), const2),
                  pl.BlockSpec((C, C), const2),
                  pl.BlockSpec((C, C), const2),
                  pl.BlockSpec((C, C), const2),
                  pl.BlockSpec((3, ng, C), lambda b, i: (0, 0, 0)),
                  pl.BlockSpec((ng, 1), const2)],
        out_specs=[pl.BlockSpec((1, C, ROW_TILE), cols),
                   pl.BlockSpec((1, ROW_TILE, C), rows),
                   pl.BlockSpec((1, C, ROW_TILE), cols),
                   pl.BlockSpec((1, ROW_TILE, C), rows),
                   pl.BlockSpec((1, ng, ROW_TILE), cols)],
        out_shape=[jax.ShapeDtypeStruct((B, C, tt), BF16),
                   jax.ShapeDtypeStruct((B, tt, C), BF16),
                   jax.ShapeDtypeStruct((B, C, tt), BF16),
                   jax.ShapeDtypeStruct((B, tt, C), BF16),
                   jax.ShapeDtypeStruct((B, ng, tt), F32)],
        compiler_params=_params(("arbitrary", "arbitrary")),
        name="mlstm_features",
    )(xm, xm, xm, conv_w8, conv_b, wq_bd_t, wk_bd, wv_bd_t, wg3, bg_col)


def _attn_kernel(q_ref, k_ref, v_ref, o_ref, s_ref, p_ref, m_ref, *, n_chunks):
    j = pl.program_id(2)

    @pl.when(j == 0)
    def _():
        s_ref[...] = jnp.zeros_like(s_ref)
        p_ref[...] = jnp.ones_like(p_ref)
        m_ref[...] = jnp.zeros_like(m_ref)

    q = q_ref[0]
    m_old = m_ref[...]
    m = jnp.full((ATTN_TQ, 1), -jnp.inf, F32)
    acc = jnp.zeros((ATTN_TQ, HEAD_W), F32)
    for c in range(n_chunks):
        keys = slice(c * ATTN_TK, (c + 1) * ATTN_TK)
        acc = acc + jnp.dot(p_ref[c], v_ref[0, keys, :], preferred_element_type=F32)
        p_ref[c] = jnp.exp2(s_ref[c] - m_old).astype(BF16)
        s = lax.dot_general(q, k_ref[0, keys, :], _NT, preferred_element_type=F32)
        s_ref[c] = s
        m = jnp.maximum(m, jnp.max(s, axis=-1, keepdims=True))
    m_ref[...] = m
    o_ref[0] = (acc[:, :V_DIM] / acc[:, V_DIM:V_DIM + 1]).astype(BF16)


def _attention(q, k, v):
    B, T, _ = q.shape
    tt = k.shape[1]
    n_chunks = tt // ATTN_TK
    n_tiles = T // ATTN_TQ
    lag = 2
    assert n_chunks * ATTN_TK == tt and n_tiles * ATTN_TQ == T
    return pl.pallas_call(
        functools.partial(_attn_kernel, n_chunks=n_chunks),
        grid=(B, MLA_HEADS, n_tiles + lag),
        in_specs=[pl.BlockSpec((1, ATTN_TQ, HEAD_W), lambda b, h, j: (b, jnp.minimum(j, n_tiles - 1), h)),
                  pl.BlockSpec((1, tt, HEAD_W), lambda b, h, j: (b, 0, h)),
                  pl.BlockSpec((1, tt, HEAD_W), lambda b, h, j: (b, 0, h))],
        out_specs=pl.BlockSpec((1, ATTN_TQ, V_DIM), lambda b, h, j: (b, jnp.maximum(j - lag, 0), h)),
        out_shape=jax.ShapeDtypeStruct((B, T, MLA_HEADS * V_DIM), BF16),
        scratch_shapes=[pltpu.VMEM((n_chunks, ATTN_TQ, ATTN_TK), F32),
                        pltpu.VMEM((n_chunks, ATTN_TQ, ATTN_TK), BF16),
                        pltpu.VMEM((ATTN_TQ, 1), F32)],
        compiler_params=_params(("arbitrary", "arbitrary", "arbitrary")),
        name="attention",
    )(q, k, v)


def _lane_cumsum(x, reverse):
    lane = lax.broadcasted_iota(I32, x.shape, 1)
    n = x.shape[1]
    s = 1
    while s < n:
        if reverse:
            x = x + jnp.where(lane < n - s, pltpu.roll(x, n - s, 1), 0.0)
        else:
            x = x + jnp.where(lane >= s, pltpu.roll(x, s, 1), 0.0)
        s *= 2
    return x


def _scan_kernel(gf_ref, gb_ref, qf_ref, kf_ref, vf_ref, qb_ref, kb_ref, vb_ref,
                 hf_ref, hb_ref, cn_ref, m_ref):
    j = pl.program_id(1)
    L = CHUNK
    dh = ML_HEAD_DIM

    @pl.when(j == 0)
    def _():
        cn_ref[...] = jnp.zeros_like(cn_ref)
        m_ref[...] = jnp.zeros_like(m_ref)

    spos = lax.broadcasted_iota(I32, (L, L), 0)
    tpos = lax.broadcasted_iota(I32, (L, L), 1)
    first_row = lax.broadcasted_iota(I32, (BF16_ROWS, L), 0) == 0
    dirs = ((gf_ref, qf_ref, kf_ref, vf_ref, hf_ref), (gb_ref, qb_ref, kb_ref, vb_ref, hb_ref))
    for d, (g_ref, q_ref, k_ref, v_ref, h_ref) in enumerate(dirs):
        reverse = d == 1
        mask = (spos >= tpos) if reverse else (spos <= tpos)
        g = g_ref[0]
        ig4 = g[d * 2 * ML_HEADS:d * 2 * ML_HEADS + ML_HEADS]
        lf4 = g[d * 2 * ML_HEADS + ML_HEADS:(d + 1) * 2 * ML_HEADS]
        b4 = _lane_cumsum(lf4, reverse)
        a4 = ig4 - b4
        a_cols = jnp.concatenate([a4, jnp.zeros((L - ML_HEADS, L), F32)], axis=0).T
        for hh in range(ML_HEADS):
            ci = d * ML_HEADS + hh
            b_row, a_row, a_col = b4[hh:hh + 1], a4[hh:hh + 1], a_cols[:, hh:hh + 1]
            b_last = b_row[:, 0:1] if reverse else b_row[:, L - 1:L]
            m = m_ref[ci, 0:1, 0:1]
            sl = slice(hh * dh, (hh + 1) * dh)
            q_t, k, v_t = q_ref[0, sl, :], k_ref[0, :, sl], v_ref[0, sl, :]
            cn = cn_ref[ci]

            dmat = jnp.where(mask, b_row + a_col, -jnp.inf)
            inter = b_row + m
            m_t = jnp.maximum(inter, jnp.max(dmat, axis=0, keepdims=True))
            w_inter = jnp.exp(inter - m_t)
            s = jnp.dot(k, q_t, preferred_element_type=F32) * jnp.exp(dmat - m_t)
            cq = jnp.dot(cn.astype(BF16), q_t, preferred_element_type=F32)
            num = jnp.dot(v_t, s.astype(BF16), preferred_element_type=F32) + w_inter * cq[:dh]
            den = jnp.sum(s, axis=0, keepdims=True) + w_inter * cq[dh:dh + 1]
            h_ref[0, sl, :] = (num / jnp.maximum(jnp.abs(den), jnp.exp(-m_t))).astype(BF16)

            dec = b_last + a_row
            m_new = jnp.maximum(b_last + m, jnp.max(dec, axis=-1, keepdims=True))
            wk = jnp.exp(dec - m_new)
            keep = jnp.exp(b_last + m - m_new)
            vw = jnp.concatenate([(v_t.astype(F32) * wk).astype(BF16),
                                  jnp.where(first_row, wk, 0.0).astype(BF16)], axis=0)
            cn_ref[ci] = keep * cn + jnp.dot(vw, k, preferred_element_type=F32)
            m_ref[ci] = jnp.broadcast_to(m_new, m_ref.shape[1:])


def _mlstm_scan(gates, q_t, k, v_t):
    B, tt, C = k.shape
    nch = tt // CHUNK
    ng = gates.shape[1]
    nc_ctx = ROW_TILE // CHUNK
    bidx = lambda j: jnp.where(j < nc_ctx, nc_ctx - 1 - j, nch - 1 + nc_ctx - j)
    rows_f = pl.BlockSpec((1, CHUNK, C), lambda b, j: (b, j, 0))
    rows_b = pl.BlockSpec((1, CHUNK, C), lambda b, j: (b, bidx(j), 0))
    cols_f = pl.BlockSpec((1, C, CHUNK), lambda b, j: (b, 0, j))
    cols_b = pl.BlockSpec((1, C, CHUNK), lambda b, j: (b, 0, bidx(j)))
    nchain = 2 * ML_HEADS
    return pl.pallas_call(
        _scan_kernel,
        grid=(B, nch),
        in_specs=[pl.BlockSpec((1, ng, CHUNK), lambda b, j: (b, 0, j)),
                  pl.BlockSpec((1, ng, CHUNK), lambda b, j: (b, 0, bidx(j))),
                  cols_f, rows_f, cols_f, cols_b, rows_b, cols_b],
        out_specs=[cols_f, cols_b],
        out_shape=[jax.ShapeDtypeStruct((B, C, tt), BF16), jax.ShapeDtypeStruct((B, C, tt), BF16)],
        scratch_shapes=[pltpu.VMEM((nchain, ML_HEAD_DIM + BF16_ROWS, ML_HEAD_DIM), F32),
                        pltpu.VMEM((nchain, SUBLANES, LANES), F32)],
        compiler_params=_params(("arbitrary", "arbitrary")),
        name="mlstm_scan",
    )(gates, gates, q_t, k, v_t, q_t, k, v_t)


def _merge_kernel(hf_ref, hb_ref, z_ref, xc_ref, gmla_ref, gml_ref, ymla_ref, x_ref,
                  g1_ref, sh2_ref, sc2_ref, mln_ref, mls_ref, wout_ref, n2_ref, wr_ref,
                  x1_ref, h2_ref, aff_ref):
    h_t = hf_ref[0].astype(F32) + hb_ref[0].astype(F32)
    parts = []
    for hh in range(ML_HEADS):
        seg = h_t[hh * ML_HEAD_DIM:(hh + 1) * ML_HEAD_DIM]
        parts.append((seg * lax.rsqrt(jnp.mean(seg * seg, axis=0, keepdims=True) + EPS)).T)
    hn = jnp.concatenate(parts, axis=-1) * mln_ref[...]
    y_ml = _sigmoid(z_ref[0].astype(F32)) * (hn + mls_ref[...] * xc_ref[0].astype(F32))
    merged = (_sigmoid(gmla_ref[0].astype(F32)) * ymla_ref[0].astype(F32)
              + _sigmoid(gml_ref[0].astype(F32)) * y_ml)
    out = jnp.dot(merged.astype(BF16), wout_ref[...], preferred_element_type=F32)
    x1 = x_ref[0] + g1_ref[0] * out
    x1_ref[0] = x1
    h2 = _rms(x1, n2_ref[...]) * (1.0 + sc2_ref[0]) + sh2_ref[0]
    h2_ref[0] = h2.astype(BF16)
    logits = lax.dot_general(wr_ref[...], h2, _NT, preferred_element_type=F32,
                             precision=lax.Precision.HIGHEST)
    e = jnp.exp(logits - jnp.max(logits, axis=0, keepdims=True))
    aff_ref[0] = e / jnp.sum(e, axis=0, keepdims=True)


def _merge(hf, hb, z, xc, gmla, gml, ymla, x, mod3, ml_norm, ml_skip, w_out, norm2, w_router_t):
    B, T, D = x.shape
    nt = T // ROW_TILE
    const2 = lambda b, i: (0, 0)
    lat = lambda b, i: (b, i, 0)
    shifted = lambda b, i: (b, i + 1, 0)
    modcol = lambda col: (lambda b, i: (b, 0, col))
    tile = lambda idx: pl.BlockSpec((1, ROW_TILE, D), idx)
    h_tile = pl.BlockSpec((1, ML_INNER, ROW_TILE), lambda b, i: (b, 0, i + 1))
    return pl.pallas_call(
        _merge_kernel,
        grid=(B, nt),
        in_specs=[h_tile, h_tile, tile(lat), tile(shifted), tile(lat), tile(lat), tile(lat),
                  tile(lat),
                  pl.BlockSpec((1, 1, D), modcol(2)), pl.BlockSpec((1, 1, D), modcol(3)),
                  pl.BlockSpec((1, 1, D), modcol(4)),
                  pl.BlockSpec((1, D), const2), pl.BlockSpec((1, D), const2),
                  pl.BlockSpec((D, D), const2), pl.BlockSpec((1, D), const2),
                  pl.BlockSpec((N_EXPERTS, D), const2)],
        out_specs=[tile(lat), tile(lat), pl.BlockSpec((1, N_EXPERTS, ROW_TILE), lambda b, i: (b, 0, i))],
        out_shape=[jax.ShapeDtypeStruct((B, T, D), F32),
                   jax.ShapeDtypeStruct((B, T, D), BF16),
                   jax.ShapeDtypeStruct((B, N_EXPERTS, T), F32)],
        compiler_params=_params(("arbitrary", "arbitrary")),
        name="merge_router",
    )(hf, hb, z, xc, gmla, gml, ymla, x, mod3, mod3, mod3, ml_norm, ml_skip, w_out, norm2, w_router_t)


def _chunked_cumsum(mask_f, tri):
    n_e, t = mask_f.shape
    off = jnp.zeros((n_e, 1), F32)
    outs = []
    for c in range(t // LANES):
        x = mask_f[:, c * LANES:(c + 1) * LANES]
        inc = jnp.dot(x.astype(BF16), tri, preferred_element_type=F32)
        outs.append(inc - x + off)
        off = off + inc[:, LANES - 1:LANES]
    return jnp.concatenate(outs, axis=1)


def _select_kernel(aff_ref, pos_ref, cs_ref, *, cap):
    aff = aff_ref[0]
    n_e = aff.shape[0]

    def count_ge(t):
        return jnp.sum(jnp.where(aff >= t, 1.0, 0.0), axis=1, keepdims=True)

    def body(carry):
        lo, hi, _ = carry
        mid = 0.5 * (lo + hi)
        ok = count_ge(mid) >= cap
        lo, hi = jnp.where(ok, mid, lo), jnp.where(ok, hi, mid)
        mid = 0.5 * (lo + hi)
        return lo, hi, jnp.max(jnp.where((mid > lo) & (mid < hi), 1.0, 0.0))

    lo, hi, _ = lax.while_loop(lambda carry: carry[2] > 0.5, body,
                               (jnp.zeros((n_e, 1), F32), jnp.full((n_e, 1), 2.0, F32), jnp.float32(1.0)))
    gt = jnp.where(aff >= hi, 1.0, 0.0)
    eq = jnp.where(aff >= lo, 1.0, 0.0) - gt
    need = cap - jnp.sum(gt, axis=1, keepdims=True)
    tri = jnp.where(lax.broadcasted_iota(I32, (LANES, LANES), 0) <= lax.broadcasted_iota(I32, (LANES, LANES), 1),
                    1.0, 0.0).astype(BF16)
    eq_rank = _chunked_cumsum(eq, tri)
    sel = gt + eq * jnp.where(eq_rank < need, 1.0, 0.0)
    cs = _chunked_cumsum(sel, tri)
    cs_ref[0] = cs.astype(I32)
    pos_ref[0] = jnp.where(sel > 0.5, cs, -1.0).astype(I32)


def _select(aff_t, cap):
    B, n_e, T = aff_t.shape
    blk = pl.BlockSpec((1, n_e, T), lambda b: (b, 0, 0))
    return pl.pallas_call(
        functools.partial(_select_kernel, cap=cap),
        grid=(B,),
        in_specs=[blk],
        out_specs=[blk, blk],
        out_shape=[jax.ShapeDtypeStruct((B, n_e, T), I32), jax.ShapeDtypeStruct((B, n_e, T), I32)],
        compiler_params=_params(("arbitrary",)),
        name="expert_select",
    )(aff_t)


def _window_start(base_ref, flat):
    start = base_ref[flat]
    return pl.multiple_of((start // BF16_ROWS) * BF16_ROWS, BF16_ROWS)


def _expert_kernel(base_ref, x_ref, pos_ref, wg_ref, wu_ref, wd_ref, y_ref, xs_ref, *, n_tiles, cap):
    b, e = pl.program_id(0), pl.program_id(1)
    head = BF16_ROWS
    xs_ref[0:head, :] = jnp.zeros((head, xs_ref.shape[1]), BF16)
    row_id = lax.broadcasted_iota(I32, (MOE_WIN, MOE_TILE), 0)

    def gather(k, carry):
        s16 = _window_start(base_ref, (b * N_EXPERTS + e) * (n_tiles + 1) + k)
        rel = pos_ref[0, pl.ds(k, 1), :] - s16
        onehot = jnp.where(row_id == rel, 1.0, 0.0).astype(BF16)
        tok = pl.ds(pl.multiple_of(k * MOE_TILE, MOE_TILE), MOE_TILE)
        rows = jnp.dot(onehot, x_ref[0, tok, :], preferred_element_type=F32)
        first = xs_ref[pl.ds(s16, head), :].astype(F32) + rows[:head]
        xs_ref[pl.ds(s16 + head, MOE_WIN - head), :] = rows[head:].astype(BF16)
        xs_ref[pl.ds(s16, head), :] = first.astype(BF16)
        return carry

    lax.fori_loop(0, n_tiles, gather, 0, unroll=4)

    for r in range(cap // ROW_TILE):
        rows = slice(r * ROW_TILE, (r + 1) * ROW_TILE)
        xs = xs_ref[rows, :]
        a = jnp.dot(xs, wg_ref[0], preferred_element_type=F32)
        u = jnp.dot(xs, wu_ref[0], preferred_element_type=F32)
        hm = (a * _sigmoid(a) * u).astype(BF16)
        y_ref[0, 0, rows, :] = jnp.dot(hm, wd_ref[0], preferred_element_type=F32).astype(BF16)
    y_ref[0, 0, cap:, :] = jnp.zeros((y_ref.shape[2] - cap, y_ref.shape[3]), BF16)


def _experts(base, h2, pos3, wg, wu, wd, cap):
    B, T, D = h2.shape
    nk = T // MOE_TILE
    yr = cap + MOE_WIN
    ff = wg.shape[2]
    wspec = lambda shape: pl.BlockSpec((1,) + shape, lambda b, e, base: (e, 0, 0))
    grid_spec = pltpu.PrefetchScalarGridSpec(
        num_scalar_prefetch=1,
        grid=(B, N_EXPERTS),
        in_specs=[pl.BlockSpec((1, T, D), lambda b, e, base: (b, 0, 0), pipeline_mode=pl.Buffered(1)),
                  pl.BlockSpec((1, nk, MOE_TILE), lambda b, e, base: (b * N_EXPERTS + e, 0, 0)),
                  wspec((D, ff)), wspec((D, ff)), wspec((ff, D))],
        out_specs=pl.BlockSpec((1, 1, yr, D), lambda b, e, base: (b, e, 0, 0)),
        scratch_shapes=[pltpu.VMEM((yr, D), BF16)])
    return pl.pallas_call(
        functools.partial(_expert_kernel, n_tiles=nk, cap=cap),
        grid_spec=grid_spec,
        out_shape=jax.ShapeDtypeStruct((B, N_EXPERTS, yr, D), BF16),
        compiler_params=_params(("arbitrary", "arbitrary")),
        name="expert_ffn",
    )(base, h2, pos3, wg, wu, wd)


def _combine_kernel(base_ref, y_ref, pos_ref, g_ref, x1_ref, g2_ref, fn_ref, o_ref, *, n_tiles, tiles_per_blk):
    b, tb, e = pl.program_id(0), pl.program_id(1), pl.program_id(2)
    flat0 = (b * N_EXPERTS + e) * (n_tiles + 1) + tb * tiles_per_blk
    pick = lax.broadcasted_iota(I32, (MOE_TILE, N_EXPERTS), 1) == e
    col_id = lax.broadcasted_iota(I32, (MOE_TILE, MOE_TILE), 1).astype(F32)

    @pl.when(e == 0)
    def _():
        o_ref[...] = jnp.zeros_like(o_ref)

    def columns(k):
        rows = slice(k * MOE_TILE, (k + 1) * MOE_TILE)
        pcol = jnp.sum(jnp.where(pick, pos_ref[0, rows, :].astype(F32), 0.0), axis=1, keepdims=True)
        gcol = jnp.sum(jnp.where(pick, g_ref[0, rows, :], 0.0), axis=1, keepdims=True)
        return rows, pcol, gcol

    for k in range(tiles_per_blk):
        s16 = _window_start(base_ref, flat0 + k)
        rows, pcol, gcol = columns(k)
        onehot = jnp.where(col_id == pcol - s16.astype(F32), 1.0, 0.0).astype(BF16)
        o_ref[0, rows, :] += jnp.dot(onehot, y_ref[0, 0, pl.ds(s16, MOE_TILE), :],
                                     preferred_element_type=F32) * gcol

    for k in range(tiles_per_blk):
        s16 = _window_start(base_ref, flat0 + k)

        @pl.when(base_ref[flat0 + k + 1] > s16 + MOE_TILE)
        def _():
            rows, pcol, gcol = columns(k)
            tail = lax.broadcasted_iota(I32, (MOE_TILE, BF16_ROWS), 1).astype(F32) + float(MOE_TILE)
            onehot = jnp.where(tail == pcol - s16.astype(F32), 1.0, 0.0).astype(BF16)
            o_ref[0, rows, :] += jnp.dot(onehot, y_ref[0, 0, pl.ds(s16 + MOE_TILE, BF16_ROWS), :],
                                         preferred_element_type=F32) * gcol

    @pl.when(e == N_EXPERTS - 1)
    def _():
        for k in range(tiles_per_blk):
            rows = slice(k * MOE_TILE, (k + 1) * MOE_TILE)
            x2 = x1_ref[0, rows, :] + g2_ref[0] * o_ref[0, rows, :]
            o_ref[0, rows, :] = _rms(x2, fn_ref[...])


def _combine(base, y, pos_t, g_t, x1, mod3, final_norm):
    B, T, D = x1.shape
    nk = T // MOE_TILE
    per = 8
    blk = per * MOE_TILE
    yr = y.shape[2]
    tok = lambda b, tb, e, base: (b, tb, 0)
    grid_spec = pltpu.PrefetchScalarGridSpec(
        num_scalar_prefetch=1,
        grid=(B, nk // per, N_EXPERTS),
        in_specs=[pl.BlockSpec((1, 1, yr, D), lambda b, tb, e, base: (b, e, 0, 0)),
                  pl.BlockSpec((1, blk, N_EXPERTS), tok),
                  pl.BlockSpec((1, blk, N_EXPERTS), tok),
                  pl.BlockSpec((1, blk, D), tok),
                  pl.BlockSpec((1, 1, D), lambda b, tb, e, base: (b, 0, 5)),
                  pl.BlockSpec((1, D), lambda b, tb, e, base: (0, 0))],
        out_specs=pl.BlockSpec((1, blk, D), tok))
    return pl.pallas_call(
        functools.partial(_combine_kernel, n_tiles=nk, tiles_per_blk=per),
        grid_spec=grid_spec,
        out_shape=jax.ShapeDtypeStruct((B, T, D), F32),
        compiler_params=_params(("arbitrary",) * 3),
        name="moe_combine",
    )(base, y, pos_t, g_t, x1, mod3, final_norm)


def _rope_tables(T):
    rows = T // GRID_W
    row = np.repeat(np.arange(rows, dtype=np.float64), GRID_W)
    col = np.tile(np.arange(GRID_W, dtype=np.float64), rows)
    inv = ROPE_BASE ** (-np.arange(ROPE_PAIRS, dtype=np.float64) / ROPE_PAIRS)
    ar, ac = row[:, None] * inv, col[:, None] * inv
    ones = np.ones((T, LANES - QK_ROPE))
    cos = np.concatenate([np.cos(ar), np.cos(ar), np.cos(ac), np.cos(ac), ones], axis=1)
    sin = np.concatenate([-np.sin(ar), np.sin(ar), -np.sin(ac), np.sin(ac), 0.0 * ones], axis=1)
    cos = np.concatenate([np.ones((ROW_TILE, LANES)), cos], axis=0)
    sin = np.concatenate([np.zeros((ROW_TILE, LANES)), sin], axis=0)
    return jnp.asarray(cos, F32), jnp.asarray(sin, F32)


def _blockdiag_dense(w):
    n, bs, _ = w.shape
    rows = jnp.broadcast_to(w.transpose(1, 0, 2).reshape(1, bs, n * bs), (n, bs, n * bs)).reshape(n * bs, n * bs)
    r = jnp.arange(n * bs) // bs
    return jnp.where(r[:, None] == r[None, :], rows, 0.0).astype(BF16)


def kernel(x, c, ctx, c_ctx, w_mod, b_mod, norm1, w_in, q_norm, w_uq, kv_norm, w_ukv, conv_w, conv_b,
           w_qblk, w_kblk, w_vblk, w_gate, b_gate, ml_norm, ml_skip, w_out, norm2, w_router,
           w_e_gate, w_e_up, w_e_down, final_norm):
    B, T, D = x.shape
    assert w_mod.shape[0] == 1 and D == D_MODEL and ctx.shape[1] == ROW_TILE
    cap = CAP_FACTOR * T // N_EXPERTS

    cc = jnp.zeros((SUBLANES, D), F32).at[:B].set(c).at[B].set(c_ctx)
    mod = _modulation(cc, w_mod[0], b_mod[0].reshape(1, -1))
    mod3 = mod[:B + 1].reshape(B + 1, 1, 6 * D)

    wi = w_in[0]
    zpad = jnp.zeros((D, LANES - QK_ROPE), F32)
    w_in_p = jnp.concatenate([wi[:, :_C_KROPE + QK_ROPE], zpad, wi[:, _C_KROPE + QK_ROPE:]], axis=1).astype(BF16)
    wuq = w_uq[0].reshape(Q_LORA, MLA_HEADS, QK_NOPE + QK_ROPE)
    wuq_nope = wuq[:, :, :QK_NOPE].reshape(Q_LORA, MLA_HEADS * QK_NOPE).astype(BF16)
    wuq_pe = jnp.pad(wuq[:, :, QK_NOPE:], ((0, 0), (0, 0), (0, LANES - QK_ROPE))).reshape(
        Q_LORA, MLA_HEADS * LANES).astype(BF16)
    wukv = w_ukv[0].reshape(KV_LORA, MLA_HEADS, QK_NOPE + V_DIM)
    wukv_p = jnp.concatenate([wukv[:, :, :QK_NOPE].reshape(KV_LORA, -1),
                              wukv[:, :, QK_NOPE:].reshape(KV_LORA, -1)], axis=1).astype(BF16)
    cos_t, sin_t = _rope_tables(T)

    q, k, v, xm, z, gmla, gml = _input_projection(
        ctx, x, mod3, norm1, w_in_p, q_norm, wuq_nope, wuq_pe, kv_norm, wukv_p, cos_t, sin_t)

    conv_w8 = jnp.zeros((SUBLANES, ML_INNER), F32).at[:CONV_W].set(conv_w[0])
    wg3 = w_gate[0].reshape(3, ML_INNER, 4 * ML_HEADS).transpose(0, 2, 1).astype(BF16)
    mq, mk, mv, xc, gates = _mlstm_features(
        xm, conv_w8, conv_b, _blockdiag_dense(w_qblk[0].transpose(0, 2, 1)), _blockdiag_dense(w_kblk[0]),
        _blockdiag_dense(w_vblk[0].transpose(0, 2, 1)), wg3, b_gate[0].reshape(-1, 1))

    y_mla = _attention(q, k, v)
    hf, hb = _mlstm_scan(gates, mq, mk, mv)

    x1, h2, aff_t = _merge(hf, hb, z, xc, gmla, gml, y_mla, x, mod3, ml_norm, ml_skip,
                           w_out[0].astype(BF16), norm2, w_router[0].T)

    pos, cs = _select(aff_t, cap)
    base = jnp.concatenate([cs[:, :, ::MOE_TILE], jnp.full((B, N_EXPERTS, 1), cap, I32)], axis=2).reshape(-1)
    y = _experts(base, h2, pos.reshape(B * N_EXPERTS, T // MOE_TILE, MOE_TILE), w_e_gate[0].astype(BF16),
                 w_e_up[0].astype(BF16), w_e_down[0].astype(BF16), cap)
    return _combine(base, y, pos.transpose(0, 2, 1), aff_t.transpose(0, 2, 1), x1, mod3,
                    final_norm.reshape(1, -1))
```

```python
import functools

import numpy as np
import jax
import jax.numpy as jnp
from jax import lax
from jax.experimental import pallas as pl
from jax.experimental.pallas import tpu as pltpu

F32 = jnp.float32
BF16 = jnp.bfloat16
I32 = jnp.int32

D_MODEL = 1024
GRID_W = 64
MLA_HEADS = 8
QK_NOPE = 128
QK_ROPE = 64
V_DIM = 128
Q_LORA = 384
KV_LORA = 256
ROPE_BASE = 10000.0
ROPE_PAIRS = QK_ROPE // 4
ATTN_SCALE = (QK_NOPE + QK_ROPE) ** -0.5
Q_SCALE = ATTN_SCALE * 1.4426950408889634
ML_HEADS = 4
ML_INNER = 1024
ML_HEAD_DIM = ML_INNER // ML_HEADS
QKV_BLOCK = 4
CONV_W = 5
CHUNK = 128
N_EXPERTS = 16
EXPERT_FF = 1024
CAP_FACTOR = 2
EPS = 1e-6

LANES = 128
SUBLANES = 8
BF16_ROWS = 16
ROW_TILE = 256
HEAD_W = 256
ATTN_TQ = 256
ATTN_TK = 768
MOE_TILE = 256
MOE_WIN = MOE_TILE + BF16_ROWS
VMEM_LIMIT = 56 * 1024 * 1024

_C_QLAT = 0
_C_KVLAT = Q_LORA
_C_KROPE = Q_LORA + KV_LORA
_C_XM = _C_KROPE + LANES
_C_Z = _C_XM + ML_INNER
_C_GMLA = _C_Z + ML_INNER
_C_GML = _C_GMLA + D_MODEL
IN_PAD = _C_GML + D_MODEL

_NT = (((1,), (1,)), ((), ()))
_TN = (((0,), (0,)), ((), ()))


def _params(sem, vmem=VMEM_LIMIT):
    return pltpu.CompilerParams(dimension_semantics=sem, vmem_limit_bytes=vmem)


def _rms(x, g):
    return x * lax.rsqrt(jnp.mean(x * x, axis=-1, keepdims=True) + EPS) * g


def _sigmoid(x):
    return jax.nn.sigmoid(x)


def _mod_kernel(c_ref, w_ref, b_ref, o_ref):
    c = c_ref[...]
    s = c * _sigmoid(c)
    o_ref[...] = jnp.dot(s, w_ref[...], preferred_element_type=F32,
                         precision=lax.Precision.HIGHEST) + b_ref[...]


def _modulation(cc, w_mod, b_mod):
    n = w_mod.shape[1]
    tn = 768
    return pl.pallas_call(
        _mod_kernel,
        grid=(n // tn,),
        in_specs=[pl.BlockSpec((SUBLANES, D_MODEL), lambda j: (0, 0)),
                  pl.BlockSpec((D_MODEL, tn), lambda j: (0, j)),
                  pl.BlockSpec((1, tn), lambda j: (0, j))],
        out_specs=pl.BlockSpec((SUBLANES, tn), lambda j: (0, j)),
        out_shape=jax.ShapeDtypeStruct((SUBLANES, n), F32),
        compiler_params=_params(("arbitrary",)),
        name="modulation",
    )(cc, w_mod, b_mod)


def _rope(v, cos, sin):
    lane = lax.broadcasted_iota(I32, v.shape, 1)
    partner = jnp.where(lane % 32 < 16, pltpu.roll(v, LANES - 16, 1), pltpu.roll(v, 16, 1))
    return v * cos + partner * sin


def _inproj_kernel(ctx_ref, x_ref, sh_ref, sc_ref, n1_ref, win_ref, qn_ref, wuqn_ref, wuqp_ref,
                   kvn_ref, wukv_ref, cos_ref, sin_ref,
                   q_ref, k_ref, v_ref, xm_ref, z_ref, gmla_ref, gml_ref):
    i = pl.program_id(1)
    xin = jnp.where(i == 0, ctx_ref[0], x_ref[0])
    h = _rms(xin, n1_ref[...]) * (1.0 + sc_ref[0]) + sh_ref[0]
    big = jnp.dot(h.astype(BF16), win_ref[...], preferred_element_type=F32)
    xm_ref[0] = big[:, _C_XM:_C_Z]
    z_ref[0] = big[:, _C_Z:_C_GMLA].astype(BF16)
    gmla_ref[0] = big[:, _C_GMLA:_C_GML].astype(BF16)
    gml_ref[0] = big[:, _C_GML:IN_PAD].astype(BF16)

    cos = cos_ref[...]
    sin = sin_ref[...]
    qn = _rms(big[:, _C_QLAT:_C_KVLAT], qn_ref[...]).astype(BF16)
    q_nope = jnp.dot(qn, wuqn_ref[...], preferred_element_type=F32)
    q_pe = jnp.dot(qn, wuqp_ref[...], preferred_element_type=F32)
    kvn = _rms(big[:, _C_KVLAT:_C_KROPE], kvn_ref[...]).astype(BF16)
    kv = jnp.dot(kvn, wukv_ref[...], preferred_element_type=F32)
    k_pe = _rope(big[:, _C_KROPE:_C_XM], cos, sin).astype(BF16)
    ones_col = jnp.where(lax.broadcasted_iota(I32, (big.shape[0], HEAD_W - V_DIM), 1) == 0, 1.0, 0.0).astype(BF16)
    for hh in range(MLA_HEADS):
        lo = hh * HEAD_W
        nope = slice(hh * QK_NOPE, (hh + 1) * QK_NOPE)
        q_ref[0, :, lo:lo + QK_NOPE] = (q_nope[:, nope] * Q_SCALE).astype(BF16)
        q_ref[0, :, lo + QK_NOPE:lo + HEAD_W] = (
            _rope(q_pe[:, hh * LANES:(hh + 1) * LANES], cos, sin) * Q_SCALE).astype(BF16)
        k_ref[0, :, lo:lo + QK_NOPE] = kv[:, nope].astype(BF16)
        k_ref[0, :, lo + QK_NOPE:lo + HEAD_W] = k_pe
        v_ref[0, :, lo:lo + V_DIM] = kv[:, MLA_HEADS * QK_NOPE + hh * V_DIM:MLA_HEADS * QK_NOPE + (hh + 1) * V_DIM].astype(BF16)
        v_ref[0, :, lo + V_DIM:lo + HEAD_W] = ones_col


def _input_projection(ctx, x, mod3, norm1, w_in_p, q_norm, wuq_nope, wuq_pe, kv_norm, wukv_p, cos_t, sin_t):
    B, T, D = x.shape
    n_lat = T // ROW_TILE
    nt = n_lat + 1
    tt = T + ROW_TILE
    const2 = lambda b, i: (0, 0)
    lat = lambda b, i: (b, jnp.maximum(i - 1, 0), 0)
    allrows = lambda b, i: (b, i, 0)
    modrow = lambda col: (lambda b, i: (jnp.where(i == 0, B, b), 0, col))
    return pl.pallas_call(
        _inproj_kernel,
        grid=(B, nt),
        in_specs=[pl.BlockSpec((1, ROW_TILE, D), lambda b, i: (b, 0, 0)),
                  pl.BlockSpec((1, ROW_TILE, D), lat),
                  pl.BlockSpec((1, 1, D), modrow(0)),
                  pl.BlockSpec((1, 1, D), modrow(1)),
                  pl.BlockSpec((1, D), const2),
                  pl.BlockSpec(w_in_p.shape, const2),
                  pl.BlockSpec((1, Q_LORA), const2),
                  pl.BlockSpec(wuq_nope.shape, const2),
                  pl.BlockSpec(wuq_pe.shape, const2),
                  pl.BlockSpec((1, KV_LORA), const2),
                  pl.BlockSpec(wukv_p.shape, const2),
                  pl.BlockSpec((ROW_TILE, LANES), lambda b, i: (i, 0)),
                  pl.BlockSpec((ROW_TILE, LANES), lambda b, i: (i, 0))],
        out_specs=[pl.BlockSpec((1, ROW_TILE, MLA_HEADS * HEAD_W), lat),
                   pl.BlockSpec((1, ROW_TILE, MLA_HEADS * HEAD_W), allrows),
                   pl.BlockSpec((1, ROW_TILE, MLA_HEADS * HEAD_W), allrows),
                   pl.BlockSpec((1, ROW_TILE, ML_INNER), allrows),
                   pl.BlockSpec((1, ROW_TILE, ML_INNER), lat),
                   pl.BlockSpec((1, ROW_TILE, D), lat),
                   pl.BlockSpec((1, ROW_TILE, D), lat)],
        out_shape=[jax.ShapeDtypeStruct((B, T, MLA_HEADS * HEAD_W), BF16),
                   jax.ShapeDtypeStruct((B, tt, MLA_HEADS * HEAD_W), BF16),
                   jax.ShapeDtypeStruct((B, tt, MLA_HEADS * HEAD_W), BF16),
                   jax.ShapeDtypeStruct((B, tt, ML_INNER), F32),
                   jax.ShapeDtypeStruct((B, T, ML_INNER), BF16),
                   jax.ShapeDtypeStruct((B, T, D), BF16),
                   jax.ShapeDtypeStruct((B, T, D), BF16)],
        compiler_params=_params(("arbitrary", "arbitrary")),
        name="input_projection",
    )(ctx, x, mod3, mod3, norm1, w_in_p, q_norm, wuq_nope, wuq_pe, kv_norm, wukv_p, cos_t, sin_t)


def _feat_kernel(prev_ref, cur_ref, next_ref, cw_ref, cb_ref, wq_ref, wk_ref, wv_ref, wg_ref, bg_ref,
                 q_ref, k_ref, v_ref, xc_ref, g_ref, *, n_tiles):
    i = pl.program_id(1)
    cur = cur_ref[0]
    prev = jnp.where(i <= 1, 0.0, prev_ref[0])
    nxt = jnp.where((i == 0) | (i == n_tiles - 1), 0.0, next_ref[0])
    xx = jnp.concatenate([prev, cur, nxt], axis=0)
    rows = cur.shape[0]
    acc = jnp.broadcast_to(cb_ref[...], cur.shape)
    for w in range(CONV_W):
        lo = SUBLANES - CONV_W // 2 + w
        acc = acc + xx[lo:lo + rows] * cw_ref[w:w + 1, :]
    xc = acc * _sigmoid(acc)
    xc_ref[0] = xc.astype(BF16)
    xcb = xc.astype(BF16)
    q_t = lax.dot_general(wq_ref[...], xcb, _NT, preferred_element_type=F32)
    k = jnp.dot(xcb, wk_ref[...], preferred_element_type=F32)
    v_t = lax.dot_general(wv_ref[...], cur.astype(BF16), _NT, preferred_element_type=F32)
    qb, kb, vb = q_t.astype(BF16), k.astype(BF16), v_t.astype(BF16)
    q_ref[0] = qb
    k_ref[0] = (k * (ML_HEAD_DIM ** -0.5)).astype(BF16)
    v_ref[0] = vb
    g = (jnp.dot(wg_ref[0], qb, preferred_element_type=F32)
         + lax.dot_general(wg_ref[1], kb, _NT, preferred_element_type=F32)
         + jnp.dot(wg_ref[2], vb, preferred_element_type=F32)) + bg_ref[...]
    row = lax.broadcasted_iota(I32, g.shape, 0)
    is_forget = (row % (2 * ML_HEADS)) >= ML_HEADS
    g_ref[0] = jnp.where(is_forget, jax.nn.log_sigmoid(g), g)


def _mlstm_features(xm, conv_w8, conv_b, wq_bd_t, wk_bd, wv_bd_t, wg3, bg_col):
    B, tt, C = xm.shape
    nt = tt // ROW_TILE
    per = ROW_TILE // SUBLANES
    last8 = tt // SUBLANES - 1
    const2 = lambda b, i: (0, 0)
    rows = lambda b, i: (b, i, 0)
    cols = lambda b, i: (b, 0, i)
    ng = 4 * ML_HEADS
    return pl.pallas_call(
        functools.partial(_feat_kernel, n_tiles=nt),
        grid=(B, nt),
        in_specs=[pl.BlockSpec((1, SUBLANES, C), lambda b, i: (b, jnp.maximum(i * per - 1, 0), 0)),
                  pl.BlockSpec((1, ROW_TILE, C), rows),
                  pl.BlockSpec((1, SUBLANES, C), lambda b, i: (b, jnp.minimum((i + 1) * per, last8), 0)),
                  pl.BlockSpec((SUBLANES, C), const2),
                  pl.BlockSpec((1, C), const2),
                  pl.BlockSpec((C, C), const2),
                  pl.BlockSpec((C, C), const2),
                  pl.BlockSpec((C, C), const2),
                  pl.BlockSpec((3, ng, C), lambda b, i: (0, 0, 0)),
                  pl.BlockSpec((ng, 1), const2)],
        out_specs=[pl.BlockSpec((1, C, ROW_TILE), cols),
                   pl.BlockSpec((1, ROW_TILE, C), rows),
                   pl.BlockSpec((1, C, ROW_TILE), cols),
                   pl.BlockSpec((1, ROW_TILE, C), rows),
                   pl.BlockSpec((1, ng, ROW_TILE), cols)],
        out_shape=[jax.ShapeDtypeStruct((B, C, tt), BF16),
                   jax.ShapeDtypeStruct((B, tt, C), BF16),
                   jax.ShapeDtypeStruct((B, C, tt), BF16),
                   jax.ShapeDtypeStruct((B, tt, C), BF16),
                   jax.ShapeDtypeStruct((B, ng, tt), F32)],
        compiler_params=_params(("arbitrary", "arbitrary")),
        name="mlstm_features",
    )(xm, xm, xm, conv_w8, conv_b, wq_bd_t, wk_bd, wv_bd_t, wg3, bg_col)


def _attn_kernel(q_ref, k_ref, v_ref, o_ref, s_ref, p_ref, m_ref, *, n_chunks):
    j = pl.program_id(0)

    @pl.when(j == 0)
    def _():
        s_ref[...] = jnp.zeros_like(s_ref)
        p_ref[...] = jnp.ones_like(p_ref)
        m_ref[...] = jnp.zeros_like(m_ref)

    q = q_ref[0]
    m_old = m_ref[...]
    m = jnp.full((ATTN_TQ, 1), -jnp.inf, F32)
    acc = jnp.zeros((ATTN_TQ, HEAD_W), F32)
    for c in range(n_chunks):
        keys = slice(c * ATTN_TK, (c + 1) * ATTN_TK)
        acc = acc + jnp.dot(p_ref[c], v_ref[0, keys, :], preferred_element_type=F32)
        p_ref[c] = jnp.exp2(s_ref[c] - m_old).astype(BF16)
        s = lax.dot_general(q, k_ref[0, keys, :], _NT, preferred_element_type=F32)
        s_ref[c] = s
        m = jnp.maximum(m, jnp.max(s, axis=-1, keepdims=True))
    m_ref[...] = m
    o_ref[0] = (acc[:, :V_DIM] / acc[:, V_DIM:V_DIM + 1]).astype(BF16)


def _attention(q, k, v):
    B, T, _ = q.shape
    tt = k.shape[1]
    n_chunks = tt // ATTN_TK
    n_tiles = T // ATTN_TQ
    n_total = B * MLA_HEADS * n_tiles
    lag = 2
    assert n_chunks * ATTN_TK == tt and n_tiles * ATTN_TQ == T

    def tile(t):
        t = jnp.clip(t, 0, n_total - 1)
        bh = t // n_tiles
        return bh // MLA_HEADS, t % n_tiles, bh % MLA_HEADS

    def q_idx(j):
        b, i, h = tile(j)
        return b, i, h

    def k_idx(j):
        b, _, h = tile(j)
        return b, 0, h

    def v_idx(j):
        b, _, h = tile(j - lag)
        return b, 0, h

    def o_idx(j):
        b, i, h = tile(j - lag)
        return b, i, h

    return pl.pallas_call(
        functools.partial(_attn_kernel, n_chunks=n_chunks),
        grid=(n_total + lag,),
        in_specs=[pl.BlockSpec((1, ATTN_TQ, HEAD_W), q_idx),
                  pl.BlockSpec((1, tt, HEAD_W), k_idx),
                  pl.BlockSpec((1, tt, HEAD_W), v_idx)],
        out_specs=pl.BlockSpec((1, ATTN_TQ, V_DIM), o_idx),
        out_shape=jax.ShapeDtypeStruct((B, T, MLA_HEADS * V_DIM), BF16),
        scratch_shapes=[pltpu.VMEM((n_chunks, ATTN_TQ, ATTN_TK), F32),
                        pltpu.VMEM((n_chunks, ATTN_TQ, ATTN_TK), BF16),
                        pltpu.VMEM((ATTN_TQ, 1), F32)],
        compiler_params=_params(("arbitrary",)),
        name="attention",
    )(q, k, v)


def _lane_cumsum(x, reverse):
    lane = lax.broadcasted_iota(I32, x.shape, 1)
    n = x.shape[1]
    s = 1
    while s < n:
        if reverse:
            x = x + jnp.where(lane < n - s, pltpu.roll(x, n - s, 1), 0.0)
        else:
            x = x + jnp.where(lane >= s, pltpu.roll(x, s, 1), 0.0)
        s *= 2
    return x


def _scan_kernel(gf_ref, gb_ref, qf_ref, kf_ref, vf_ref, qb_ref, kb_ref, vb_ref,
                 hf_ref, hb_ref, cn_ref, m_ref):
    j = pl.program_id(1)
    L = CHUNK
    dh = ML_HEAD_DIM

    @pl.when(j == 0)
    def _():
        cn_ref[...] = jnp.zeros_like(cn_ref)
        m_ref[...] = jnp.zeros_like(m_ref)

    spos = lax.broadcasted_iota(I32, (L, L), 0)
    tpos = lax.broadcasted_iota(I32, (L, L), 1)
    first_row = lax.broadcasted_iota(I32, (BF16_ROWS, L), 0) == 0
    dirs = ((gf_ref, qf_ref, kf_ref, vf_ref, hf_ref), (gb_ref, qb_ref, kb_ref, vb_ref, hb_ref))
    for d, (g_ref, q_ref, k_ref, v_ref, h_ref) in enumerate(dirs):
        reverse = d == 1
        mask = (spos >= tpos) if reverse else (spos <= tpos)
        g = g_ref[0]
        ig4 = g[d * 2 * ML_HEADS:d * 2 * ML_HEADS + ML_HEADS]
        lf4 = g[d * 2 * ML_HEADS + ML_HEADS:(d + 1) * 2 * ML_HEADS]
        b4 = _lane_cumsum(lf4, reverse)
        a4 = ig4 - b4
        a_cols = jnp.concatenate([a4, jnp.zeros((L - ML_HEADS, L), F32)], axis=0).T
        for hh in range(ML_HEADS):
            ci = d * ML_HEADS + hh
            b_row, a_row, a_col = b4[hh:hh + 1], a4[hh:hh + 1], a_cols[:, hh:hh + 1]
            b_last = b_row[:, 0:1] if reverse else b_row[:, L - 1:L]
            m = m_ref[ci, 0:1, 0:1]
            sl = slice(hh * dh, (hh + 1) * dh)
            q_t, k, v_t = q_ref[0, sl, :], k_ref[0, :, sl], v_ref[0, sl, :]
            cn = cn_ref[ci]

            dmat = jnp.where(mask, b_row + a_col, -jnp.inf)
            inter = b_row + m
            m_t = jnp.maximum(inter, jnp.max(dmat, axis=0, keepdims=True))
            w_inter = jnp.exp(inter - m_t)
            s = jnp.dot(k, q_t, preferred_element_type=F32) * jnp.exp(dmat - m_t)
            cq = jnp.dot(cn.astype(BF16), q_t, preferred_element_type=F32)
            num = jnp.dot(v_t, s.astype(BF16), preferred_element_type=F32) + w_inter * cq[:dh]
            den = jnp.sum(s, axis=0, keepdims=True) + w_inter * cq[dh:dh + 1]
            h_ref[0, sl, :] = (num / jnp.maximum(jnp.abs(den), jnp.exp(-m_t))).astype(BF16)

            dec = b_last + a_row
            m_new = jnp.maximum(b_last + m, jnp.max(dec, axis=-1, keepdims=True))
            wk = jnp.exp(dec - m_new)
            keep = jnp.exp(b_last + m - m_new)
            vw = jnp.concatenate([(v_t.astype(F32) * wk).astype(BF16),
                                  jnp.where(first_row, wk, 0.0).astype(BF16)], axis=0)
            cn_ref[ci] = keep * cn + jnp.dot(vw, k, preferred_element_type=F32)
            m_ref[ci] = jnp.broadcast_to(m_new, m_ref.shape[1:])


def _mlstm_scan(gates, q_t, k, v_t):
    B, tt, C = k.shape
    nch = tt // CHUNK
    ng = gates.shape[1]
    nc_ctx = ROW_TILE // CHUNK
    bidx = lambda j: jnp.where(j < nc_ctx, nc_ctx - 1 - j, nch - 1 + nc_ctx - j)
    rows_f = pl.BlockSpec((1, CHUNK, C), lambda b, j: (b, j, 0))
    rows_b = pl.BlockSpec((1, CHUNK, C), lambda b, j: (b, bidx(j), 0))
    cols_f = pl.BlockSpec((1, C, CHUNK), lambda b, j: (b, 0, j))
    cols_b = pl.BlockSpec((1, C, CHUNK), lambda b, j: (b, 0, bidx(j)))
    nchain = 2 * ML_HEADS
    return pl.pallas_call(
        _scan_kernel,
        grid=(B, nch),
        in_specs=[pl.BlockSpec((1, ng, CHUNK), lambda b, j: (b, 0, j)),
                  pl.BlockSpec((1, ng, CHUNK), lambda b, j: (b, 0, bidx(j))),
                  cols_f, rows_f, cols_f, cols_b, rows_b, cols_b],
        out_specs=[cols_f, cols_b],
        out_shape=[jax.ShapeDtypeStruct((B, C, tt), BF16), jax.ShapeDtypeStruct((B, C, tt), BF16)],
        scratch_shapes=[pltpu.VMEM((nchain, ML_HEAD_DIM + BF16_ROWS, ML_HEAD_DIM), F32),
                        pltpu.VMEM((nchain, SUBLANES, LANES), F32)],
        compiler_params=_params(("arbitrary", "arbitrary")),
        name="mlstm_scan",
    )(gates, gates, q_t, k, v_t, q_t, k, v_t)


def _merge_kernel(hf_ref, hb_ref, z_ref, xc_ref, gmla_ref, gml_ref, ymla_ref, x_ref,
                  g1_ref, sh2_ref, sc2_ref, mln_ref, mls_ref, wout_ref, n2_ref, wr_ref,
                  x1_ref, h2_ref, aff_ref):
    h_t = hf_ref[0].astype(F32) + hb_ref[0].astype(F32)
    parts = []
    for hh in range(ML_HEADS):
        seg = h_t[hh * ML_HEAD_DIM:(hh + 1) * ML_HEAD_DIM]
        parts.append((seg * lax.rsqrt(jnp.mean(seg * seg, axis=0, keepdims=True) + EPS)).T)
    hn = jnp.concatenate(parts, axis=-1) * mln_ref[...]
    y_ml = _sigmoid(z_ref[0].astype(F32)) * (hn + mls_ref[...] * xc_ref[0].astype(F32))
    merged = (_sigmoid(gmla_ref[0].astype(F32)) * ymla_ref[0].astype(F32)
              + _sigmoid(gml_ref[0].astype(F32)) * y_ml)
    out = jnp.dot(merged.astype(BF16), wout_ref[...], preferred_element_type=F32)
    x1 = x_ref[0] + g1_ref[0] * out
    x1_ref[0] = x1
    h2 = _rms(x1, n2_ref[...]) * (1.0 + sc2_ref[0]) + sh2_ref[0]
    h2_ref[0] = h2.astype(BF16)
    logits = lax.dot_general(wr_ref[...], h2, _NT, preferred_element_type=F32,
                             precision=lax.Precision.HIGHEST)
    e = jnp.exp(logits - jnp.max(logits, axis=0, keepdims=True))
    aff_ref[0] = e / jnp.sum(e, axis=0, keepdims=True)


def _merge(hf, hb, z, xc, gmla, gml, ymla, x, mod3, ml_norm, ml_skip, w_out, norm2, w_router_t):
    B, T, D = x.shape
    nt = T // ROW_TILE
    const2 = lambda b, i: (0, 0)
    lat = lambda b, i: (b, i, 0)
    shifted = lambda b, i: (b, i + 1, 0)
    modcol = lambda col: (lambda b, i: (b, 0, col))
    tile = lambda idx: pl.BlockSpec((1, ROW_TILE, D), idx)
    h_tile = pl.BlockSpec((1, ML_INNER, ROW_TILE), lambda b, i: (b, 0, i + 1))
    return pl.pallas_call(
        _merge_kernel,
        grid=(B, nt),
        in_specs=[h_tile, h_tile, tile(lat), tile(shifted), tile(lat), tile(lat), tile(lat),
                  tile(lat),
                  pl.BlockSpec((1, 1, D), modcol(2)), pl.BlockSpec((1, 1, D), modcol(3)),
                  pl.BlockSpec((1, 1, D), modcol(4)),
                  pl.BlockSpec((1, D), const2), pl.BlockSpec((1, D), const2),
                  pl.BlockSpec((D, D), const2), pl.BlockSpec((1, D), const2),
                  pl.BlockSpec((N_EXPERTS, D), const2)],
        out_specs=[tile(lat), tile(lat), pl.BlockSpec((1, N_EXPERTS, ROW_TILE), lambda b, i: (b, 0, i))],
        out_shape=[jax.ShapeDtypeStruct((B, T, D), F32),
                   jax.ShapeDtypeStruct((B, T, D), BF16),
                   jax.ShapeDtypeStruct((B, N_EXPERTS, T), F32)],
        compiler_params=_params(("arbitrary", "arbitrary")),
        name="merge_router",
    )(hf, hb, z, xc, gmla, gml, ymla, x, mod3, mod3, mod3, ml_norm, ml_skip, w_out, norm2, w_router_t)


def _chunked_cumsum(mask_f, tri):
    n_e, t = mask_f.shape
    off = jnp.zeros((n_e, 1), F32)
    outs = []
    for c in range(t // LANES):
        x = mask_f[:, c * LANES:(c + 1) * LANES]
        inc = jnp.dot(x.astype(BF16), tri, preferred_element_type=F32)
        outs.append(inc - x + off)
        off = off + inc[:, LANES - 1:LANES]
    return jnp.concatenate(outs, axis=1)


def _select_kernel(aff_ref, pos_ref, cs_ref, *, cap):
    aff = aff_ref[0]
    n_e = aff.shape[0]

    def count_ge(t):
        return jnp.sum(jnp.where(aff >= t, 1.0, 0.0), axis=1, keepdims=True)

    def body(carry):
        lo, hi, _ = carry
        mid = 0.5 * (lo + hi)
        ok = count_ge(mid) >= cap
        lo, hi = jnp.where(ok, mid, lo), jnp.where(ok, hi, mid)
        mid = 0.5 * (lo + hi)
        return lo, hi, jnp.max(jnp.where((mid > lo) & (mid < hi), 1.0, 0.0))

    lo, hi, _ = lax.while_loop(lambda carry: carry[2] > 0.5, body,
                               (jnp.zeros((n_e, 1), F32), jnp.full((n_e, 1), 2.0, F32), jnp.float32(1.0)))
    gt = jnp.where(aff >= hi, 1.0, 0.0)
    eq = jnp.where(aff >= lo, 1.0, 0.0) - gt
    need = cap - jnp.sum(gt, axis=1, keepdims=True)
    tri = jnp.where(lax.broadcasted_iota(I32, (LANES, LANES), 0) <= lax.broadcasted_iota(I32, (LANES, LANES), 1),
                    1.0, 0.0).astype(BF16)
    eq_rank = _chunked_cumsum(eq, tri)
    sel = gt + eq * jnp.where(eq_rank < need, 1.0, 0.0)
    cs = _chunked_cumsum(sel, tri)
    cs_ref[0] = cs.astype(I32)
    pos_ref[0] = jnp.where(sel > 0.5, cs, -1.0).astype(I32)


def _select(aff_t, cap):
    B, n_e, T = aff_t.shape
    blk = pl.BlockSpec((1, n_e, T), lambda b: (b, 0, 0))
    return pl.pallas_call(
        functools.partial(_select_kernel, cap=cap),
        grid=(B,),
        in_specs=[blk],
        out_specs=[blk, blk],
        out_shape=[jax.ShapeDtypeStruct((B, n_e, T), I32), jax.ShapeDtypeStruct((B, n_e, T), I32)],
        compiler_params=_params(("arbitrary",)),
        name="expert_select",
    )(aff_t)


def _window_start(base_ref, flat):
    start = base_ref[flat]
    return pl.multiple_of((start // BF16_ROWS) * BF16_ROWS, BF16_ROWS)


def _expert_kernel(base_ref, x_ref, pos_ref, wg_ref, wu_ref, wd_ref, y_ref, xs_ref, *, n_tiles, cap):
    b, e = pl.program_id(0), pl.program_id(1)
    head = BF16_ROWS
    xs_ref[0:head, :] = jnp.zeros((head, xs_ref.shape[1]), BF16)
    row_id = lax.broadcasted_iota(I32, (MOE_WIN, MOE_TILE), 0)

    def gather(k, carry):
        s16 = _window_start(base_ref, (b * N_EXPERTS + e) * (n_tiles + 1) + k)
        rel = pos_ref[0, pl.ds(k, 1), :] - s16
        onehot = jnp.where(row_id == rel, 1.0, 0.0).astype(BF16)
        tok = pl.ds(pl.multiple_of(k * MOE_TILE, MOE_TILE), MOE_TILE)
        rows = jnp.dot(onehot, x_ref[0, tok, :], preferred_element_type=F32)
        first = xs_ref[pl.ds(s16, head), :].astype(F32) + rows[:head]
        xs_ref[pl.ds(s16 + head, MOE_WIN - head), :] = rows[head:].astype(BF16)
        xs_ref[pl.ds(s16, head), :] = first.astype(BF16)
        return carry

    lax.fori_loop(0, n_tiles, gather, 0, unroll=4)

    for r in range(cap // ROW_TILE):
        rows = slice(r * ROW_TILE, (r + 1) * ROW_TILE)
        xs = xs_ref[rows, :]
        a = jnp.dot(xs, wg_ref[0], preferred_element_type=F32)
        u = jnp.dot(xs, wu_ref[0], preferred_element_type=F32)
        hm = (a * _sigmoid(a) * u).astype(BF16)
        y_ref[0, 0, rows, :] = jnp.dot(hm, wd_ref[0], preferred_element_type=F32).astype(BF16)
    y_ref[0, 0, cap:, :] = jnp.zeros((y_ref.shape[2] - cap, y_ref.shape[3]), BF16)


def _experts(base, h2, pos3, wg, wu, wd, cap):
    B, T, D = h2.shape
    nk = T // MOE_TILE
    yr = cap + MOE_WIN
    ff = wg.shape[2]
    wspec = lambda shape: pl.BlockSpec((1,) + shape, lambda b, e, base: (e, 0, 0))
    grid_spec = pltpu.PrefetchScalarGridSpec(
        num_scalar_prefetch=1,
        grid=(B, N_EXPERTS),
        in_specs=[pl.BlockSpec((1, T, D), lambda b, e, base: (b, 0, 0), pipeline_mode=pl.Buffered(1)),
                  pl.BlockSpec((1, nk, MOE_TILE), lambda b, e, base: (b * N_EXPERTS + e, 0, 0)),
                  wspec((D, ff)), wspec((D, ff)), wspec((ff, D))],
        out_specs=pl.BlockSpec((1, 1, yr, D), lambda b, e, base: (b, e, 0, 0)),
        scratch_shapes=[pltpu.VMEM((yr, D), BF16)])
    return pl.pallas_call(
        functools.partial(_expert_kernel, n_tiles=nk, cap=cap),
        grid_spec=grid_spec,
        out_shape=jax.ShapeDtypeStruct((B, N_EXPERTS, yr, D), BF16),
        compiler_params=_params(("arbitrary", "arbitrary")),
        name="expert_ffn",
    )(base, h2, pos3, wg, wu, wd)


def _combine_kernel(base_ref, y_ref, pos_ref, g_ref, x1_ref, g2_ref, fn_ref, o_ref, *, n_tiles, tiles_per_blk):
    b, tb, e = pl.program_id(0), pl.program_id(1), pl.program_id(2)
    flat0 = (b * N_EXPERTS + e) * (n_tiles + 1) + tb * tiles_per_blk
    pick = lax.broadcasted_iota(I32, (MOE_TILE, N_EXPERTS), 1) == e
    col_id = lax.broadcasted_iota(I32, (MOE_TILE, MOE_TILE), 1).astype(F32)

    @pl.when(e == 0)
    def _():
        o_ref[...] = jnp.zeros_like(o_ref)

    def columns(k):
        rows = slice(k * MOE_TILE, (k + 1) * MOE_TILE)
        pcol = jnp.sum(jnp.where(pick, pos_ref[0, rows, :].astype(F32), 0.0), axis=1, keepdims=True)
        gcol = jnp.sum(jnp.where(pick, g_ref[0, rows, :], 0.0), axis=1, keepdims=True)
        return rows, pcol, gcol

    for k in range(tiles_per_blk):
        s16 = _window_start(base_ref, flat0 + k)
        rows, pcol, gcol = columns(k)
        onehot = jnp.where(col_id == pcol - s16.astype(F32), 1.0, 0.0).astype(BF16)
        o_ref[0, rows, :] += jnp.dot(onehot, y_ref[0, 0, pl.ds(s16, MOE_TILE), :],
                                     preferred_element_type=F32) * gcol

    for k in range(tiles_per_blk):
        s16 = _window_start(base_ref, flat0 + k)

        @pl.when(base_ref[flat0 + k + 1] > s16 + MOE_TILE)
        def _():
            rows, pcol, gcol = columns(k)
            tail = lax.broadcasted_iota(I32, (MOE_TILE, BF16_ROWS), 1).astype(F32) + float(MOE_TILE)
            onehot = jnp.where(tail == pcol - s16.astype(F32), 1.0, 0.0).astype(BF16)
            o_ref[0, rows, :] += jnp.dot(onehot, y_ref[0, 0, pl.ds(s16 + MOE_TILE, BF16_ROWS), :],
                                         preferred_element_type=F32) * gcol

    @pl.when(e == N_EXPERTS - 1)
    def _():
        for k in range(tiles_per_blk):
            rows = slice(k * MOE_TILE, (k + 1) * MOE_TILE)
            x2 = x1_ref[0, rows, :] + g2_ref[0] * o_ref[0, rows, :]
            o_ref[0, rows, :] = _rms(x2, fn_ref[...])


def _combine(base, y, pos_t, g_t, x1, mod3, final_norm):
    B, T, D = x1.shape
    nk = T // MOE_TILE
    per = 8
    blk = per * MOE_TILE
    yr = y.shape[2]
    tok = lambda b, tb, e, base: (b, tb, 0)
    grid_spec = pltpu.PrefetchScalarGridSpec(
        num_scalar_prefetch=1,
        grid=(B, nk // per, N_EXPERTS),
        in_specs=[pl.BlockSpec((1, 1, yr, D), lambda b, tb, e, base: (b, e, 0, 0)),
                  pl.BlockSpec((1, blk, N_EXPERTS), tok),
                  pl.BlockSpec((1, blk, N_EXPERTS), tok),
                  pl.BlockSpec((1, blk, D), tok),
                  pl.BlockSpec((1, 1, D), lambda b, tb, e, base: (b, 0, 5)),
                  pl.BlockSpec((1, D), lambda b, tb, e, base: (0, 0))],
        out_specs=pl.BlockSpec((1, blk, D), tok))
    return pl.pallas_call(
        functools.partial(_combine_kernel, n_tiles=nk, tiles_per_blk=per),
        grid_spec=grid_spec,
        out_shape=jax.ShapeDtypeStruct((B, T, D), F32),
        compiler_params=_params(("arbitrary",) * 3),
        name="moe_combine",
    )(base, y, pos_t, g_t, x1, mod3, final_norm)


def _rope_tables(T):
    rows = T // GRID_W
    row = np.repeat(np.arange(rows, dtype=np.float64), GRID_W)
    col = np.tile(np.arange(GRID_W, dtype=np.float64), rows)
    inv = ROPE_BASE ** (-np.arange(ROPE_PAIRS, dtype=np.float64) / ROPE_PAIRS)
    ar, ac = row[:, None] * inv, col[:, None] * inv
    ones = np.ones((T, LANES - QK_ROPE))
    cos = np.concatenate([np.cos(ar), np.cos(ar), np.cos(ac), np.cos(ac), ones], axis=1)
    sin = np.concatenate([-np.sin(ar), np.sin(ar), -np.sin(ac), np.sin(ac), 0.0 * ones], axis=1)
    cos = np.concatenate([np.ones((ROW_TILE, LANES)), cos], axis=0)
    sin = np.concatenate([np.zeros((ROW_TILE, LANES)), sin], axis=0)
    return jnp.asarray(cos, F32), jnp.asarray(sin, F32)


def _blockdiag_dense(w):
    n, bs, _ = w.shape
    rows = jnp.broadcast_to(w.transpose(1, 0, 2).reshape(1, bs, n * bs), (n, bs, n * bs)).reshape(n * bs, n * bs)
    r = jnp.arange(n * bs) // bs
    return jnp.where(r[:, None] == r[None, :], rows, 0.0).astype(BF16)


def kernel(x, c, ctx, c_ctx, w_mod, b_mod, norm1, w_in, q_norm, w_uq, kv_norm, w_ukv, conv_w, conv_b,
           w_qblk, w_kblk, w_vblk, w_gate, b_gate, ml_norm, ml_skip, w_out, norm2, w_router,
           w_e_gate, w_e_up, w_e_down, final_norm):
    B, T, D = x.shape
    assert w_mod.shape[0] == 1 and D == D_MODEL and ctx.shape[1] == ROW_TILE
    cap = CAP_FACTOR * T // N_EXPERTS

    cc = jnp.zeros((SUBLANES, D), F32).at[:B].set(c).at[B].set(c_ctx)
    mod = _modulation(cc, w_mod[0], b_mod[0].reshape(1, -1))
    mod3 = mod[:B + 1].reshape(B + 1, 1, 6 * D)

    wi = w_in[0]
    zpad = jnp.zeros((D, LANES - QK_ROPE), F32)
    w_in_p = jnp.concatenate([wi[:, :_C_KROPE + QK_ROPE], zpad, wi[:, _C_KROPE + QK_ROPE:]], axis=1).astype(BF16)
    wuq = w_uq[0].reshape(Q_LORA, MLA_HEADS, QK_NOPE + QK_ROPE)
    wuq_nope = wuq[:, :, :QK_NOPE].reshape(Q_LORA, MLA_HEADS * QK_NOPE).astype(BF16)
    wuq_pe = jnp.pad(wuq[:, :, QK_NOPE:], ((0, 0), (0, 0), (0, LANES - QK_ROPE))).reshape(
        Q_LORA, MLA_HEADS * LANES).astype(BF16)
    wukv = w_ukv[0].reshape(KV_LORA, MLA_HEADS, QK_NOPE + V_DIM)
    wukv_p = jnp.concatenate([wukv[:, :, :QK_NOPE].reshape(KV_LORA, -1),
                              wukv[:, :, QK_NOPE:].reshape(KV_LORA, -1)], axis=1).astype(BF16)
    cos_t, sin_t = _rope_tables(T)

    q, k, v, xm, z, gmla, gml = _input_projection(
        ctx, x, mod3, norm1, w_in_p, q_norm, wuq_nope, wuq_pe, kv_norm, wukv_p, cos_t, sin_t)

    conv_w8 = jnp.zeros((SUBLANES, ML_INNER), F32).at[:CONV_W].set(conv_w[0])
    wg3 = w_gate[0].reshape(3, ML_INNER, 4 * ML_HEADS).transpose(0, 2, 1).astype(BF16)
    mq, mk, mv, xc, gates = _mlstm_features(
        xm, conv_w8, conv_b, _blockdiag_dense(w_qblk[0].transpose(0, 2, 1)), _blockdiag_dense(w_kblk[0]),
        _blockdiag_dense(w_vblk[0].transpose(0, 2, 1)), wg3, b_gate[0].reshape(-1, 1))

    y_mla = _attention(q, k, v)
    hf, hb = _mlstm_scan(gates, mq, mk, mv)

    x1, h2, aff_t = _merge(hf, hb, z, xc, gmla, gml, y_mla, x, mod3, ml_norm, ml_skip,
                           w_out[0].astype(BF16), norm2, w_router[0].T)

    pos, cs = _select(aff_t, cap)
    base = jnp.concatenate([cs[:, :, ::MOE_TILE], jnp.full((B, N_EXPERTS, 1), cap, I32)], axis=2).reshape(-1)
    y = _experts(base, h2, pos.reshape(B * N_EXPERTS, T // MOE_TILE, MOE_TILE), w_e_gate[0].astype(BF16),
                 w_e_up[0].astype(BF16), w_e_down[0].astype(BF16), cap)
    return _combine(base, y, pos.transpose(0, 2, 1), aff_t.transpose(0, 2, 1), x1, mod3,
                    final_norm.reshape(1, -1))
```

```python
import functools

import numpy as np
import jax
import jax.numpy as jnp
from jax import lax
from jax.experimental import pallas as pl
from jax.experimental.pallas import tpu as pltpu

F32 = jnp.float32
BF16 = jnp.bfloat16
I32 = jnp.int32

D_MODEL = 1024
GRID_W = 64
MLA_HEADS = 8
QK_NOPE = 128
QK_ROPE = 64
V_DIM = 128
Q_LORA = 384
KV_LORA = 256
ROPE_BASE = 10000.0
ROPE_PAIRS = QK_ROPE // 4
ATTN_SCALE = (QK_NOPE + QK_ROPE) ** -0.5
Q_SCALE = ATTN_SCALE * 1.4426950408889634
ML_HEADS = 4
ML_INNER = 1024
ML_HEAD_DIM = ML_INNER // ML_HEADS
QKV_BLOCK = 4
CONV_W = 5
CHUNK = 128
N_EXPERTS = 16
EXPERT_FF = 1024
CAP_FACTOR = 2
EPS = 1e-6

LANES = 128
SUBLANES = 8
BF16_ROWS = 16
ROW_TILE = 256
HEAD_W = 256
ATTN_TQ = 256
ATTN_TK = 768
MOE_TILE = 256
MOE_WIN = MOE_TILE + BF16_ROWS
VMEM_LIMIT = 56 * 1024 * 1024

_C_QLAT = 0
_C_KVLAT = Q_LORA
_C_KROPE = Q_LORA + KV_LORA
_C_XM = _C_KROPE + LANES
_C_Z = _C_XM + ML_INNER
_C_GMLA = _C_Z + ML_INNER
_C_GML = _C_GMLA + D_MODEL
IN_PAD = _C_GML + D_MODEL

_NT = (((1,), (1,)), ((), ()))
_TN = (((0,), (0,)), ((), ()))


def _params(sem, vmem=VMEM_LIMIT):
    return pltpu.CompilerParams(dimension_semantics=sem, vmem_limit_bytes=vmem)


def _rms(x, g):
    return x * lax.rsqrt(jnp.mean(x * x, axis=-1, keepdims=True) + EPS) * g


def _sigmoid(x):
    return jax.nn.sigmoid(x)


def _mod_kernel(c_ref, w_ref, b_ref, o_ref):
    c = c_ref[...]
    s = c * _sigmoid(c)
    o_ref[...] = jnp.dot(s, w_ref[...], preferred_element_type=F32,
                         precision=lax.Precision.HIGHEST) + b_ref[...]


def _modulation(cc, w_mod, b_mod):
    n = w_mod.shape[1]
    tn = 768
    return pl.pallas_call(
        _mod_kernel,
        grid=(n // tn,),
        in_specs=[pl.BlockSpec((SUBLANES, D_MODEL), lambda j: (0, 0)),
                  pl.BlockSpec((D_MODEL, tn), lambda j: (0, j)),
                  pl.BlockSpec((1, tn), lambda j: (0, j))],
        out_specs=pl.BlockSpec((SUBLANES, tn), lambda j: (0, j)),
        out_shape=jax.ShapeDtypeStruct((SUBLANES, n), F32),
        compiler_params=_params(("arbitrary",)),
        name="modulation",
    )(cc, w_mod, b_mod)


def _rope(v, cos, sin):
    lane = lax.broadcasted_iota(I32, v.shape, 1)
    partner = jnp.where(lane % 32 < 16, pltpu.roll(v, LANES - 16, 1), pltpu.roll(v, 16, 1))
    return v * cos + partner * sin


def _inproj_kernel(ctx_ref, x_ref, sh_ref, sc_ref, n1_ref, win_ref, qn_ref, wuqn_ref, wuqp_ref,
                   kvn_ref, wukv_ref, cos_ref, sin_ref,
                   q_ref, k_ref, v_ref, xm_ref, z_ref, gmla_ref, gml_ref):
    i = pl.program_id(1)
    xin = jnp.where(i == 0, ctx_ref[0], x_ref[0])
    h = _rms(xin, n1_ref[...]) * (1.0 + sc_ref[0]) + sh_ref[0]
    big = jnp.dot(h.astype(BF16), win_ref[...], preferred_element_type=F32)
    xm_ref[0] = big[:, _C_XM:_C_Z]
    z_ref[0] = big[:, _C_Z:_C_GMLA].astype(BF16)
    gmla_ref[0] = big[:, _C_GMLA:_C_GML].astype(BF16)
    gml_ref[0] = big[:, _C_GML:IN_PAD].astype(BF16)

    cos = cos_ref[...]
    sin = sin_ref[...]
    qn = _rms(big[:, _C_QLAT:_C_KVLAT], qn_ref[...]).astype(BF16)
    q_nope = jnp.dot(qn, wuqn_ref[...], preferred_element_type=F32)
    q_pe = jnp.dot(qn, wuqp_ref[...], preferred_element_type=F32)
    kvn = _rms(big[:, _C_KVLAT:_C_KROPE], kvn_ref[...]).astype(BF16)
    kv = jnp.dot(kvn, wukv_ref[...], preferred_element_type=F32)
    k_pe = _rope(big[:, _C_KROPE:_C_XM], cos, sin).astype(BF16)
    ones_col = jnp.where(lax.broadcasted_iota(I32, (big.shape[0], HEAD_W - V_DIM), 1) == 0, 1.0, 0.0).astype(BF16)
    for hh in range(MLA_HEADS):
        lo = hh * HEAD_W
        nope = slice(hh * QK_NOPE, (hh + 1) * QK_NOPE)
        q_ref[0, :, lo:lo + QK_NOPE] = (q_nope[:, nope] * Q_SCALE).astype(BF16)
        q_ref[0, :, lo + QK_NOPE:lo + HEAD_W] = (
            _rope(q_pe[:, hh * LANES:(hh + 1) * LANES], cos, sin) * Q_SCALE).astype(BF16)
        k_ref[0, :, lo:lo + QK_NOPE] = kv[:, nope].astype(BF16)
        k_ref[0, :, lo + QK_NOPE:lo + HEAD_W] = k_pe
        v_ref[0, :, lo:lo + V_DIM] = kv[:, MLA_HEADS * QK_NOPE + hh * V_DIM:MLA_HEADS * QK_NOPE + (hh + 1) * V_DIM].astype(BF16)
        v_ref[0, :, lo + V_DIM:lo + HEAD_W] = ones_col


def _input_projection(ctx, x, mod3, norm1, w_in_p, q_norm, wuq_nope, wuq_pe, kv_norm, wukv_p, cos_t, sin_t):
    B, T, D = x.shape
    n_lat = T // ROW_TILE
    nt = n_lat + 1
    tt = T + ROW_TILE
    const2 = lambda b, i: (0, 0)
    lat = lambda b, i: (b, jnp.maximum(i - 1, 0), 0)
    allrows = lambda b, i: (b, i, 0)
    modrow = lambda col: (lambda b, i: (jnp.where(i == 0, B, b), 0, col))
    return pl.pallas_call(
        _inproj_kernel,
        grid=(B, nt),
        in_specs=[pl.BlockSpec((1, ROW_TILE, D), lambda b, i: (b, 0, 0)),
                  pl.BlockSpec((1, ROW_TILE, D), lat),
                  pl.BlockSpec((1, 1, D), modrow(0)),
                  pl.BlockSpec((1, 1, D), modrow(1)),
                  pl.BlockSpec((1, D), const2),
                  pl.BlockSpec(w_in_p.shape, const2),
                  pl.BlockSpec((1, Q_LORA), const2),
                  pl.BlockSpec(wuq_nope.shape, const2),
                  pl.BlockSpec(wuq_pe.shape, const2),
                  pl.BlockSpec((1, KV_LORA), const2),
                  pl.BlockSpec(wukv_p.shape, const2),
                  pl.BlockSpec((ROW_TILE, LANES), lambda b, i: (i, 0)),
                  pl.BlockSpec((ROW_TILE, LANES), lambda b, i: (i, 0))],
        out_specs=[pl.BlockSpec((1, ROW_TILE, MLA_HEADS * HEAD_W), lat),
                   pl.BlockSpec((1, ROW_TILE, MLA_HEADS * HEAD_W), allrows),
                   pl.BlockSpec((1, ROW_TILE, MLA_HEADS * HEAD_W), allrows),
                   pl.BlockSpec((1, ROW_TILE, ML_INNER), allrows),
                   pl.BlockSpec((1, ROW_TILE, ML_INNER), lat),
                   pl.BlockSpec((1, ROW_TILE, D), lat),
                   pl.BlockSpec((1, ROW_TILE, D), lat)],
        out_shape=[jax.ShapeDtypeStruct((B, T, MLA_HEADS * HEAD_W), BF16),
                   jax.ShapeDtypeStruct((B, tt, MLA_HEADS * HEAD_W), BF16),
                   jax.ShapeDtypeStruct((B, tt, MLA_HEADS * HEAD_W), BF16),
                   jax.ShapeDtypeStruct((B, tt, ML_INNER), F32),
                   jax.ShapeDtypeStruct((B, T, ML_INNER), BF16),
                   jax.ShapeDtypeStruct((B, T, D), BF16),
                   jax.ShapeDtypeStruct((B, T, D), BF16)],
        compiler_params=_params(("arbitrary", "arbitrary")),
        name="input_projection",
    )(ctx, x, mod3, mod3, norm1, w_in_p, q_norm, wuq_nope, wuq_pe, kv_norm, wukv_p, cos_t, sin_t)


def _feat_kernel(prev_ref, cur_ref, next_ref, cw_ref, cb_ref, wq_ref, wk_ref, wv_ref, wg_ref, bg_ref,
                 q_ref, k_ref, v_ref, xc_ref, g_ref, *, n_tiles):
    i = pl.program_id(1)
    cur = cur_ref[0]
    prev = jnp.where(i <= 1, 0.0, prev_ref[0])
    nxt = jnp.where((i == 0) | (i == n_tiles - 1), 0.0, next_ref[0])
    xx = jnp.concatenate([prev, cur, nxt], axis=0)
    rows = cur.shape[0]
    acc = jnp.broadcast_to(cb_ref[...], cur.shape)
    for w in range(CONV_W):
        lo = SUBLANES - CONV_W // 2 + w
        acc = acc + xx[lo:lo + rows] * cw_ref[w:w + 1, :]
    xc = acc * _sigmoid(acc)
    xc_ref[0] = xc.astype(BF16)
    xcb = xc.astype(BF16)
    q_t = lax.dot_general(wq_ref[...], xcb, _NT, preferred_element_type=F32)
    k = jnp.dot(xcb, wk_ref[...], preferred_element_type=F32)
    v_t = lax.dot_general(wv_ref[...], cur.astype(BF16), _NT, preferred_element_type=F32)
    qb, kb, vb = q_t.astype(BF16), k.astype(BF16), v_t.astype(BF16)
    q_ref[0] = qb
    k_ref[0] = (k * (ML_HEAD_DIM ** -0.5)).astype(BF16)
    v_ref[0] = vb
    g = (jnp.dot(wg_ref[0], qb, preferred_element_type=F32)
         + lax.dot_general(wg_ref[1], kb, _NT, preferred_element_type=F32)
         + jnp.dot(wg_ref[2], vb, preferred_element_type=F32)) + bg_ref[...]
    row = lax.broadcasted_iota(I32, (g.shape[0], CHUNK), 0)
    fwd_forget = (row >= ML_HEADS) & (row < 2 * ML_HEADS)
    bwd_forget = row >= 3 * ML_HEADS
    src = lax.broadcasted_iota(I32, (CHUNK, CHUNK), 0)
    dst = lax.broadcasted_iota(I32, (CHUNK, CHUNK), 1)
    tri_prefix = jnp.where(src <= dst, 1.0, 0.0).astype(BF16)
    tri_suffix = jnp.where(src >= dst, 1.0, 0.0).astype(BF16)
    for c in range(g.shape[1] // CHUNK):
        gc = g[:, c * CHUNK:(c + 1) * CHUNK]
        lf = jax.nn.log_sigmoid(gc)
        lf_hi = lf.astype(BF16)
        lf_lo = (lf - lf_hi.astype(F32)).astype(BF16)
        prefix = (jnp.dot(lf_hi, tri_prefix, preferred_element_type=F32)
                  + jnp.dot(lf_lo, tri_prefix, preferred_element_type=F32))
        suffix = (jnp.dot(lf_hi, tri_suffix, preferred_element_type=F32)
                  + jnp.dot(lf_lo, tri_suffix, preferred_element_type=F32))
        g_ref[0, :, c * CHUNK:(c + 1) * CHUNK] = jnp.where(fwd_forget, prefix, jnp.where(bwd_forget, suffix, gc))


def _mlstm_features(xm, conv_w8, conv_b, wq_bd_t, wk_bd, wv_bd_t, wg3, bg_col):
    B, tt, C = xm.shape
    nt = tt // ROW_TILE
    per = ROW_TILE // SUBLANES
    last8 = tt // SUBLANES - 1
    const2 = lambda b, i: (0, 0)
    rows = lambda b, i: (b, i, 0)
    cols = lambda b, i: (b, 0, i)
    ng = 4 * ML_HEADS
    return pl.pallas_call(
        functools.partial(_feat_kernel, n_tiles=nt),
        grid=(B, nt),
        in_specs=[pl.BlockSpec((1, SUBLANES, C), lambda b, i: (b, jnp.maximum(i * per - 1, 0), 0)),
                  pl.BlockSpec((1, ROW_TILE, C), rows),
                  pl.BlockSpec((1, SUBLANES, C), lambda b, i: (b, jnp.minimum((i + 1) * per, last8), 0)),
                  pl.BlockSpec((SUBLANES, C), const2),
                  pl.BlockSpec((1, C), const2),
                  pl.BlockSpec((C, C), const2),
                  pl.BlockSpec((C, C), const2),
                  pl.BlockSpec((C, C), const2),
                  pl.BlockSpec((3, ng, C), lambda b, i: (0, 0, 0)),
                  pl.BlockSpec((ng, 1), const2)],
        out_specs=[pl.BlockSpec((1, C, ROW_TILE), cols),
                   pl.BlockSpec((1, ROW_TILE, C), rows),
                   pl.BlockSpec((1, C, ROW_TILE), cols),
                   pl.BlockSpec((1, ROW_TILE, C), rows),
                   pl.BlockSpec((1, ng, ROW_TILE), cols)],
        out_shape=[jax.ShapeDtypeStruct((B, C, tt), BF16),
                   jax.ShapeDtypeStruct((B, tt, C), BF16),
                   jax.ShapeDtypeStruct((B, C, tt), BF16),
                   jax.ShapeDtypeStruct((B, tt, C), BF16),
                   jax.ShapeDtypeStruct((B, ng, tt), F32)],
        compiler_params=_params(("arbitrary", "arbitrary")),
        name="mlstm_features",
    )(xm, xm, xm, conv_w8, conv_b, wq_bd_t, wk_bd, wv_bd_t, wg3, bg_col)


def _attn_kernel(q_ref, k_ref, v_ref, o_ref, s_ref, p_ref, m_ref, *, n_chunks):
    j = pl.program_id(0)

    @pl.when(j == 0)
    def _():
        s_ref[...] = jnp.zeros_like(s_ref)
        p_ref[...] = jnp.ones_like(p_ref)
        m_ref[...] = jnp.zeros_like(m_ref)

    q = q_ref[0]
    m_old = m_ref[...]
    m = jnp.full((ATTN_TQ, 1), -jnp.inf, F32)
    acc = jnp.zeros((ATTN_TQ, HEAD_W), F32)
    for c in range(n_chunks):
        keys = slice(c * ATTN_TK, (c + 1) * ATTN_TK)
        acc = acc + jnp.dot(p_ref[c], v_ref[0, keys, :], preferred_element_type=F32)
        p_ref[c] = jnp.exp2(s_ref[c] - m_old).astype(BF16)
        s = lax.dot_general(q, k_ref[0, keys, :], _NT, preferred_element_type=F32)
        s_ref[c] = s
        m = jnp.maximum(m, jnp.max(s, axis=-1, keepdims=True))
    m_ref[...] = m
    o_ref[0] = (acc[:, :V_DIM] / acc[:, V_DIM:V_DIM + 1]).astype(BF16)


def _attention(q, k, v):
    B, T, _ = q.shape
    tt = k.shape[1]
    n_chunks = tt // ATTN_TK
    n_tiles = T // ATTN_TQ
    n_total = B * MLA_HEADS * n_tiles
    lag = 2
    assert n_chunks * ATTN_TK == tt and n_tiles * ATTN_TQ == T

    def tile(t):
        t = jnp.clip(t, 0, n_total - 1)
        bh = t // n_tiles
        return bh // MLA_HEADS, t % n_tiles, bh % MLA_HEADS

    def q_idx(j):
        b, i, h = tile(j)
        return b, i, h

    def k_idx(j):
        b, _, h = tile(j)
        return b, 0, h

    def v_idx(j):
        b, _, h = tile(j - lag)
        return b, 0, h

    def o_idx(j):
        b, i, h = tile(j - lag)
        return b, i, h

    return pl.pallas_call(
        functools.partial(_attn_kernel, n_chunks=n_chunks),
        grid=(n_total + lag,),
        in_specs=[pl.BlockSpec((1, ATTN_TQ, HEAD_W), q_idx),
                  pl.BlockSpec((1, tt, HEAD_W), k_idx),
                  pl.BlockSpec((1, tt, HEAD_W), v_idx)],
        out_specs=pl.BlockSpec((1, ATTN_TQ, V_DIM), o_idx),
        out_shape=jax.ShapeDtypeStruct((B, T, MLA_HEADS * V_DIM), BF16),
        scratch_shapes=[pltpu.VMEM((n_chunks, ATTN_TQ, ATTN_TK), F32),
                        pltpu.VMEM((n_chunks, ATTN_TQ, ATTN_TK), BF16),
                        pltpu.VMEM((ATTN_TQ, 1), F32)],
        compiler_params=_params(("arbitrary",)),
        name="attention",
    )(q, k, v)


def _scan_kernel(gf_ref, gb_ref, qf_ref, kf_ref, vf_ref, qb_ref, kb_ref, vb_ref,
                 hf_ref, hb_ref, cn_ref, m_ref):
    j = pl.program_id(0)
    L = CHUNK
    dh = ML_HEAD_DIM

    @pl.when(j == 0)
    def _():
        cn_ref[...] = jnp.zeros_like(cn_ref)
        m_ref[...] = jnp.zeros_like(m_ref)

    spos = lax.broadcasted_iota(I32, (L, L), 0)
    tpos = lax.broadcasted_iota(I32, (L, L), 1)
    first_row = lax.broadcasted_iota(I32, (BF16_ROWS, L), 0) == 0
    dirs = ((gf_ref, qf_ref, kf_ref, vf_ref, hf_ref), (gb_ref, qb_ref, kb_ref, vb_ref, hb_ref))
    for bi in range(gf_ref.shape[0]):
        for d, (g_ref, q_ref, k_ref, v_ref, h_ref) in enumerate(dirs):
            reverse = d == 1
            mask = (spos >= tpos) if reverse else (spos <= tpos)
            g = g_ref[bi]
            ig4 = g[d * 2 * ML_HEADS:d * 2 * ML_HEADS + ML_HEADS]
            b4 = g[d * 2 * ML_HEADS + ML_HEADS:(d + 1) * 2 * ML_HEADS]
            a4 = ig4 - b4
            a_cols = jnp.concatenate([a4, jnp.zeros((L - ML_HEADS, L), F32)], axis=0).T
            for hh in range(ML_HEADS):
                ci = (bi * 2 + d) * ML_HEADS + hh
                b_row, a_row, a_col = b4[hh:hh + 1], a4[hh:hh + 1], a_cols[:, hh:hh + 1]
                b_last = b_row[:, 0:1] if reverse else b_row[:, L - 1:L]
                m = m_ref[ci, 0:1, 0:1]
                sl = slice(hh * dh, (hh + 1) * dh)
                q_t, k, v_t = q_ref[bi, sl, :], k_ref[bi, :, sl], v_ref[bi, sl, :]
                cn = cn_ref[ci]

                dmat = jnp.where(mask, b_row + a_col, -jnp.inf)
                inter = b_row + m
                m_t = jnp.maximum(inter, jnp.max(dmat, axis=0, keepdims=True))
                w_inter = jnp.exp(inter - m_t)
                s = jnp.dot(k, q_t, preferred_element_type=F32) * jnp.exp(dmat - m_t)
                cq = jnp.dot(cn.astype(BF16), q_t, preferred_element_type=F32)
                num = jnp.dot(v_t, s.astype(BF16), preferred_element_type=F32) + w_inter * cq[:dh]
                den = jnp.sum(s, axis=0, keepdims=True) + w_inter * cq[dh:dh + 1]
                h_ref[bi, sl, :] = (num / jnp.maximum(jnp.abs(den), jnp.exp(-m_t))).astype(BF16)

                dec = b_last + a_row
                m_new = jnp.maximum(b_last + m, jnp.max(dec, axis=-1, keepdims=True))
                wk = jnp.exp(dec - m_new)
                keep = jnp.exp(b_last + m - m_new)
                vw = jnp.concatenate([(v_t.astype(F32) * wk).astype(BF16),
                                      jnp.where(first_row, wk, 0.0).astype(BF16)], axis=0)
                cn_ref[ci] = keep * cn + jnp.dot(vw, k, preferred_element_type=F32)
                m_ref[ci] = jnp.broadcast_to(m_new, m_ref.shape[1:])


def _mlstm_scan(gates, q_t, k, v_t):
    B, tt, C = k.shape
    nch = tt // CHUNK
    ng = gates.shape[1]
    nc_ctx = ROW_TILE // CHUNK
    bidx = lambda j: jnp.where(j < nc_ctx, nc_ctx - 1 - j, nch - 1 + nc_ctx - j)
    rows_f = pl.BlockSpec((B, CHUNK, C), lambda j: (0, j, 0))
    rows_b = pl.BlockSpec((B, CHUNK, C), lambda j: (0, bidx(j), 0))
    cols_f = pl.BlockSpec((B, C, CHUNK), lambda j: (0, 0, j))
    cols_b = pl.BlockSpec((B, C, CHUNK), lambda j: (0, 0, bidx(j)))
    nchain = B * 2 * ML_HEADS
    return pl.pallas_call(
        _scan_kernel,
        grid=(nch,),
        in_specs=[pl.BlockSpec((B, ng, CHUNK), lambda j: (0, 0, j)),
                  pl.BlockSpec((B, ng, CHUNK), lambda j: (0, 0, bidx(j))),
                  cols_f, rows_f, cols_f, cols_b, rows_b, cols_b],
        out_specs=[cols_f, cols_b],
        out_shape=[jax.ShapeDtypeStruct((B, C, tt), BF16), jax.ShapeDtypeStruct((B, C, tt), BF16)],
        scratch_shapes=[pltpu.VMEM((nchain, ML_HEAD_DIM + BF16_ROWS, ML_HEAD_DIM), F32),
                        pltpu.VMEM((nchain, SUBLANES, LANES), F32)],
        compiler_params=_params(("arbitrary",)),
        name="mlstm_scan",
    )(gates, gates, q_t, k, v_t, q_t, k, v_t)


def _merge_kernel(hf_ref, hb_ref, z_ref, xc_ref, gmla_ref, gml_ref, ymla_ref, x_ref,
                  g1_ref, sh2_ref, sc2_ref, mln_ref, mls_ref, wout_ref, n2_ref, wr_ref,
                  x1_ref, h2_ref, aff_ref):
    h_t = hf_ref[0].astype(F32) + hb_ref[0].astype(F32)
    parts = []
    for hh in range(ML_HEADS):
        seg = h_t[hh * ML_HEAD_DIM:(hh + 1) * ML_HEAD_DIM]
        parts.append((seg * lax.rsqrt(jnp.mean(seg * seg, axis=0, keepdims=True) + EPS)).T)
    hn = jnp.concatenate(parts, axis=-1) * mln_ref[...]
    y_ml = _sigmoid(z_ref[0].astype(F32)) * (hn + mls_ref[...] * xc_ref[0].astype(F32))
    merged = (_sigmoid(gmla_ref[0].astype(F32)) * ymla_ref[0].astype(F32)
              + _sigmoid(gml_ref[0].astype(F32)) * y_ml)
    out = jnp.dot(merged.astype(BF16), wout_ref[...], preferred_element_type=F32)
    x1 = x_ref[0] + g1_ref[0] * out
    x1_ref[0] = x1
    h2 = _rms(x1, n2_ref[...]) * (1.0 + sc2_ref[0]) + sh2_ref[0]
    h2_hi = h2.astype(BF16)
    h2_ref[0] = h2_hi
    h2_lo = (h2 - h2_hi.astype(F32)).astype(BF16)
    wr = wr_ref[...]
    wr_hi = wr.astype(BF16)
    wr_lo = (wr - wr_hi.astype(F32)).astype(BF16)
    logits = (lax.dot_general(wr_hi, h2_hi, _NT, preferred_element_type=F32)
              + lax.dot_general(wr_hi, h2_lo, _NT, preferred_element_type=F32)
              + lax.dot_general(wr_lo, h2_hi, _NT, preferred_element_type=F32))
    e = jnp.exp(logits - jnp.max(logits, axis=0, keepdims=True))
    aff_ref[0] = e / jnp.sum(e, axis=0, keepdims=True)


def _merge(hf, hb, z, xc, gmla, gml, ymla, x, mod3, ml_norm, ml_skip, w_out, norm2, w_router_t):
    B, T, D = x.shape
    nt = T // ROW_TILE
    const2 = lambda b, i: (0, 0)
    lat = lambda b, i: (b, i, 0)
    shifted = lambda b, i: (b, i + 1, 0)
    modcol = lambda col: (lambda b, i: (b, 0, col))
    tile = lambda idx: pl.BlockSpec((1, ROW_TILE, D), idx)
    h_tile = pl.BlockSpec((1, ML_INNER, ROW_TILE), lambda b, i: (b, 0, i + 1))
    return pl.pallas_call(
        _merge_kernel,
        grid=(B, nt),
        in_specs=[h_tile, h_tile, tile(lat), tile(shifted), tile(lat), tile(lat), tile(lat),
                  tile(lat),
                  pl.BlockSpec((1, 1, D), modcol(2)), pl.BlockSpec((1, 1, D), modcol(3)),
                  pl.BlockSpec((1, 1, D), modcol(4)),
                  pl.BlockSpec((1, D), const2), pl.BlockSpec((1, D), const2),
                  pl.BlockSpec((D, D), const2), pl.BlockSpec((1, D), const2),
                  pl.BlockSpec((N_EXPERTS, D), const2)],
        out_specs=[tile(lat), tile(lat), pl.BlockSpec((1, N_EXPERTS, ROW_TILE), lambda b, i: (b, 0, i))],
        out_shape=[jax.ShapeDtypeStruct((B, T, D), F32),
                   jax.ShapeDtypeStruct((B, T, D), BF16),
                   jax.ShapeDtypeStruct((B, N_EXPERTS, T), F32)],
        compiler_params=_params(("arbitrary", "arbitrary")),
        name="merge_router",
    )(hf, hb, z, xc, gmla, gml, ymla, x, mod3, mod3, mod3, ml_norm, ml_skip, w_out, norm2, w_router_t)


def _chunked_cumsum(mask_f, tri):
    n_e, t = mask_f.shape
    off = jnp.zeros((n_e, 1), F32)
    outs = []
    for c in range(t // LANES):
        x = mask_f[:, c * LANES:(c + 1) * LANES]
        inc = jnp.dot(x.astype(BF16), tri, preferred_element_type=F32)
        outs.append(inc - x + off)
        off = off + inc[:, LANES - 1:LANES]
    return jnp.concatenate(outs, axis=1)


def _select_kernel(aff_ref, pos_ref, cs_ref, *, cap):
    aff = aff_ref[0]
    n_e = aff.shape[0]

    def count_ge(t):
        return jnp.sum(jnp.where(aff >= t, 1.0, 0.0), axis=1, keepdims=True)

    def body(carry):
        lo, hi, _ = carry
        mid = 0.5 * (lo + hi)
        ok = count_ge(mid) >= cap
        lo, hi = jnp.where(ok, mid, lo), jnp.where(ok, hi, mid)
        mid = 0.5 * (lo + hi)
        return lo, hi, jnp.max(jnp.where((mid > lo) & (mid < hi), 1.0, 0.0))

    lo, hi, _ = lax.while_loop(lambda carry: carry[2] > 0.5, body,
                               (jnp.zeros((n_e, 1), F32), jnp.full((n_e, 1), 2.0, F32), jnp.float32(1.0)))
    gt = jnp.where(aff >= hi, 1.0, 0.0)
    eq = jnp.where(aff >= lo, 1.0, 0.0) - gt
    need = cap - jnp.sum(gt, axis=1, keepdims=True)
    tri = jnp.where(lax.broadcasted_iota(I32, (LANES, LANES), 0) <= lax.broadcasted_iota(I32, (LANES, LANES), 1),
                    1.0, 0.0).astype(BF16)
    eq_rank = _chunked_cumsum(eq, tri)
    sel = gt + eq * jnp.where(eq_rank < need, 1.0, 0.0)
    cs = _chunked_cumsum(sel, tri)
    cs_ref[0] = cs.astype(I32)
    pos_ref[0] = jnp.where(sel > 0.5, cs, -1.0).astype(I32)


def _select(aff_t, cap):
    B, n_e, T = aff_t.shape
    blk = pl.BlockSpec((1, n_e, T), lambda b: (b, 0, 0))
    return pl.pallas_call(
        functools.partial(_select_kernel, cap=cap),
        grid=(B,),
        in_specs=[blk],
        out_specs=[blk, blk],
        out_shape=[jax.ShapeDtypeStruct((B, n_e, T), I32), jax.ShapeDtypeStruct((B, n_e, T), I32)],
        compiler_params=_params(("arbitrary",)),
        name="expert_select",
    )(aff_t)


def _window_start(base_ref, flat):
    start = base_ref[flat]
    return pl.multiple_of((start // BF16_ROWS) * BF16_ROWS, BF16_ROWS)


def _expert_kernel(base_ref, x_ref, pos_ref, wg_ref, wu_ref, wd_ref, y_ref, xs_ref, *, n_tiles, cap):
    b, e = pl.program_id(0), pl.program_id(1)
    head = BF16_ROWS
    xs_ref[0:head, :] = jnp.zeros((head, xs_ref.shape[1]), BF16)
    row_id = lax.broadcasted_iota(I32, (MOE_WIN, MOE_TILE), 0)

    def gather(k, carry):
        s16 = _window_start(base_ref, (b * N_EXPERTS + e) * (n_tiles + 1) + k)
        rel = pos_ref[0, pl.ds(k, 1), :] - s16
        onehot = jnp.where(row_id == rel, 1.0, 0.0).astype(BF16)
        tok = pl.ds(pl.multiple_of(k * MOE_TILE, MOE_TILE), MOE_TILE)
        rows = jnp.dot(onehot, x_ref[0, tok, :], preferred_element_type=F32)
        first = xs_ref[pl.ds(s16, head), :].astype(F32) + rows[:head]
        xs_ref[pl.ds(s16 + head, MOE_WIN - head), :] = rows[head:].astype(BF16)
        xs_ref[pl.ds(s16, head), :] = first.astype(BF16)
        return carry

    lax.fori_loop(0, n_tiles, gather, 0, unroll=4)

    for r in range(cap // ROW_TILE):
        rows = slice(r * ROW_TILE, (r + 1) * ROW_TILE)
        xs = xs_ref[rows, :]
        a = jnp.dot(xs, wg_ref[0], preferred_element_type=F32)
        u = jnp.dot(xs, wu_ref[0], preferred_element_type=F32)
        hm = (a * _sigmoid(a) * u).astype(BF16)
        y_ref[0, 0, rows, :] = jnp.dot(hm, wd_ref[0], preferred_element_type=F32).astype(BF16)
    y_ref[0, 0, cap:, :] = jnp.zeros((y_ref.shape[2] - cap, y_ref.shape[3]), BF16)


def _experts(base, h2, pos3, wg, wu, wd, cap):
    B, T, D = h2.shape
    nk = T // MOE_TILE
    yr = cap + MOE_WIN
    ff = wg.shape[2]
    wspec = lambda shape: pl.BlockSpec((1,) + shape, lambda b, e, base: (e, 0, 0))
    grid_spec = pltpu.PrefetchScalarGridSpec(
        num_scalar_prefetch=1,
        grid=(B, N_EXPERTS),
        in_specs=[pl.BlockSpec((1, T, D), lambda b, e, base: (b, 0, 0), pipeline_mode=pl.Buffered(1)),
                  pl.BlockSpec((1, nk, MOE_TILE), lambda b, e, base: (b * N_EXPERTS + e, 0, 0)),
                  wspec((D, ff)), wspec((D, ff)), wspec((ff, D))],
        out_specs=pl.BlockSpec((1, 1, yr, D), lambda b, e, base: (b, e, 0, 0)),
        scratch_shapes=[pltpu.VMEM((yr, D), BF16)])
    return pl.pallas_call(
        functools.partial(_expert_kernel, n_tiles=nk, cap=cap),
        grid_spec=grid_spec,
        out_shape=jax.ShapeDtypeStruct((B, N_EXPERTS, yr, D), BF16),
        compiler_params=_params(("arbitrary", "arbitrary")),
        name="expert_ffn",
    )(base, h2, pos3, wg, wu, wd)


def _combine_kernel(base_ref, y_ref, pos_ref, g_ref, x1_ref, g2_ref, fn_ref, o_ref, *, n_tiles, tiles_per_blk):
    b, tb, e = pl.program_id(0), pl.program_id(1), pl.program_id(2)
    flat0 = (b * N_EXPERTS + e) * (n_tiles + 1) + tb * tiles_per_blk
    pick = lax.broadcasted_iota(I32, (MOE_TILE, N_EXPERTS), 1) == e
    col_id = lax.broadcasted_iota(I32, (MOE_TILE, MOE_TILE), 1).astype(F32)

    @pl.when(e == 0)
    def _():
        o_ref[...] = jnp.zeros_like(o_ref)

    def columns(k):
        rows = slice(k * MOE_TILE, (k + 1) * MOE_TILE)
        pcol = jnp.sum(jnp.where(pick, pos_ref[0, rows, :].astype(F32), 0.0), axis=1, keepdims=True)
        gcol = jnp.sum(jnp.where(pick, g_ref[0, rows, :], 0.0), axis=1, keepdims=True)
        return rows, pcol, gcol

    for k in range(tiles_per_blk):
        s16 = _window_start(base_ref, flat0 + k)
        rows, pcol, gcol = columns(k)
        onehot = jnp.where(col_id == pcol - s16.astype(F32), 1.0, 0.0).astype(BF16)
        o_ref[0, rows, :] += jnp.dot(onehot, y_ref[0, 0, pl.ds(s16, MOE_TILE), :],
                                     preferred_element_type=F32) * gcol

    for k in range(tiles_per_blk):
        s16 = _window_start(base_ref, flat0 + k)

        @pl.when(base_ref[flat0 + k + 1] > s16 + MOE_TILE)
        def _():
            rows, pcol, gcol = columns(k)
            tail = lax.broadcasted_iota(I32, (MOE_TILE, BF16_ROWS), 1).astype(F32) + float(MOE_TILE)
            onehot = jnp.where(tail == pcol - s16.astype(F32), 1.0, 0.0).astype(BF16)
            o_ref[0, rows, :] += jnp.dot(onehot, y_ref[0, 0, pl.ds(s16 + MOE_TILE, BF16_ROWS), :],
                                         preferred_element_type=F32) * gcol

    @pl.when(e == N_EXPERTS - 1)
    def _():
        for k in range(tiles_per_blk):
            rows = slice(k * MOE_TILE, (k + 1) * MOE_TILE)
            x2 = x1_ref[0, rows, :] + g2_ref[0] * o_ref[0, rows, :]
            o_ref[0, rows, :] = _rms(x2, fn_ref[...])


def _combine(base, y, pos_t, g_t, x1, mod3, final_norm):
    B, T, D = x1.shape
    nk = T // MOE_TILE
    per = 8
    blk = per * MOE_TILE
    yr = y.shape[2]
    tok = lambda b, tb, e, base: (b, tb, 0)
    grid_spec = pltpu.PrefetchScalarGridSpec(
        num_scalar_prefetch=1,
        grid=(B, nk // per, N_EXPERTS),
        in_specs=[pl.BlockSpec((1, 1, yr, D), lambda b, tb, e, base: (b, e, 0, 0)),
                  pl.BlockSpec((1, blk, N_EXPERTS), tok),
                  pl.BlockSpec((1, blk, N_EXPERTS), tok),
                  pl.BlockSpec((1, blk, D), tok),
                  pl.BlockSpec((1, 1, D), lambda b, tb, e, base: (b, 0, 5)),
                  pl.BlockSpec((1, D), lambda b, tb, e, base: (0, 0))],
        out_specs=pl.BlockSpec((1, blk, D), tok))
    return pl.pallas_call(
        functools.partial(_combine_kernel, n_tiles=nk, tiles_per_blk=per),
        grid_spec=grid_spec,
        out_shape=jax.ShapeDtypeStruct((B, T, D), F32),
        compiler_params=_params(("arbitrary",) * 3),
        name="moe_combine",
    )(base, y, pos_t, g_t, x1, mod3, final_norm)


def _rope_tables(T):
    rows = T // GRID_W
    row = np.repeat(np.arange(rows, dtype=np.float64), GRID_W)
    col = np.tile(np.arange(GRID_W, dtype=np.float64), rows)
    inv = ROPE_BASE ** (-np.arange(ROPE_PAIRS, dtype=np.float64) / ROPE_PAIRS)
    ar, ac = row[:, None] * inv, col[:, None] * inv
    ones = np.ones((T, LANES - QK_ROPE))
    cos = np.concatenate([np.cos(ar), np.cos(ar), np.cos(ac), np.cos(ac), ones], axis=1)
    sin = np.concatenate([-np.sin(ar), np.sin(ar), -np.sin(ac), np.sin(ac), 0.0 * ones], axis=1)
    cos = np.concatenate([np.ones((ROW_TILE, LANES)), cos], axis=0)
    sin = np.concatenate([np.zeros((ROW_TILE, LANES)), sin], axis=0)
    return jnp.asarray(cos, F32), jnp.asarray(sin, F32)


def _blockdiag_dense(w):
    n, bs, _ = w.shape
    rows = jnp.broadcast_to(w.transpose(1, 0, 2).reshape(1, bs, n * bs), (n, bs, n * bs)).reshape(n * bs, n * bs)
    r = jnp.arange(n * bs) // bs
    return jnp.where(r[:, None] == r[None, :], rows, 0.0).astype(BF16)


def kernel(x, c, ctx, c_ctx, w_mod, b_mod, norm1, w_in, q_norm, w_uq, kv_norm, w_ukv, conv_w, conv_b,
           w_qblk, w_kblk, w_vblk, w_gate, b_gate, ml_norm, ml_skip, w_out, norm2, w_router,
           w_e_gate, w_e_up, w_e_down, final_norm):
    B, T, D = x.shape
    assert w_mod.shape[0] == 1 and D == D_MODEL and ctx.shape[1] == ROW_TILE
    cap = CAP_FACTOR * T // N_EXPERTS

    cc = jnp.zeros((SUBLANES, D), F32).at[:B].set(c).at[B].set(c_ctx)
    mod = _modulation(cc, w_mod[0], b_mod[0].reshape(1, -1))
    mod3 = mod[:B + 1].reshape(B + 1, 1, 6 * D)

    wi = w_in[0]
    zpad = jnp.zeros((D, LANES - QK_ROPE), F32)
    w_in_p = jnp.concatenate([wi[:, :_C_KROPE + QK_ROPE], zpad, wi[:, _C_KROPE + QK_ROPE:]], axis=1).astype(BF16)
    wuq = w_uq[0].reshape(Q_LORA, MLA_HEADS, QK_NOPE + QK_ROPE)
    wuq_nope = wuq[:, :, :QK_NOPE].reshape(Q_LORA, MLA_HEADS * QK_NOPE).astype(BF16)
    wuq_pe = jnp.pad(wuq[:, :, QK_NOPE:], ((0, 0), (0, 0), (0, LANES - QK_ROPE))).reshape(
        Q_LORA, MLA_HEADS * LANES).astype(BF16)
    wukv = w_ukv[0].reshape(KV_LORA, MLA_HEADS, QK_NOPE + V_DIM)
    wukv_p = jnp.concatenate([wukv[:, :, :QK_NOPE].reshape(KV_LORA, -1),
                              wukv[:, :, QK_NOPE:].reshape(KV_LORA, -1)], axis=1).astype(BF16)
    cos_t, sin_t = _rope_tables(T)

    q, k, v, xm, z, gmla, gml = _input_projection(
        ctx, x, mod3, norm1, w_in_p, q_norm, wuq_nope, wuq_pe, kv_norm, wukv_p, cos_t, sin_t)

    conv_w8 = jnp.zeros((SUBLANES, ML_INNER), F32).at[:CONV_W].set(conv_w[0])
    wg3 = w_gate[0].reshape(3, ML_INNER, 4 * ML_HEADS).transpose(0, 2, 1).astype(BF16)
    mq, mk, mv, xc, gates = _mlstm_features(
        xm, conv_w8, conv_b, _blockdiag_dense(w_qblk[0].transpose(0, 2, 1)), _blockdiag_dense(w_kblk[0]),
        _blockdiag_dense(w_vblk[0].transpose(0, 2, 1)), wg3, b_gate[0].reshape(-1, 1))

    y_mla = _attention(q, k, v)
    hf, hb = _mlstm_scan(gates, mq, mk, mv)

    x1, h2, aff_t = _merge(hf, hb, z, xc, gmla, gml, y_mla, x, mod3, ml_norm, ml_skip,
                           w_out[0].astype(BF16), norm2, w_router[0].T)

    pos, cs = _select(aff_t, cap)
    base = jnp.concatenate([cs[:, :, ::MOE_TILE], jnp.full((B, N_EXPERTS, 1), cap, I32)], axis=2).reshape(-1)
    y = _experts(base, h2, pos.reshape(B * N_EXPERTS, T // MOE_TILE, MOE_TILE), w_e_gate[0].astype(BF16),
                 w_e_up[0].astype(BF16), w_e_down[0].astype(BF16), cap)
    return _combine(base, y, pos.transpose(0, 2, 1), aff_t.transpose(0, 2, 1), x1, mod3,
                    final_norm.reshape(1, -1))
```

```python
import functools

import numpy as np
import jax
import jax.numpy as jnp
from jax import lax
from jax.experimental import pallas as pl
from jax.experimental.pallas import tpu as pltpu

F32 = jnp.float32
BF16 = jnp.bfloat16
I32 = jnp.int32

D_MODEL = 1024
GRID_W = 64
MLA_HEADS = 8
QK_NOPE = 128
QK_ROPE = 64
V_DIM = 128
Q_LORA = 384
KV_LORA = 256
ROPE_BASE = 10000.0
ROPE_PAIRS = QK_ROPE // 4
ATTN_SCALE = (QK_NOPE + QK_ROPE) ** -0.5
Q_SCALE = ATTN_SCALE * 1.4426950408889634
ML_HEADS = 4
ML_INNER = 1024
ML_HEAD_DIM = ML_INNER // ML_HEADS
QKV_BLOCK = 4
CONV_W = 5
CHUNK = 128
N_EXPERTS = 16
EXPERT_FF = 1024
CAP_FACTOR = 2
EPS = 1e-6

LANES = 128
SUBLANES = 8
BF16_ROWS = 16
ROW_TILE = 256
HEAD_W = 256
ATTN_TQ = 256
ATTN_TK = 768
MOE_TILE = 256
MOE_WIN = MOE_TILE + BF16_ROWS
VMEM_LIMIT = 56 * 1024 * 1024

_C_QLAT = 0
_C_KVLAT = Q_LORA
_C_KROPE = Q_LORA + KV_LORA
_C_XM = _C_KROPE + LANES
_C_Z = _C_XM + ML_INNER
_C_GMLA = _C_Z + ML_INNER
_C_GML = _C_GMLA + D_MODEL
IN_PAD = _C_GML + D_MODEL

_NT = (((1,), (1,)), ((), ()))
_TN = (((0,), (0,)), ((), ()))


def _params(sem, vmem=VMEM_LIMIT):
    return pltpu.CompilerParams(dimension_semantics=sem, vmem_limit_bytes=vmem)


def _rms(x, g):
    return x * lax.rsqrt(jnp.mean(x * x, axis=-1, keepdims=True) + EPS) * g


def _sigmoid(x):
    return jax.nn.sigmoid(x)


def _mod_kernel(c_ref, w_ref, b_ref, o_ref):
    c = c_ref[...]
    s = c * _sigmoid(c)
    o_ref[...] = jnp.dot(s, w_ref[...], preferred_element_type=F32,
                         precision=lax.Precision.HIGHEST) + b_ref[...]


def _modulation(cc, w_mod, b_mod):
    n = w_mod.shape[1]
    tn = 768
    return pl.pallas_call(
        _mod_kernel,
        grid=(n // tn,),
        in_specs=[pl.BlockSpec((SUBLANES, D_MODEL), lambda j: (0, 0)),
                  pl.BlockSpec((D_MODEL, tn), lambda j: (0, j)),
                  pl.BlockSpec((1, tn), lambda j: (0, j))],
        out_specs=pl.BlockSpec((SUBLANES, tn), lambda j: (0, j)),
        out_shape=jax.ShapeDtypeStruct((SUBLANES, n), F32),
        compiler_params=_params(("arbitrary",)),
        name="modulation",
    )(cc, w_mod, b_mod)


def _rope(v, cos, sin):
    lane = lax.broadcasted_iota(I32, v.shape, 1)
    partner = jnp.where(lane % 32 < 16, pltpu.roll(v, LANES - 16, 1), pltpu.roll(v, 16, 1))
    return v * cos + partner * sin


def _inproj_kernel(ctx_ref, x_ref, sh_ref, sc_ref, n1_ref, win_ref, qn_ref, wuqn_ref, wuqp_ref,
                   kvn_ref, wukv_ref, cos_ref, sin_ref,
                   q_ref, k_ref, v_ref, xm_ref, z_ref, gmla_ref, gml_ref):
    i = pl.program_id(1)
    xin = jnp.where(i == 0, ctx_ref[0], x_ref[0])
    h = _rms(xin, n1_ref[...]) * (1.0 + sc_ref[0]) + sh_ref[0]
    big = jnp.dot(h.astype(BF16), win_ref[...], preferred_element_type=F32)
    xm_ref[0] = big[:, _C_XM:_C_Z]
    z_ref[0] = big[:, _C_Z:_C_GMLA].astype(BF16)
    gmla_ref[0] = big[:, _C_GMLA:_C_GML].astype(BF16)
    gml_ref[0] = big[:, _C_GML:IN_PAD].astype(BF16)

    cos = cos_ref[...]
    sin = sin_ref[...]
    qn = _rms(big[:, _C_QLAT:_C_KVLAT], qn_ref[...]).astype(BF16)
    q_nope = jnp.dot(qn, wuqn_ref[...], preferred_element_type=F32)
    q_pe = jnp.dot(qn, wuqp_ref[...], preferred_element_type=F32)
    kvn = _rms(big[:, _C_KVLAT:_C_KROPE], kvn_ref[...]).astype(BF16)
    kv = jnp.dot(kvn, wukv_ref[...], preferred_element_type=F32)
    k_pe = _rope(big[:, _C_KROPE:_C_XM], cos, sin).astype(BF16)
    ones_col = jnp.where(lax.broadcasted_iota(I32, (big.shape[0], HEAD_W - V_DIM), 1) == 0, 1.0, 0.0).astype(BF16)
    for hh in range(MLA_HEADS):
        lo = hh * HEAD_W
        nope = slice(hh * QK_NOPE, (hh + 1) * QK_NOPE)
        q_ref[0, :, lo:lo + QK_NOPE] = (q_nope[:, nope] * Q_SCALE).astype(BF16)
        q_ref[0, :, lo + QK_NOPE:lo + HEAD_W] = (
            _rope(q_pe[:, hh * LANES:(hh + 1) * LANES], cos, sin) * Q_SCALE).astype(BF16)
        k_ref[0, :, lo:lo + QK_NOPE] = kv[:, nope].astype(BF16)
        k_ref[0, :, lo + QK_NOPE:lo + HEAD_W] = k_pe
        v_ref[0, :, lo:lo + V_DIM] = kv[:, MLA_HEADS * QK_NOPE + hh * V_DIM:MLA_HEADS * QK_NOPE + (hh + 1) * V_DIM].astype(BF16)
        v_ref[0, :, lo + V_DIM:lo + HEAD_W] = ones_col


def _input_projection(ctx, x, mod3, norm1, w_in_p, q_norm, wuq_nope, wuq_pe, kv_norm, wukv_p, cos_t, sin_t):
    B, T, D = x.shape
    n_lat = T // ROW_TILE
    nt = n_lat + 1
    tt = T + ROW_TILE
    const2 = lambda b, i: (0, 0)
    lat = lambda b, i: (b, jnp.maximum(i - 1, 0), 0)
    allrows = lambda b, i: (b, i, 0)
    modrow = lambda col: (lambda b, i: (jnp.where(i == 0, B, b), 0, col))
    return pl.pallas_call(
        _inproj_kernel,
        grid=(B, nt),
        in_specs=[pl.BlockSpec((1, ROW_TILE, D), lambda b, i: (b, 0, 0)),
                  pl.BlockSpec((1, ROW_TILE, D), lat),
                  pl.BlockSpec((1, 1, D), modrow(0)),
                  pl.BlockSpec((1, 1, D), modrow(1)),
                  pl.BlockSpec((1, D), const2),
                  pl.BlockSpec(w_in_p.shape, const2),
                  pl.BlockSpec((1, Q_LORA), const2),
                  pl.BlockSpec(wuq_nope.shape, const2),
                  pl.BlockSpec(wuq_pe.shape, const2),
                  pl.BlockSpec((1, KV_LORA), const2),
                  pl.BlockSpec(wukv_p.shape, const2),
                  pl.BlockSpec((ROW_TILE, LANES), lambda b, i: (i, 0)),
                  pl.BlockSpec((ROW_TILE, LANES), lambda b, i: (i, 0))],
        out_specs=[pl.BlockSpec((1, ROW_TILE, MLA_HEADS * HEAD_W), lat),
                   pl.BlockSpec((1, ROW_TILE, MLA_HEADS * HEAD_W), allrows),
                   pl.BlockSpec((1, ROW_TILE, MLA_HEADS * HEAD_W), allrows),
                   pl.BlockSpec((1, ROW_TILE, ML_INNER), allrows),
                   pl.BlockSpec((1, ROW_TILE, ML_INNER), lat),
                   pl.BlockSpec((1, ROW_TILE, D), lat),
                   pl.BlockSpec((1, ROW_TILE, D), lat)],
        out_shape=[jax.ShapeDtypeStruct((B, T, MLA_HEADS * HEAD_W), BF16),
                   jax.ShapeDtypeStruct((B, tt, MLA_HEADS * HEAD_W), BF16),
                   jax.ShapeDtypeStruct((B, tt, MLA_HEADS * HEAD_W), BF16),
                   jax.ShapeDtypeStruct((B, tt, ML_INNER), F32),
                   jax.ShapeDtypeStruct((B, T, ML_INNER), BF16),
                   jax.ShapeDtypeStruct((B, T, D), BF16),
                   jax.ShapeDtypeStruct((B, T, D), BF16)],
        compiler_params=_params(("arbitrary", "arbitrary")),
        name="input_projection",
    )(ctx, x, mod3, mod3, norm1, w_in_p, q_norm, wuq_nope, wuq_pe, kv_norm, wukv_p, cos_t, sin_t)


def _feat_kernel(prev_ref, cur_ref, next_ref, cw_ref, cb_ref, wq_ref, wk_ref, wv_ref, wg_ref, bg_ref,
                 q_ref, k_ref, v_ref, xc_ref, g_ref, *, n_tiles):
    i = pl.program_id(1)
    cur = cur_ref[0]
    prev = jnp.where(i <= 1, 0.0, prev_ref[0])
    nxt = jnp.where((i == 0) | (i == n_tiles - 1), 0.0, next_ref[0])
    xx = jnp.concatenate([prev, cur, nxt], axis=0)
    rows = cur.shape[0]
    acc = jnp.broadcast_to(cb_ref[...], cur.shape)
    for w in range(CONV_W):
        lo = SUBLANES - CONV_W // 2 + w
        acc = acc + xx[lo:lo + rows] * cw_ref[w:w + 1, :]
    xc = acc * _sigmoid(acc)
    xc_ref[0] = xc.astype(BF16)
    xcb = xc.astype(BF16)
    q_t = lax.dot_general(wq_ref[...], xcb, _NT, preferred_element_type=F32)
    k = jnp.dot(xcb, wk_ref[...], preferred_element_type=F32)
    v_t = lax.dot_general(wv_ref[...], cur.astype(BF16), _NT, preferred_element_type=F32)
    qb, kb, vb = q_t.astype(BF16), k.astype(BF16), v_t.astype(BF16)
    q_ref[0] = qb
    k_ref[0] = (k * (ML_HEAD_DIM ** -0.5)).astype(BF16)
    v_ref[0] = vb
    g = (jnp.dot(wg_ref[0], qb, preferred_element_type=F32)
         + lax.dot_general(wg_ref[1], kb, _NT, preferred_element_type=F32)
         + jnp.dot(wg_ref[2], vb, preferred_element_type=F32)) + bg_ref[...]
    row = lax.broadcasted_iota(I32, (g.shape[0], CHUNK), 0)
    fwd_forget = (row >= ML_HEADS) & (row < 2 * ML_HEADS)
    bwd_forget = row >= 3 * ML_HEADS
    src = lax.broadcasted_iota(I32, (CHUNK, CHUNK), 0)
    dst = lax.broadcasted_iota(I32, (CHUNK, CHUNK), 1)
    tri_prefix = jnp.where(src <= dst, 1.0, 0.0).astype(BF16)
    tri_suffix = jnp.where(src >= dst, 1.0, 0.0).astype(BF16)
    for c in range(g.shape[1] // CHUNK):
        gc = g[:, c * CHUNK:(c + 1) * CHUNK]
        lf = jax.nn.log_sigmoid(gc)
        lf_hi = lf.astype(BF16)
        lf_lo = (lf - lf_hi.astype(F32)).astype(BF16)
        prefix = (jnp.dot(lf_hi, tri_prefix, preferred_element_type=F32)
                  + jnp.dot(lf_lo, tri_prefix, preferred_element_type=F32))
        suffix = (jnp.dot(lf_hi, tri_suffix, preferred_element_type=F32)
                  + jnp.dot(lf_lo, tri_suffix, preferred_element_type=F32))
        g_ref[0, :, c * CHUNK:(c + 1) * CHUNK] = jnp.where(fwd_forget, prefix, jnp.where(bwd_forget, suffix, gc))


def _mlstm_features(xm, conv_w8, conv_b, wq_bd_t, wk_bd, wv_bd_t, wg3, bg_col):
    B, tt, C = xm.shape
    nt = tt // ROW_TILE
    per = ROW_TILE // SUBLANES
    last8 = tt // SUBLANES - 1
    const2 = lambda b, i: (0, 0)
    rows = lambda b, i: (b, i, 0)
    cols = lambda b, i: (b, 0, i)
    ng = 4 * ML_HEADS
    return pl.pallas_call(
        functools.partial(_feat_kernel, n_tiles=nt),
        grid=(B, nt),
        in_specs=[pl.BlockSpec((1, SUBLANES, C), lambda b, i: (b, jnp.maximum(i * per - 1, 0), 0)),
                  pl.BlockSpec((1, ROW_TILE, C), rows),
                  pl.BlockSpec((1, SUBLANES, C), lambda b, i: (b, jnp.minimum((i + 1) * per, last8), 0)),
                  pl.BlockSpec((SUBLANES, C), const2),
                  pl.BlockSpec((1, C), const2),
                  pl.BlockSpec((C, C), const2),
                  pl.BlockSpec((C, C), const2),
                  pl.BlockSpec((C, C), const2),
                  pl.BlockSpec((3, ng, C), lambda b, i: (0, 0, 0)),
                  pl.BlockSpec((ng, 1), const2)],
        out_specs=[pl.BlockSpec((1, C, ROW_TILE), cols),
                   pl.BlockSpec((1, ROW_TILE, C), rows),
                   pl.BlockSpec((1, C, ROW_TILE), cols),
                   pl.BlockSpec((1, ROW_TILE, C), rows),
                   pl.BlockSpec((1, ng, ROW_TILE), cols)],
        out_shape=[jax.ShapeDtypeStruct((B, C, tt), BF16),
                   jax.ShapeDtypeStruct((B, tt, C), BF16),
                   jax.ShapeDtypeStruct((B, C, tt), BF16),
                   jax.ShapeDtypeStruct((B, tt, C), BF16),
                   jax.ShapeDtypeStruct((B, ng, tt), F32)],
        compiler_params=_params(("arbitrary", "arbitrary")),
        name="mlstm_features",
    )(xm, xm, xm, conv_w8, conv_b, wq_bd_t, wk_bd, wv_bd_t, wg3, bg_col)


def _attn_kernel(q_ref, k_ref, v_ref, o_ref, s_ref, p_ref, m_ref, *, n_chunks):
    j = pl.program_id(0)

    @pl.when(j == 0)
    def _():
        s_ref[...] = jnp.zeros_like(s_ref)
        p_ref[...] = jnp.ones_like(p_ref)
        m_ref[...] = jnp.zeros_like(m_ref)

    q = q_ref[0]
    m_old = m_ref[...]
    m = jnp.full((ATTN_TQ, 1), -jnp.inf, F32)
    acc = jnp.zeros((ATTN_TQ, HEAD_W), F32)
    for c in range(n_chunks):
        keys = slice(c * ATTN_TK, (c + 1) * ATTN_TK)
        acc = acc + jnp.dot(p_ref[c], v_ref[0, keys, :], preferred_element_type=F32)
        p_ref[c] = jnp.exp2(s_ref[c] - m_old).astype(BF16)
        s = lax.dot_general(q, k_ref[0, keys, :], _NT, preferred_element_type=F32)
        s_ref[c] = s
        m = jnp.maximum(m, jnp.max(s, axis=-1, keepdims=True))
    m_ref[...] = m
    o_ref[0] = (acc[:, :V_DIM] / acc[:, V_DIM:V_DIM + 1]).astype(BF16)


def _attention(q, k, v):
    B, T, _ = q.shape
    tt = k.shape[1]
    n_chunks = tt // ATTN_TK
    n_tiles = T // ATTN_TQ
    n_total = B * MLA_HEADS * n_tiles
    lag = 2
    assert n_chunks * ATTN_TK == tt and n_tiles * ATTN_TQ == T

    def tile(t):
        t = jnp.clip(t, 0, n_total - 1)
        bh = t // n_tiles
        return bh // MLA_HEADS, t % n_tiles, bh % MLA_HEADS

    def q_idx(j):
        b, i, h = tile(j)
        return b, i, h

    def k_idx(j):
        b, _, h = tile(j)
        return b, 0, h

    def v_idx(j):
        b, _, h = tile(j - lag)
        return b, 0, h

    def o_idx(j):
        b, i, h = tile(j - lag)
        return b, i, h

    return pl.pallas_call(
        functools.partial(_attn_kernel, n_chunks=n_chunks),
        grid=(n_total + lag,),
        in_specs=[pl.BlockSpec((1, ATTN_TQ, HEAD_W), q_idx),
                  pl.BlockSpec((1, tt, HEAD_W), k_idx),
                  pl.BlockSpec((1, tt, HEAD_W), v_idx)],
        out_specs=pl.BlockSpec((1, ATTN_TQ, V_DIM), o_idx),
        out_shape=jax.ShapeDtypeStruct((B, T, MLA_HEADS * V_DIM), BF16),
        scratch_shapes=[pltpu.VMEM((n_chunks, ATTN_TQ, ATTN_TK), F32),
                        pltpu.VMEM((n_chunks, ATTN_TQ, ATTN_TK), BF16),
                        pltpu.VMEM((ATTN_TQ, 1), F32)],
        compiler_params=_params(("arbitrary",)),
        name="attention",
    )(q, k, v)


def _scan_kernel(gf_ref, gb_ref, qf_ref, kf_ref, vf_ref, qb_ref, kb_ref, vb_ref,
                 hf_ref, hb_ref, cn_ref, m_ref):
    j = pl.program_id(0)
    L = CHUNK
    dh = ML_HEAD_DIM

    @pl.when(j == 0)
    def _():
        cn_ref[...] = jnp.zeros_like(cn_ref)
        m_ref[...] = jnp.zeros_like(m_ref)

    spos = lax.broadcasted_iota(I32, (L, L), 0)
    tpos = lax.broadcasted_iota(I32, (L, L), 1)
    first_row = lax.broadcasted_iota(I32, (BF16_ROWS, L), 0) == 0
    dirs = ((gf_ref, qf_ref, kf_ref, vf_ref, hf_ref), (gb_ref, qb_ref, kb_ref, vb_ref, hb_ref))
    for bi in range(gf_ref.shape[0]):
        for d, (g_ref, q_ref, k_ref, v_ref, h_ref) in enumerate(dirs):
            reverse = d == 1
            mask = (spos >= tpos) if reverse else (spos <= tpos)
            g = g_ref[bi]
            ig4 = g[d * 2 * ML_HEADS:d * 2 * ML_HEADS + ML_HEADS]
            b4 = g[d * 2 * ML_HEADS + ML_HEADS:(d + 1) * 2 * ML_HEADS]
            a4 = ig4 - b4
            a_cols = jnp.concatenate([a4, jnp.zeros((L - ML_HEADS, L), F32)], axis=0).T
            for hh in range(ML_HEADS):
                ci = (bi * 2 + d) * ML_HEADS + hh
                b_row, a_row, a_col = b4[hh:hh + 1], a4[hh:hh + 1], a_cols[:, hh:hh + 1]
                b_last = b_row[:, 0:1] if reverse else b_row[:, L - 1:L]
                m = m_ref[ci, 0:1, 0:1]
                sl = slice(hh * dh, (hh + 1) * dh)
                q_t, k, v_t = q_ref[bi, sl, :], k_ref[bi, :, sl], v_ref[bi, sl, :]
                cn = cn_ref[ci]

                dmat = jnp.where(mask, b_row + a_col, -jnp.inf)
                inter = b_row + m
                m_t = jnp.maximum(inter, jnp.max(dmat, axis=0, keepdims=True))
                w_inter = jnp.exp(inter - m_t)
                s = jnp.dot(k, q_t, preferred_element_type=F32) * jnp.exp(dmat - m_t)
                cq = jnp.dot(cn.astype(BF16), q_t, preferred_element_type=F32)
                num = jnp.dot(v_t, s.astype(BF16), preferred_element_type=F32) + w_inter * cq[:dh]
                den = jnp.sum(s, axis=0, keepdims=True) + w_inter * cq[dh:dh + 1]
                h_ref[bi, sl, :] = (num / jnp.maximum(jnp.abs(den), jnp.exp(-m_t))).astype(BF16)

                dec = b_last + a_row
                m_new = jnp.maximum(b_last + m, jnp.max(dec, axis=-1, keepdims=True))
                wk = jnp.exp(dec - m_new)
                keep = jnp.exp(b_last + m - m_new)
                vw = jnp.concatenate([(v_t.astype(F32) * wk).astype(BF16),
                                      jnp.where(first_row, wk, 0.0).astype(BF16)], axis=0)
                cn_ref[ci] = keep * cn + jnp.dot(vw, k, preferred_element_type=F32)
                m_ref[ci] = jnp.broadcast_to(m_new, m_ref.shape[1:])


def _mlstm_scan(gates, q_t, k, v_t):
    B, tt, C = k.shape
    nch = tt // CHUNK
    ng = gates.shape[1]
    nc_ctx = ROW_TILE // CHUNK
    bidx = lambda j: jnp.where(j < nc_ctx, nc_ctx - 1 - j, nch - 1 + nc_ctx - j)
    rows_f = pl.BlockSpec((B, CHUNK, C), lambda j: (0, j, 0))
    rows_b = pl.BlockSpec((B, CHUNK, C), lambda j: (0, bidx(j), 0))
    cols_f = pl.BlockSpec((B, C, CHUNK), lambda j: (0, 0, j))
    cols_b = pl.BlockSpec((B, C, CHUNK), lambda j: (0, 0, bidx(j)))
    nchain = B * 2 * ML_HEADS
    return pl.pallas_call(
        _scan_kernel,
        grid=(nch,),
        in_specs=[pl.BlockSpec((B, ng, CHUNK), lambda j: (0, 0, j)),
                  pl.BlockSpec((B, ng, CHUNK), lambda j: (0, 0, bidx(j))),
                  cols_f, rows_f, cols_f, cols_b, rows_b, cols_b],
        out_specs=[cols_f, cols_b],
        out_shape=[jax.ShapeDtypeStruct((B, C, tt), BF16), jax.ShapeDtypeStruct((B, C, tt), BF16)],
        scratch_shapes=[pltpu.VMEM((nchain, ML_HEAD_DIM + BF16_ROWS, ML_HEAD_DIM), F32),
                        pltpu.VMEM((nchain, SUBLANES, LANES), F32)],
        compiler_params=_params(("arbitrary",)),
        name="mlstm_scan",
    )(gates, gates, q_t, k, v_t, q_t, k, v_t)


def _merge_kernel(hf_ref, hb_ref, z_ref, xc_ref, gmla_ref, gml_ref, ymla_ref, x_ref,
                  g1_ref, sh2_ref, sc2_ref, mln_ref, mls_ref, wout_ref, n2_ref, wr_ref,
                  x1_ref, h2_ref, aff_ref):
    h_t = hf_ref[0].astype(F32) + hb_ref[0].astype(F32)
    parts = []
    for hh in range(ML_HEADS):
        seg = h_t[hh * ML_HEAD_DIM:(hh + 1) * ML_HEAD_DIM]
        parts.append((seg * lax.rsqrt(jnp.mean(seg * seg, axis=0, keepdims=True) + EPS)).T)
    hn = jnp.concatenate(parts, axis=-1) * mln_ref[...]
    y_ml = _sigmoid(z_ref[0].astype(F32)) * (hn + mls_ref[...] * xc_ref[0].astype(F32))
    merged = (_sigmoid(gmla_ref[0].astype(F32)) * ymla_ref[0].astype(F32)
              + _sigmoid(gml_ref[0].astype(F32)) * y_ml)
    out = jnp.dot(merged.astype(BF16), wout_ref[...], preferred_element_type=F32)
    x1 = x_ref[0] + g1_ref[0] * out
    x1_ref[0] = x1
    h2 = _rms(x1, n2_ref[...]) * (1.0 + sc2_ref[0]) + sh2_ref[0]
    h2_hi = h2.astype(BF16)
    h2_ref[0] = h2_hi
    h2_lo = (h2 - h2_hi.astype(F32)).astype(BF16)
    wr = wr_ref[...]
    wr_hi = wr.astype(BF16)
    wr_lo = (wr - wr_hi.astype(F32)).astype(BF16)
    logits = (lax.dot_general(wr_hi, h2_hi, _NT, preferred_element_type=F32)
              + lax.dot_general(wr_hi, h2_lo, _NT, preferred_element_type=F32)
              + lax.dot_general(wr_lo, h2_hi, _NT, preferred_element_type=F32))
    e = jnp.exp(logits - jnp.max(logits, axis=0, keepdims=True))
    aff_ref[0] = e / jnp.sum(e, axis=0, keepdims=True)


def _merge(hf, hb, z, xc, gmla, gml, ymla, x, mod3, ml_norm, ml_skip, w_out, norm2, w_router_t):
    B, T, D = x.shape
    nt = T // ROW_TILE
    const2 = lambda b, i: (0, 0)
    lat = lambda b, i: (b, i, 0)
    shifted = lambda b, i: (b, i + 1, 0)
    modcol = lambda col: (lambda b, i: (b, 0, col))
    tile = lambda idx: pl.BlockSpec((1, ROW_TILE, D), idx)
    h_tile = pl.BlockSpec((1, ML_INNER, ROW_TILE), lambda b, i: (b, 0, i + 1))
    return pl.pallas_call(
        _merge_kernel,
        grid=(B, nt),
        in_specs=[h_tile, h_tile, tile(lat), tile(shifted), tile(lat), tile(lat), tile(lat),
                  tile(lat),
                  pl.BlockSpec((1, 1, D), modcol(2)), pl.BlockSpec((1, 1, D), modcol(3)),
                  pl.BlockSpec((1, 1, D), modcol(4)),
                  pl.BlockSpec((1, D), const2), pl.BlockSpec((1, D), const2),
                  pl.BlockSpec((D, D), const2), pl.BlockSpec((1, D), const2),
                  pl.BlockSpec((N_EXPERTS, D), const2)],
        out_specs=[tile(lat), tile(lat), pl.BlockSpec((1, N_EXPERTS, ROW_TILE), lambda b, i: (b, 0, i))],
        out_shape=[jax.ShapeDtypeStruct((B, T, D), F32),
                   jax.ShapeDtypeStruct((B, T, D), BF16),
                   jax.ShapeDtypeStruct((B, N_EXPERTS, T), F32)],
        compiler_params=_params(("arbitrary", "arbitrary")),
        name="merge_router",
    )(hf, hb, z, xc, gmla, gml, ymla, x, mod3, mod3, mod3, ml_norm, ml_skip, w_out, norm2, w_router_t)


def _chunked_cumsum(mask_f, tri):
    n_e, t = mask_f.shape
    off = jnp.zeros((n_e, 1), F32)
    outs = []
    for c in range(t // LANES):
        x = mask_f[:, c * LANES:(c + 1) * LANES]
        inc = jnp.dot(x.astype(BF16), tri, preferred_element_type=F32)
        outs.append(inc - x + off)
        off = off + inc[:, LANES - 1:LANES]
    return jnp.concatenate(outs, axis=1)


def _select_kernel(aff_ref, pos_ref, cs_ref, *, cap):
    aff = aff_ref[0]
    n_e = aff.shape[0]

    def count_ge(t):
        return jnp.sum(jnp.where(aff >= t, 1.0, 0.0), axis=1, keepdims=True)

    def body(carry):
        lo, hi, _ = carry
        mid = 0.5 * (lo + hi)
        ok = count_ge(mid) >= cap
        lo, hi = jnp.where(ok, mid, lo), jnp.where(ok, hi, mid)
        mid = 0.5 * (lo + hi)
        return lo, hi, jnp.max(jnp.where((mid > lo) & (mid < hi), 1.0, 0.0))

    lo, hi, _ = lax.while_loop(lambda carry: carry[2] > 0.5, body,
                               (jnp.zeros((n_e, 1), F32), jnp.full((n_e, 1), 2.0, F32), jnp.float32(1.0)))
    gt = jnp.where(aff >= hi, 1.0, 0.0)
    eq = jnp.where(aff >= lo, 1.0, 0.0) - gt
    need = cap - jnp.sum(gt, axis=1, keepdims=True)
    tri = jnp.where(lax.broadcasted_iota(I32, (LANES, LANES), 0) <= lax.broadcasted_iota(I32, (LANES, LANES), 1),
                    1.0, 0.0).astype(BF16)
    eq_rank = _chunked_cumsum(eq, tri)
    sel = gt + eq * jnp.where(eq_rank < need, 1.0, 0.0)
    cs = _chunked_cumsum(sel, tri)
    cs_ref[0] = cs.astype(I32)
    pos_ref[0] = jnp.where(sel > 0.5, cs, -1.0).astype(I32)


def _select(aff_t, cap):
    B, n_e, T = aff_t.shape
    blk = pl.BlockSpec((1, n_e, T), lambda b: (b, 0, 0))
    return pl.pallas_call(
        functools.partial(_select_kernel, cap=cap),
        grid=(B,),
        in_specs=[blk],
        out_specs=[blk, blk],
        out_shape=[jax.ShapeDtypeStruct((B, n_e, T), I32), jax.ShapeDtypeStruct((B, n_e, T), I32)],
        compiler_params=_params(("arbitrary",)),
        name="expert_select",
    )(aff_t)


def _window_start(base_ref, flat):
    start = base_ref[flat]
    return pl.multiple_of((start // BF16_ROWS) * BF16_ROWS, BF16_ROWS)


def _expert_kernel(base_ref, x_ref, pos_ref, g_ref, wg_ref, wu_ref, wd_ref, y_ref, xs_ref, gs_ref, *, n_tiles, cap):
    b, e = pl.program_id(0), pl.program_id(1)
    head = BF16_ROWS
    xs_ref[0:head, :] = jnp.zeros((head, xs_ref.shape[1]), BF16)
    gs_ref[0:head, :] = jnp.zeros((head, 1), F32)
    row_id = lax.broadcasted_iota(I32, (MOE_WIN, MOE_TILE), 0)

    def gather(k, carry):
        s16 = _window_start(base_ref, (b * N_EXPERTS + e) * (n_tiles + 1) + k)
        rel = pos_ref[0, pl.ds(k, 1), :] - s16
        hit = row_id == rel
        onehot = jnp.where(hit, 1.0, 0.0).astype(BF16)
        tok = pl.ds(pl.multiple_of(k * MOE_TILE, MOE_TILE), MOE_TILE)
        rows = jnp.dot(onehot, x_ref[0, tok, :], preferred_element_type=F32)
        first = xs_ref[pl.ds(s16, head), :].astype(F32) + rows[:head]
        xs_ref[pl.ds(s16 + head, MOE_WIN - head), :] = rows[head:].astype(BF16)
        xs_ref[pl.ds(s16, head), :] = first.astype(BF16)
        gate = jnp.sum(jnp.where(hit, g_ref[0, pl.ds(k, 1), :], 0.0), axis=1, keepdims=True)
        first_gate = gs_ref[pl.ds(s16, head), :] + gate[:head]
        gs_ref[pl.ds(s16 + head, MOE_WIN - head), :] = gate[head:]
        gs_ref[pl.ds(s16, head), :] = first_gate
        return carry

    lax.fori_loop(0, n_tiles, gather, 0, unroll=4)

    for r in range(cap // ROW_TILE):
        rows = slice(r * ROW_TILE, (r + 1) * ROW_TILE)
        xs = xs_ref[rows, :]
        a = jnp.dot(xs, wg_ref[0], preferred_element_type=F32)
        u = jnp.dot(xs, wu_ref[0], preferred_element_type=F32)
        hm = (a * _sigmoid(a) * u).astype(BF16)
        y_ref[0, 0, rows, :] = (jnp.dot(hm, wd_ref[0], preferred_element_type=F32) * gs_ref[rows, :]).astype(BF16)
    y_ref[0, 0, cap:, :] = jnp.zeros((y_ref.shape[2] - cap, y_ref.shape[3]), BF16)


def _experts(base, h2, pos3, g3, wg, wu, wd, cap):
    B, T, D = h2.shape
    nk = T // MOE_TILE
    yr = cap + MOE_WIN
    ff = wg.shape[2]
    wspec = lambda shape: pl.BlockSpec((1,) + shape, lambda b, e, base: (e, 0, 0))
    grid_spec = pltpu.PrefetchScalarGridSpec(
        num_scalar_prefetch=1,
        grid=(B, N_EXPERTS),
        in_specs=[pl.BlockSpec((1, T, D), lambda b, e, base: (b, 0, 0), pipeline_mode=pl.Buffered(1)),
                  pl.BlockSpec((1, nk, MOE_TILE), lambda b, e, base: (b * N_EXPERTS + e, 0, 0)),
                  pl.BlockSpec((1, nk, MOE_TILE), lambda b, e, base: (b * N_EXPERTS + e, 0, 0)),
                  wspec((D, ff)), wspec((D, ff)), wspec((ff, D))],
        out_specs=pl.BlockSpec((1, 1, yr, D), lambda b, e, base: (b, e, 0, 0)),
        scratch_shapes=[pltpu.VMEM((yr, D), BF16), pltpu.VMEM((yr, 1), F32)])
    return pl.pallas_call(
        functools.partial(_expert_kernel, n_tiles=nk, cap=cap),
        grid_spec=grid_spec,
        out_shape=jax.ShapeDtypeStruct((B, N_EXPERTS, yr, D), BF16),
        compiler_params=_params(("arbitrary", "arbitrary")),
        name="expert_ffn",
    )(base, h2, pos3, g3, wg, wu, wd)


def _combine_kernel(base_ref, y_ref, pos_ref, x1_ref, g2_ref, fn_ref, o_ref, *, n_tiles, tiles_per_blk):
    b, tb, e = pl.program_id(0), pl.program_id(1), pl.program_id(2)
    flat0 = (b * N_EXPERTS + e) * (n_tiles + 1) + tb * tiles_per_blk
    pick = lax.broadcasted_iota(I32, (MOE_TILE, N_EXPERTS), 1) == e
    col_id = lax.broadcasted_iota(I32, (MOE_TILE, MOE_TILE), 1).astype(F32)

    @pl.when(e == 0)
    def _():
        o_ref[...] = jnp.zeros_like(o_ref)

    def column(k):
        rows = slice(k * MOE_TILE, (k + 1) * MOE_TILE)
        pcol = jnp.sum(jnp.where(pick, pos_ref[0, rows, :].astype(F32), 0.0), axis=1, keepdims=True)
        return rows, pcol

    for k in range(tiles_per_blk):
        s16 = _window_start(base_ref, flat0 + k)
        rows, pcol = column(k)
        onehot = jnp.where(col_id == pcol - s16.astype(F32), 1.0, 0.0).astype(BF16)
        o_ref[0, rows, :] += jnp.dot(onehot, y_ref[0, 0, pl.ds(s16, MOE_TILE), :], preferred_element_type=F32)

    for k in range(tiles_per_blk):
        s16 = _window_start(base_ref, flat0 + k)

        @pl.when(base_ref[flat0 + k + 1] > s16 + MOE_TILE)
        def _():
            rows, pcol = column(k)
            tail = lax.broadcasted_iota(I32, (MOE_TILE, BF16_ROWS), 1).astype(F32) + float(MOE_TILE)
            onehot = jnp.where(tail == pcol - s16.astype(F32), 1.0, 0.0).astype(BF16)
            o_ref[0, rows, :] += jnp.dot(onehot, y_ref[0, 0, pl.ds(s16 + MOE_TILE, BF16_ROWS), :],
                                         preferred_element_type=F32)

    @pl.when(e == N_EXPERTS - 1)
    def _():
        for k in range(tiles_per_blk):
            rows = slice(k * MOE_TILE, (k + 1) * MOE_TILE)
            x2 = x1_ref[0, rows, :] + g2_ref[0] * o_ref[0, rows, :]
            o_ref[0, rows, :] = _rms(x2, fn_ref[...])


def _combine(base, y, pos_t, x1, mod3, final_norm):
    B, T, D = x1.shape
    nk = T // MOE_TILE
    per = 8
    blk = per * MOE_TILE
    yr = y.shape[2]
    tok = lambda b, tb, e, base: (b, tb, 0)
    grid_spec = pltpu.PrefetchScalarGridSpec(
        num_scalar_prefetch=1,
        grid=(B, nk // per, N_EXPERTS),
        in_specs=[pl.BlockSpec((1, 1, yr, D), lambda b, tb, e, base: (b, e, 0, 0)),
                  pl.BlockSpec((1, blk, N_EXPERTS), tok),
                  pl.BlockSpec((1, blk, D), tok),
                  pl.BlockSpec((1, 1, D), lambda b, tb, e, base: (b, 0, 5)),
                  pl.BlockSpec((1, D), lambda b, tb, e, base: (0, 0))],
        out_specs=pl.BlockSpec((1, blk, D), tok))
    return pl.pallas_call(
        functools.partial(_combine_kernel, n_tiles=nk, tiles_per_blk=per),
        grid_spec=grid_spec,
        out_shape=jax.ShapeDtypeStruct((B, T, D), F32),
        compiler_params=_params(("arbitrary",) * 3),
        name="moe_combine",
    )(base, y, pos_t, x1, mod3, final_norm)


def _rope_tables(T):
    rows = T // GRID_W
    row = np.repeat(np.arange(rows, dtype=np.float64), GRID_W)
    col = np.tile(np.arange(GRID_W, dtype=np.float64), rows)
    inv = ROPE_BASE ** (-np.arange(ROPE_PAIRS, dtype=np.float64) / ROPE_PAIRS)
    ar, ac = row[:, None] * inv, col[:, None] * inv
    ones = np.ones((T, LANES - QK_ROPE))
    cos = np.concatenate([np.cos(ar), np.cos(ar), np.cos(ac), np.cos(ac), ones], axis=1)
    sin = np.concatenate([-np.sin(ar), np.sin(ar), -np.sin(ac), np.sin(ac), 0.0 * ones], axis=1)
    cos = np.concatenate([np.ones((ROW_TILE, LANES)), cos], axis=0)
    sin = np.concatenate([np.zeros((ROW_TILE, LANES)), sin], axis=0)
    return jnp.asarray(cos, F32), jnp.asarray(sin, F32)


def _blockdiag_dense(w):
    n, bs, _ = w.shape
    rows = jnp.broadcast_to(w.transpose(1, 0, 2).reshape(1, bs, n * bs), (n, bs, n * bs)).reshape(n * bs, n * bs)
    r = jnp.arange(n * bs) // bs
    return jnp.where(r[:, None] == r[None, :], rows, 0.0).astype(BF16)


def kernel(x, c, ctx, c_ctx, w_mod, b_mod, norm1, w_in, q_norm, w_uq, kv_norm, w_ukv, conv_w, conv_b,
           w_qblk, w_kblk, w_vblk, w_gate, b_gate, ml_norm, ml_skip, w_out, norm2, w_router,
           w_e_gate, w_e_up, w_e_down, final_norm):
    B, T, D = x.shape
    assert w_mod.shape[0] == 1 and D == D_MODEL and ctx.shape[1] == ROW_TILE
    cap = CAP_FACTOR * T // N_EXPERTS

    cc = jnp.zeros((SUBLANES, D), F32).at[:B].set(c).at[B].set(c_ctx)
    mod = _modulation(cc, w_mod[0], b_mod[0].reshape(1, -1))
    mod3 = mod[:B + 1].reshape(B + 1, 1, 6 * D)

    wi = w_in[0]
    zpad = jnp.zeros((D, LANES - QK_ROPE), F32)
    w_in_p = jnp.concatenate([wi[:, :_C_KROPE + QK_ROPE], zpad, wi[:, _C_KROPE + QK_ROPE:]], axis=1).astype(BF16)
    wuq = w_uq[0].reshape(Q_LORA, MLA_HEADS, QK_NOPE + QK_ROPE)
    wuq_nope = wuq[:, :, :QK_NOPE].reshape(Q_LORA, MLA_HEADS * QK_NOPE).astype(BF16)
    wuq_pe = jnp.pad(wuq[:, :, QK_NOPE:], ((0, 0), (0, 0), (0, LANES - QK_ROPE))).reshape(
        Q_LORA, MLA_HEADS * LANES).astype(BF16)
    wukv = w_ukv[0].reshape(KV_LORA, MLA_HEADS, QK_NOPE + V_DIM)
    wukv_p = jnp.concatenate([wukv[:, :, :QK_NOPE].reshape(KV_LORA, -1),
                              wukv[:, :, QK_NOPE:].reshape(KV_LORA, -1)], axis=1).astype(BF16)
    cos_t, sin_t = _rope_tables(T)

    q, k, v, xm, z, gmla, gml = _input_projection(
        ctx, x, mod3, norm1, w_in_p, q_norm, wuq_nope, wuq_pe, kv_norm, wukv_p, cos_t, sin_t)

    conv_w8 = jnp.zeros((SUBLANES, ML_INNER), F32).at[:CONV_W].set(conv_w[0])
    wg3 = w_gate[0].reshape(3, ML_INNER, 4 * ML_HEADS).transpose(0, 2, 1).astype(BF16)
    mq, mk, mv, xc, gates = _mlstm_features(
        xm, conv_w8, conv_b, _blockdiag_dense(w_qblk[0].transpose(0, 2, 1)), _blockdiag_dense(w_kblk[0]),
        _blockdiag_dense(w_vblk[0].transpose(0, 2, 1)), wg3, b_gate[0].reshape(-1, 1))

    y_mla = _attention(q, k, v)
    hf, hb = _mlstm_scan(gates, mq, mk, mv)

    x1, h2, aff_t = _merge(hf, hb, z, xc, gmla, gml, y_mla, x, mod3, ml_norm, ml_skip,
                           w_out[0].astype(BF16), norm2, w_router[0].T)

    pos, cs = _select(aff_t, cap)
    base = jnp.concatenate([cs[:, :, ::MOE_TILE], jnp.full((B, N_EXPERTS, 1), cap, I32)], axis=2).reshape(-1)
    tiled = (B * N_EXPERTS, T // MOE_TILE, MOE_TILE)
    y = _experts(base, h2, pos.reshape(tiled), aff_t.reshape(tiled), w_e_gate[0].astype(BF16),
                 w_e_up[0].astype(BF16), w_e_down[0].astype(BF16), cap)
    return _combine(base, y, pos.transpose(0, 2, 1), x1, mod3, final_norm.reshape(1, -1))
```

```python
import functools

import numpy as np
import jax
import jax.numpy as jnp
from jax import lax
from jax.experimental import pallas as pl
from jax.experimental.pallas import tpu as pltpu

F32 = jnp.float32
BF16 = jnp.bfloat16
I32 = jnp.int32

D_MODEL = 1024
GRID_W = 64
MLA_HEADS = 8
QK_NOPE = 128
QK_ROPE = 64
V_DIM = 128
Q_LORA = 384
KV_LORA = 256
ROPE_BASE = 10000.0
ROPE_PAIRS = QK_ROPE // 4
ATTN_SCALE = (QK_NOPE + QK_ROPE) ** -0.5
Q_SCALE = ATTN_SCALE * 1.4426950408889634
ML_HEADS = 4
ML_INNER = 1024
ML_HEAD_DIM = ML_INNER // ML_HEADS
QKV_BLOCK = 4
CONV_W = 5
CHUNK = 128
N_EXPERTS = 16
EXPERT_FF = 1024
CAP_FACTOR = 2
EPS = 1e-6

LANES = 128
SUBLANES = 8
BF16_ROWS = 16
ROW_TILE = 256
HEAD_W = 256
ATTN_TQ = 256
ATTN_SUB = 2
ATTN_TK = 768
MOE_TILE = 256
MOE_WIN = MOE_TILE + BF16_ROWS
VMEM_LIMIT = 56 * 1024 * 1024

_C_QLAT = 0
_C_KVLAT = Q_LORA
_C_KROPE = Q_LORA + KV_LORA
_C_XM = _C_KROPE + LANES
_C_Z = _C_XM + ML_INNER
_C_GMLA = _C_Z + ML_INNER
_C_GML = _C_GMLA + D_MODEL
IN_PAD = _C_GML + D_MODEL

_NT = (((1,), (1,)), ((), ()))
_TN = (((0,), (0,)), ((), ()))


def _params(sem, vmem=VMEM_LIMIT):
    return pltpu.CompilerParams(dimension_semantics=sem, vmem_limit_bytes=vmem)


def _rms(x, g):
    return x * lax.rsqrt(jnp.mean(x * x, axis=-1, keepdims=True) + EPS) * g


def _sigmoid(x):
    return jax.nn.sigmoid(x)


def _mod_kernel(c_ref, w_ref, b_ref, o_ref):
    c = c_ref[...]
    s = c * _sigmoid(c)
    o_ref[...] = jnp.dot(s, w_ref[...], preferred_element_type=F32,
                         precision=lax.Precision.HIGHEST) + b_ref[...]


def _modulation(cc, w_mod, b_mod):
    n = w_mod.shape[1]
    tn = 768
    return pl.pallas_call(
        _mod_kernel,
        grid=(n // tn,),
        in_specs=[pl.BlockSpec((SUBLANES, D_MODEL), lambda j: (0, 0)),
                  pl.BlockSpec((D_MODEL, tn), lambda j: (0, j)),
                  pl.BlockSpec((1, tn), lambda j: (0, j))],
        out_specs=pl.BlockSpec((SUBLANES, tn), lambda j: (0, j)),
        out_shape=jax.ShapeDtypeStruct((SUBLANES, n), F32),
        compiler_params=_params(("arbitrary",)),
        name="modulation",
    )(cc, w_mod, b_mod)


def _rope(v, cos, sin):
    lane = lax.broadcasted_iota(I32, v.shape, 1)
    partner = jnp.where(lane % 32 < 16, pltpu.roll(v, LANES - 16, 1), pltpu.roll(v, 16, 1))
    return v * cos + partner * sin


def _inproj_kernel(ctx_ref, x_ref, sh_ref, sc_ref, n1_ref, win_ref, qn_ref, wuqn_ref, wuqp_ref,
                   kvn_ref, wukv_ref, cos_ref, sin_ref,
                   q_ref, k_ref, v_ref, xm_ref, z_ref, gmla_ref, gml_ref):
    i = pl.program_id(1)
    xin = jnp.where(i == 0, ctx_ref[0], x_ref[0])
    h = _rms(xin, n1_ref[...]) * (1.0 + sc_ref[0]) + sh_ref[0]
    big = jnp.dot(h.astype(BF16), win_ref[...], preferred_element_type=F32)
    xm_ref[0] = big[:, _C_XM:_C_Z]
    z_ref[0] = big[:, _C_Z:_C_GMLA].astype(BF16)
    gmla_ref[0] = big[:, _C_GMLA:_C_GML].astype(BF16)
    gml_ref[0] = big[:, _C_GML:IN_PAD].astype(BF16)

    cos = cos_ref[...]
    sin = sin_ref[...]
    qn = _rms(big[:, _C_QLAT:_C_KVLAT], qn_ref[...]).astype(BF16)
    q_nope = jnp.dot(qn, wuqn_ref[...], preferred_element_type=F32)
    q_pe = jnp.dot(qn, wuqp_ref[...], preferred_element_type=F32)
    kvn = _rms(big[:, _C_KVLAT:_C_KROPE], kvn_ref[...]).astype(BF16)
    kv = jnp.dot(kvn, wukv_ref[...], preferred_element_type=F32)
    k_pe = _rope(big[:, _C_KROPE:_C_XM], cos, sin).astype(BF16)
    ones_col = jnp.where(lax.broadcasted_iota(I32, (big.shape[0], HEAD_W - V_DIM), 1) == 0, 1.0, 0.0).astype(BF16)
    for hh in range(MLA_HEADS):
        lo = hh * HEAD_W
        nope = slice(hh * QK_NOPE, (hh + 1) * QK_NOPE)
        q_ref[0, :, lo:lo + QK_NOPE] = (q_nope[:, nope] * Q_SCALE).astype(BF16)
        q_ref[0, :, lo + QK_NOPE:lo + HEAD_W] = (
            _rope(q_pe[:, hh * LANES:(hh + 1) * LANES], cos, sin) * Q_SCALE).astype(BF16)
        k_ref[0, :, lo:lo + QK_NOPE] = kv[:, nope].astype(BF16)
        k_ref[0, :, lo + QK_NOPE:lo + HEAD_W] = k_pe
        v_ref[0, :, lo:lo + V_DIM] = kv[:, MLA_HEADS * QK_NOPE + hh * V_DIM:MLA_HEADS * QK_NOPE + (hh + 1) * V_DIM].astype(BF16)
        v_ref[0, :, lo + V_DIM:lo + HEAD_W] = ones_col


def _input_projection(ctx, x, mod3, norm1, w_in_p, q_norm, wuq_nope, wuq_pe, kv_norm, wukv_p, cos_t, sin_t):
    B, T, D = x.shape
    n_lat = T // ROW_TILE
    nt = n_lat + 1
    tt = T + ROW_TILE
    const2 = lambda b, i: (0, 0)
    lat = lambda b, i: (b, jnp.maximum(i - 1, 0), 0)
    allrows = lambda b, i: (b, i, 0)
    modrow = lambda col: (lambda b, i: (jnp.where(i == 0, B, b), 0, col))
    return pl.pallas_call(
        _inproj_kernel,
        grid=(B, nt),
        in_specs=[pl.BlockSpec((1, ROW_TILE, D), lambda b, i: (b, 0, 0)),
                  pl.BlockSpec((1, ROW_TILE, D), lat),
                  pl.BlockSpec((1, 1, D), modrow(0)),
                  pl.BlockSpec((1, 1, D), modrow(1)),
                  pl.BlockSpec((1, D), const2),
                  pl.BlockSpec(w_in_p.shape, const2),
                  pl.BlockSpec((1, Q_LORA), const2),
                  pl.BlockSpec(wuq_nope.shape, const2),
                  pl.BlockSpec(wuq_pe.shape, const2),
                  pl.BlockSpec((1, KV_LORA), const2),
                  pl.BlockSpec(wukv_p.shape, const2),
                  pl.BlockSpec((ROW_TILE, LANES), lambda b, i: (i, 0)),
                  pl.BlockSpec((ROW_TILE, LANES), lambda b, i: (i, 0))],
        out_specs=[pl.BlockSpec((1, ROW_TILE, MLA_HEADS * HEAD_W), lat),
                   pl.BlockSpec((1, ROW_TILE, MLA_HEADS * HEAD_W), allrows),
                   pl.BlockSpec((1, ROW_TILE, MLA_HEADS * HEAD_W), allrows),
                   pl.BlockSpec((1, ROW_TILE, ML_INNER), allrows),
                   pl.BlockSpec((1, ROW_TILE, ML_INNER), lat),
                   pl.BlockSpec((1, ROW_TILE, D), lat),
                   pl.BlockSpec((1, ROW_TILE, D), lat)],
        out_shape=[jax.ShapeDtypeStruct((B, T, MLA_HEADS * HEAD_W), BF16),
                   jax.ShapeDtypeStruct((B, tt, MLA_HEADS * HEAD_W), BF16),
                   jax.ShapeDtypeStruct((B, tt, MLA_HEADS * HEAD_W), BF16),
                   jax.ShapeDtypeStruct((B, tt, ML_INNER), F32),
                   jax.ShapeDtypeStruct((B, T, ML_INNER), BF16),
                   jax.ShapeDtypeStruct((B, T, D), BF16),
                   jax.ShapeDtypeStruct((B, T, D), BF16)],
        compiler_params=_params(("arbitrary", "arbitrary")),
        name="input_projection",
    )(ctx, x, mod3, mod3, norm1, w_in_p, q_norm, wuq_nope, wuq_pe, kv_norm, wukv_p, cos_t, sin_t)


def _feat_kernel(prev_ref, cur_ref, next_ref, cw_ref, cb_ref, wq_ref, wk_ref, wv_ref, wg_ref, bg_ref,
                 q_ref, k_ref, v_ref, xc_ref, g_ref, *, n_tiles):
    i = pl.program_id(1)
    cur = cur_ref[0]
    prev = jnp.where(i <= 1, 0.0, prev_ref[0])
    nxt = jnp.where((i == 0) | (i == n_tiles - 1), 0.0, next_ref[0])
    xx = jnp.concatenate([prev, cur, nxt], axis=0)
    rows = cur.shape[0]
    acc = jnp.broadcast_to(cb_ref[...], cur.shape)
    for w in range(CONV_W):
        lo = SUBLANES - CONV_W // 2 + w
        acc = acc + xx[lo:lo + rows] * cw_ref[w:w + 1, :]
    xc = acc * _sigmoid(acc)
    xc_ref[0] = xc.astype(BF16)
    xcb = xc.astype(BF16)
    q_t = lax.dot_general(wq_ref[...], xcb, _NT, preferred_element_type=F32)
    k = jnp.dot(xcb, wk_ref[...], preferred_element_type=F32)
    v_t = lax.dot_general(wv_ref[...], cur.astype(BF16), _NT, preferred_element_type=F32)
    qb, kb, vb = q_t.astype(BF16), k.astype(BF16), v_t.astype(BF16)
    q_ref[0] = qb
    k_ref[0] = (k * (ML_HEAD_DIM ** -0.5)).astype(BF16)
    v_ref[0] = vb
    g = (jnp.dot(wg_ref[0], qb, preferred_element_type=F32)
         + lax.dot_general(wg_ref[1], kb, _NT, preferred_element_type=F32)
         + jnp.dot(wg_ref[2], vb, preferred_element_type=F32)) + bg_ref[...]
    row = lax.broadcasted_iota(I32, (g.shape[0], CHUNK), 0)
    fwd_forget = (row >= ML_HEADS) & (row < 2 * ML_HEADS)
    bwd_forget = row >= 3 * ML_HEADS
    src = lax.broadcasted_iota(I32, (CHUNK, CHUNK), 0)
    dst = lax.broadcasted_iota(I32, (CHUNK, CHUNK), 1)
    tri_prefix = jnp.where(src <= dst, 1.0, 0.0).astype(BF16)
    tri_suffix = jnp.where(src >= dst, 1.0, 0.0).astype(BF16)
    for c in range(g.shape[1] // CHUNK):
        gc = g[:, c * CHUNK:(c + 1) * CHUNK]
        lf = jax.nn.log_sigmoid(gc)
        lf_hi = lf.astype(BF16)
        lf_lo = (lf - lf_hi.astype(F32)).astype(BF16)
        prefix = (jnp.dot(lf_hi, tri_prefix, preferred_element_type=F32)
                  + jnp.dot(lf_lo, tri_prefix, preferred_element_type=F32))
        suffix = (jnp.dot(lf_hi, tri_suffix, preferred_element_type=F32)
                  + jnp.dot(lf_lo, tri_suffix, preferred_element_type=F32))
        g_ref[0, :, c * CHUNK:(c + 1) * CHUNK] = jnp.where(fwd_forget, prefix, jnp.where(bwd_forget, suffix, gc))


def _mlstm_features(xm, conv_w8, conv_b, wq_bd_t, wk_bd, wv_bd_t, wg3, bg_col):
    B, tt, C = xm.shape
    nt = tt // ROW_TILE
    per = ROW_TILE // SUBLANES
    last8 = tt // SUBLANES - 1
    const2 = lambda b, i: (0, 0)
    rows = lambda b, i: (b, i, 0)
    cols = lambda b, i: (b, 0, i)
    ng = 4 * ML_HEADS
    return pl.pallas_call(
        functools.partial(_feat_kernel, n_tiles=nt),
        grid=(B, nt),
        in_specs=[pl.BlockSpec((1, SUBLANES, C), lambda b, i: (b, jnp.maximum(i * per - 1, 0), 0)),
                  pl.BlockSpec((1, ROW_TILE, C), rows),
                  pl.BlockSpec((1, SUBLANES, C), lambda b, i: (b, jnp.minimum((i + 1) * per, last8), 0)),
                  pl.BlockSpec((SUBLANES, C), const2),
                  pl.BlockSpec((1, C), const2),
                  pl.BlockSpec((C, C), const2),
                  pl.BlockSpec((C, C), const2),
                  pl.BlockSpec((C, C), const2),
                  pl.BlockSpec((3, ng, C), lambda b, i: (0, 0, 0)),
                  pl.BlockSpec((ng, 1), const2)],
        out_specs=[pl.BlockSpec((1, C, ROW_TILE), cols),
                   pl.BlockSpec((1, ROW_TILE, C), rows),
                   pl.BlockSpec((1, C, ROW_TILE), cols),
                   pl.BlockSpec((1, ROW_TILE, C), rows),
                   pl.BlockSpec((1, ng, ROW_TILE), cols)],
        out_shape=[jax.ShapeDtypeStruct((B, C, tt), BF16),
                   jax.ShapeDtypeStruct((B, tt, C), BF16),
                   jax.ShapeDtypeStruct((B, C, tt), BF16),
                   jax.ShapeDtypeStruct((B, tt, C), BF16),
                   jax.ShapeDtypeStruct((B, ng, tt), F32)],
        compiler_params=_params(("arbitrary", "arbitrary")),
        name="mlstm_features",
    )(xm, xm, xm, conv_w8, conv_b, wq_bd_t, wk_bd, wv_bd_t, wg3, bg_col)


def _attn_kernel(q_ref, k_ref, v_ref, o_ref, sa_ref, sb_ref, pa_ref, pb_ref, mb_ref, *, n_chunks):
    j = pl.program_id(0)
    tq = ATTN_TQ

    @pl.when(j == 0)
    def _():
        sb_ref[...] = jnp.zeros_like(sb_ref)
        pa_ref[...] = jnp.ones_like(pa_ref)
        mb_ref[...] = jnp.zeros_like(mb_ref)

    def half(q, s_new, s_old, m_old, p_new, p_old):
        m = jnp.full((tq, 1), -jnp.inf, F32)
        acc = jnp.zeros((tq, HEAD_W), F32)
        for c in range(n_chunks):
            keys = slice(c * ATTN_TK, (c + 1) * ATTN_TK)
            s = lax.dot_general(q, k_ref[0, keys, :], _NT, preferred_element_type=F32)
            s_new[c] = s
            m = jnp.maximum(m, jnp.max(s, axis=-1, keepdims=True))
            p_new[c] = jnp.exp2(s_old[c] - m_old).astype(BF16)
            acc = acc + jnp.dot(p_old[c], v_ref[0, keys, :], preferred_element_type=F32)
        return m, (acc[:, :V_DIM] / acc[:, V_DIM:V_DIM + 1]).astype(BF16)

    m_a, out = half(q_ref[0, :tq], sa_ref, sb_ref, mb_ref[...], pb_ref, pa_ref)
    o_ref[0, :tq] = out
    m_b, out = half(q_ref[0, tq:], sb_ref, sa_ref, m_a, pa_ref, pb_ref)
    o_ref[0, tq:] = out
    mb_ref[...] = m_b


def _attention(q, k, v):
    B, T, _ = q.shape
    tt = k.shape[1]
    n_chunks = tt // ATTN_TK
    blk = ATTN_SUB * ATTN_TQ
    n_blk = T // blk
    n_total = B * MLA_HEADS * n_blk
    assert n_chunks * ATTN_TK == tt and n_blk * blk == T and ATTN_SUB == 2

    def block(t):
        t = jnp.clip(t, 0, n_total - 1)
        bh = t // n_blk
        return bh // MLA_HEADS, t % n_blk, bh % MLA_HEADS

    def q_idx(j):
        b, i, h = block(j)
        return b, i, h

    def k_idx(j):
        b, _, h = block(j)
        return b, 0, h

    def v_idx(j):
        b, _, h = block(j - 1)
        return b, 0, h

    def o_idx(j):
        b, i, h = block(j - 1)
        return b, i, h

    buf = lambda dt: pltpu.VMEM((n_chunks, ATTN_TQ, ATTN_TK), dt)
    return pl.pallas_call(
        functools.partial(_attn_kernel, n_chunks=n_chunks),
        grid=(n_total + 1,),
        in_specs=[pl.BlockSpec((1, blk, HEAD_W), q_idx),
                  pl.BlockSpec((1, tt, HEAD_W), k_idx),
                  pl.BlockSpec((1, tt, HEAD_W), v_idx)],
        out_specs=pl.BlockSpec((1, blk, V_DIM), o_idx),
        out_shape=jax.ShapeDtypeStruct((B, T, MLA_HEADS * V_DIM), BF16),
        scratch_shapes=[buf(F32), buf(F32), buf(BF16), buf(BF16), pltpu.VMEM((ATTN_TQ, 1), F32)],
        compiler_params=_params(("arbitrary",)),
        name="attention",
    )(q, k, v)


def _scan_kernel(gf_ref, gb_ref, qf_ref, kf_ref, vf_ref, qb_ref, kb_ref, vb_ref,
                 hf_ref, hb_ref, cn_ref, m_ref):
    j = pl.program_id(0)
    L = CHUNK
    dh = ML_HEAD_DIM

    @pl.when(j == 0)
    def _():
        cn_ref[...] = jnp.zeros_like(cn_ref)
        m_ref[...] = jnp.zeros_like(m_ref)

    spos = lax.broadcasted_iota(I32, (L, L), 0)
    tpos = lax.broadcasted_iota(I32, (L, L), 1)
    first_row = lax.broadcasted_iota(I32, (BF16_ROWS, L), 0) == 0
    dirs = ((gf_ref, qf_ref, kf_ref, vf_ref, hf_ref), (gb_ref, qb_ref, kb_ref, vb_ref, hb_ref))
    for bi in range(gf_ref.shape[0]):
        for d, (g_ref, q_ref, k_ref, v_ref, h_ref) in enumerate(dirs):
            reverse = d == 1
            mask = (spos >= tpos) if reverse else (spos <= tpos)
            g = g_ref[bi]
            ig4 = g[d * 2 * ML_HEADS:d * 2 * ML_HEADS + ML_HEADS]
            b4 = g[d * 2 * ML_HEADS + ML_HEADS:(d + 1) * 2 * ML_HEADS]
            a4 = ig4 - b4
            a_cols = jnp.concatenate([a4, jnp.zeros((L - ML_HEADS, L), F32)], axis=0).T
            for hh in range(ML_HEADS):
                ci = (bi * 2 + d) * ML_HEADS + hh
                b_row, a_row, a_col = b4[hh:hh + 1], a4[hh:hh + 1], a_cols[:, hh:hh + 1]
                b_last = b_row[:, 0:1] if reverse else b_row[:, L - 1:L]
                m = m_ref[ci, 0:1, 0:1]
                sl = slice(hh * dh, (hh + 1) * dh)
                q_t, k, v_t = q_ref[bi, sl, :], k_ref[bi, :, sl], v_ref[bi, sl, :]
                cn = cn_ref[ci]

                dmat = jnp.where(mask, b_row + a_col, -jnp.inf)
                inter = b_row + m
                m_t = jnp.maximum(inter, jnp.max(dmat, axis=0, keepdims=True))
                w_inter = jnp.exp(inter - m_t)
                s = jnp.dot(k, q_t, preferred_element_type=F32) * jnp.exp(dmat - m_t)
                cq = jnp.dot(cn.astype(BF16), q_t, preferred_element_type=F32)
                num = jnp.dot(v_t, s.astype(BF16), preferred_element_type=F32) + w_inter * cq[:dh]
                den = jnp.sum(s, axis=0, keepdims=True) + w_inter * cq[dh:dh + 1]
                h_ref[bi, sl, :] = (num / jnp.maximum(jnp.abs(den), jnp.exp(-m_t))).astype(BF16)

                dec = b_last + a_row
                m_new = jnp.maximum(b_last + m, jnp.max(dec, axis=-1, keepdims=True))
                wk = jnp.exp(dec - m_new)
                keep = jnp.exp(b_last + m - m_new)
                vw = jnp.concatenate([(v_t.astype(F32) * wk).astype(BF16),
                                      jnp.where(first_row, wk, 0.0).astype(BF16)], axis=0)
                cn_ref[ci] = keep * cn + jnp.dot(vw, k, preferred_element_type=F32)
                m_ref[ci] = jnp.broadcast_to(m_new, m_ref.shape[1:])


def _mlstm_scan(gates, q_t, k, v_t):
    B, tt, C = k.shape
    nch = tt // CHUNK
    ng = gates.shape[1]
    nc_ctx = ROW_TILE // CHUNK
    bidx = lambda j: jnp.where(j < nc_ctx, nc_ctx - 1 - j, nch - 1 + nc_ctx - j)
    rows_f = pl.BlockSpec((B, CHUNK, C), lambda j: (0, j, 0))
    rows_b = pl.BlockSpec((B, CHUNK, C), lambda j: (0, bidx(j), 0))
    cols_f = pl.BlockSpec((B, C, CHUNK), lambda j: (0, 0, j))
    cols_b = pl.BlockSpec((B, C, CHUNK), lambda j: (0, 0, bidx(j)))
    nchain = B * 2 * ML_HEADS
    return pl.pallas_call(
        _scan_kernel,
        grid=(nch,),
        in_specs=[pl.BlockSpec((B, ng, CHUNK), lambda j: (0, 0, j)),
                  pl.BlockSpec((B, ng, CHUNK), lambda j: (0, 0, bidx(j))),
                  cols_f, rows_f, cols_f, cols_b, rows_b, cols_b],
        out_specs=[cols_f, cols_b],
        out_shape=[jax.ShapeDtypeStruct((B, C, tt), BF16), jax.ShapeDtypeStruct((B, C, tt), BF16)],
        scratch_shapes=[pltpu.VMEM((nchain, ML_HEAD_DIM + BF16_ROWS, ML_HEAD_DIM), F32),
                        pltpu.VMEM((nchain, SUBLANES, LANES), F32)],
        compiler_params=_params(("arbitrary",)),
        name="mlstm_scan",
    )(gates, gates, q_t, k, v_t, q_t, k, v_t)


def _merge_kernel(hf_ref, hb_ref, z_ref, xc_ref, gmla_ref, gml_ref, ymla_ref, x_ref,
                  g1_ref, sh2_ref, sc2_ref, mln_ref, mls_ref, wout_ref, n2_ref, wr_ref,
                  x1_ref, h2_ref, aff_ref):
    h_t = hf_ref[0].astype(F32) + hb_ref[0].astype(F32)
    parts = []
    for hh in range(ML_HEADS):
        seg = h_t[hh * ML_HEAD_DIM:(hh + 1) * ML_HEAD_DIM]
        parts.append((seg * lax.rsqrt(jnp.mean(seg * seg, axis=0, keepdims=True) + EPS)).T)
    hn = jnp.concatenate(parts, axis=-1) * mln_ref[...]
    y_ml = _sigmoid(z_ref[0].astype(F32)) * (hn + mls_ref[...] * xc_ref[0].astype(F32))
    merged = (_sigmoid(gmla_ref[0].astype(F32)) * ymla_ref[0].astype(F32)
              + _sigmoid(gml_ref[0].astype(F32)) * y_ml)
    out = jnp.dot(merged.astype(BF16), wout_ref[...], preferred_element_type=F32)
    x1 = x_ref[0] + g1_ref[0] * out
    x1_ref[0] = x1
    h2 = _rms(x1, n2_ref[...]) * (1.0 + sc2_ref[0]) + sh2_ref[0]
    h2_hi = h2.astype(BF16)
    h2_ref[0] = h2_hi
    h2_lo = (h2 - h2_hi.astype(F32)).astype(BF16)
    wr = wr_ref[...]
    wr_hi = wr.astype(BF16)
    wr_lo = (wr - wr_hi.astype(F32)).astype(BF16)
    logits = (lax.dot_general(wr_hi, h2_hi, _NT, preferred_element_type=F32)
              + lax.dot_general(wr_hi, h2_lo, _NT, preferred_element_type=F32)
              + lax.dot_general(wr_lo, h2_hi, _NT, preferred_element_type=F32))
    e = jnp.exp(logits - jnp.max(logits, axis=0, keepdims=True))
    aff_ref[0] = e / jnp.sum(e, axis=0, keepdims=True)


def _merge(hf, hb, z, xc, gmla, gml, ymla, x, mod3, ml_norm, ml_skip, w_out, norm2, w_router_t):
    B, T, D = x.shape
    nt = T // ROW_TILE
    const2 = lambda b, i: (0, 0)
    lat = lambda b, i: (b, i, 0)
    shifted = lambda b, i: (b, i + 1, 0)
    modcol = lambda col: (lambda b, i: (b, 0, col))
    tile = lambda idx: pl.BlockSpec((1, ROW_TILE, D), idx)
    h_tile = pl.BlockSpec((1, ML_INNER, ROW_TILE), lambda b, i: (b, 0, i + 1))
    return pl.pallas_call(
        _merge_kernel,
        grid=(B, nt),
        in_specs=[h_tile, h_tile, tile(lat), tile(shifted), tile(lat), tile(lat), tile(lat),
                  tile(lat),
                  pl.BlockSpec((1, 1, D), modcol(2)), pl.BlockSpec((1, 1, D), modcol(3)),
                  pl.BlockSpec((1, 1, D), modcol(4)),
                  pl.BlockSpec((1, D), const2), pl.BlockSpec((1, D), const2),
                  pl.BlockSpec((D, D), const2), pl.BlockSpec((1, D), const2),
                  pl.BlockSpec((N_EXPERTS, D), const2)],
        out_specs=[tile(lat), tile(lat), pl.BlockSpec((1, N_EXPERTS, ROW_TILE), lambda b, i: (b, 0, i))],
        out_shape=[jax.ShapeDtypeStruct((B, T, D), F32),
                   jax.ShapeDtypeStruct((B, T, D), BF16),
                   jax.ShapeDtypeStruct((B, N_EXPERTS, T), F32)],
        compiler_params=_params(("arbitrary", "arbitrary")),
        name="merge_router",
    )(hf, hb, z, xc, gmla, gml, ymla, x, mod3, mod3, mod3, ml_norm, ml_skip, w_out, norm2, w_router_t)


def _chunked_cumsum(mask_f, tri):
    n_e, t = mask_f.shape
    off = jnp.zeros((n_e, 1), F32)
    outs = []
    for c in range(t // LANES):
        x = mask_f[:, c * LANES:(c + 1) * LANES]
        inc = jnp.dot(x.astype(BF16), tri, preferred_element_type=F32)
        outs.append(inc - x + off)
        off = off + inc[:, LANES - 1:LANES]
    return jnp.concatenate(outs, axis=1)


def _select_kernel(aff_ref, pos_ref, cs_ref, *, cap):
    aff = aff_ref[0]
    n_e = aff.shape[0]

    def count_ge(t):
        return jnp.sum(jnp.where(aff >= t, 1.0, 0.0), axis=1, keepdims=True)

    def body(carry):
        lo, hi, _ = carry
        mid = 0.5 * (lo + hi)
        ok = count_ge(mid) >= cap
        lo, hi = jnp.where(ok, mid, lo), jnp.where(ok, hi, mid)
        mid = 0.5 * (lo + hi)
        return lo, hi, jnp.max(jnp.where((mid > lo) & (mid < hi), 1.0, 0.0))

    lo, hi, _ = lax.while_loop(lambda carry: carry[2] > 0.5, body,
                               (jnp.zeros((n_e, 1), F32), jnp.full((n_e, 1), 2.0, F32), jnp.float32(1.0)))
    gt = jnp.where(aff >= hi, 1.0, 0.0)
    eq = jnp.where(aff >= lo, 1.0, 0.0) - gt
    need = cap - jnp.sum(gt, axis=1, keepdims=True)
    tri = jnp.where(lax.broadcasted_iota(I32, (LANES, LANES), 0) <= lax.broadcasted_iota(I32, (LANES, LANES), 1),
                    1.0, 0.0).astype(BF16)
    eq_rank = _chunked_cumsum(eq, tri)
    sel = gt + eq * jnp.where(eq_rank < need, 1.0, 0.0)
    cs = _chunked_cumsum(sel, tri)
    cs_ref[0] = cs.astype(I32)
    pos_ref[0] = jnp.where(sel > 0.5, cs, -1.0).astype(I32)


def _select(aff_t, cap):
    B, n_e, T = aff_t.shape
    blk = pl.BlockSpec((1, n_e, T), lambda b: (b, 0, 0))
    return pl.pallas_call(
        functools.partial(_select_kernel, cap=cap),
        grid=(B,),
        in_specs=[blk],
        out_specs=[blk, blk],
        out_shape=[jax.ShapeDtypeStruct((B, n_e, T), I32), jax.ShapeDtypeStruct((B, n_e, T), I32)],
        compiler_params=_params(("arbitrary",)),
        name="expert_select",
    )(aff_t)


def _window_start(base_ref, flat):
    start = base_ref[flat]
    return pl.multiple_of((start // BF16_ROWS) * BF16_ROWS, BF16_ROWS)


def _expert_kernel(base_ref, x_ref, pos_ref, g_ref, wg_ref, wu_ref, wd_ref, y_ref, xs_ref, gs_ref, *, n_tiles, cap):
    b, e = pl.program_id(0), pl.program_id(1)
    head = BF16_ROWS
    xs_ref[0:head, :] = jnp.zeros((head, xs_ref.shape[1]), BF16)
    gs_ref[0:head, :] = jnp.zeros((head, 1), F32)
    row_id = lax.broadcasted_iota(I32, (MOE_WIN, MOE_TILE), 0)

    def gather(k, carry):
        s16 = _window_start(base_ref, (b * N_EXPERTS + e) * (n_tiles + 1) + k)
        rel = pos_ref[0, pl.ds(k, 1), :] - s16
        hit = row_id == rel
        onehot = jnp.where(hit, 1.0, 0.0).astype(BF16)
        tok = pl.ds(pl.multiple_of(k * MOE_TILE, MOE_TILE), MOE_TILE)
        rows = jnp.dot(onehot, x_ref[0, tok, :], preferred_element_type=F32)
        first = xs_ref[pl.ds(s16, head), :].astype(F32) + rows[:head]
        xs_ref[pl.ds(s16 + head, MOE_WIN - head), :] = rows[head:].astype(BF16)
        xs_ref[pl.ds(s16, head), :] = first.astype(BF16)
        gate = jnp.sum(jnp.where(hit, g_ref[0, pl.ds(k, 1), :], 0.0), axis=1, keepdims=True)
        first_gate = gs_ref[pl.ds(s16, head), :] + gate[:head]
        gs_ref[pl.ds(s16 + head, MOE_WIN - head), :] = gate[head:]
        gs_ref[pl.ds(s16, head), :] = first_gate
        return carry

    lax.fori_loop(0, n_tiles, gather, 0, unroll=4)

    for r in range(cap // ROW_TILE):
        rows = slice(r * ROW_TILE, (r + 1) * ROW_TILE)
        xs = xs_ref[rows, :]
        a = jnp.dot(xs, wg_ref[0], preferred_element_type=F32)
        u = jnp.dot(xs, wu_ref[0], preferred_element_type=F32)
        hm = (a * _sigmoid(a) * u).astype(BF16)
        y_ref[0, 0, rows, :] = (jnp.dot(hm, wd_ref[0], preferred_element_type=F32) * gs_ref[rows, :]).astype(BF16)
    y_ref[0, 0, cap:, :] = jnp.zeros((y_ref.shape[2] - cap, y_ref.shape[3]), BF16)


def _experts(base, h2, pos3, g3, wg, wu, wd, cap):
    B, T, D = h2.shape
    nk = T // MOE_TILE
    yr = cap + MOE_WIN
    ff = wg.shape[2]
    wspec = lambda shape: pl.BlockSpec((1,) + shape, lambda b, e, base: (e, 0, 0))
    grid_spec = pltpu.PrefetchScalarGridSpec(
        num_scalar_prefetch=1,
        grid=(B, N_EXPERTS),
        in_specs=[pl.BlockSpec((1, T, D), lambda b, e, base: (b, 0, 0), pipeline_mode=pl.Buffered(1)),
                  pl.BlockSpec((1, nk, MOE_TILE), lambda b, e, base: (b * N_EXPERTS + e, 0, 0)),
                  pl.BlockSpec((1, nk, MOE_TILE), lambda b, e, base: (b * N_EXPERTS + e, 0, 0)),
                  wspec((D, ff)), wspec((D, ff)), wspec((ff, D))],
        out_specs=pl.BlockSpec((1, 1, yr, D), lambda b, e, base: (b, e, 0, 0)),
        scratch_shapes=[pltpu.VMEM((yr, D), BF16), pltpu.VMEM((yr, 1), F32)])
    return pl.pallas_call(
        functools.partial(_expert_kernel, n_tiles=nk, cap=cap),
        grid_spec=grid_spec,
        out_shape=jax.ShapeDtypeStruct((B, N_EXPERTS, yr, D), BF16),
        compiler_params=_params(("arbitrary", "arbitrary")),
        name="expert_ffn",
    )(base, h2, pos3, g3, wg, wu, wd)


def _combine_kernel(base_ref, y_ref, pos_ref, x1_ref, g2_ref, fn_ref, o_ref, *, n_tiles, tiles_per_blk):
    b, tb, e = pl.program_id(0), pl.program_id(1), pl.program_id(2)
    flat0 = (b * N_EXPERTS + e) * (n_tiles + 1) + tb * tiles_per_blk
    pick = lax.broadcasted_iota(I32, (MOE_TILE, N_EXPERTS), 1) == e
    col_id = lax.broadcasted_iota(I32, (MOE_TILE, MOE_TILE), 1).astype(F32)

    @pl.when(e == 0)
    def _():
        o_ref[...] = jnp.zeros_like(o_ref)

    def column(k):
        rows = slice(k * MOE_TILE, (k + 1) * MOE_TILE)
        pcol = jnp.sum(jnp.where(pick, pos_ref[0, rows, :].astype(F32), 0.0), axis=1, keepdims=True)
        return rows, pcol

    for k in range(tiles_per_blk):
        s16 = _window_start(base_ref, flat0 + k)
        rows, pcol = column(k)
        onehot = jnp.where(col_id == pcol - s16.astype(F32), 1.0, 0.0).astype(BF16)
        o_ref[0, rows, :] += jnp.dot(onehot, y_ref[0, 0, pl.ds(s16, MOE_TILE), :], preferred_element_type=F32)

    for k in range(tiles_per_blk):
        s16 = _window_start(base_ref, flat0 + k)

        @pl.when(base_ref[flat0 + k + 1] > s16 + MOE_TILE)
        def _():
            rows, pcol = column(k)
            tail = lax.broadcasted_iota(I32, (MOE_TILE, BF16_ROWS), 1).astype(F32) + float(MOE_TILE)
            onehot = jnp.where(tail == pcol - s16.astype(F32), 1.0, 0.0).astype(BF16)
            o_ref[0, rows, :] += jnp.dot(onehot, y_ref[0, 0, pl.ds(s16 + MOE_TILE, BF16_ROWS), :],
                                         preferred_element_type=F32)

    @pl.when(e == N_EXPERTS - 1)
    def _():
        for k in range(tiles_per_blk):
            rows = slice(k * MOE_TILE, (k + 1) * MOE_TILE)
            x2 = x1_ref[0, rows, :] + g2_ref[0] * o_ref[0, rows, :]
            o_ref[0, rows, :] = _rms(x2, fn_ref[...])


def _combine(base, y, pos_t, x1, mod3, final_norm):
    B, T, D = x1.shape
    nk = T // MOE_TILE
    per = 8
    blk = per * MOE_TILE
    yr = y.shape[2]
    tok = lambda b, tb, e, base: (b, tb, 0)
    grid_spec = pltpu.PrefetchScalarGridSpec(
        num_scalar_prefetch=1,
        grid=(B, nk // per, N_EXPERTS),
        in_specs=[pl.BlockSpec((1, 1, yr, D), lambda b, tb, e, base: (b, e, 0, 0)),
                  pl.BlockSpec((1, blk, N_EXPERTS), tok),
                  pl.BlockSpec((1, blk, D), tok),
                  pl.BlockSpec((1, 1, D), lambda b, tb, e, base: (b, 0, 5)),
                  pl.BlockSpec((1, D), lambda b, tb, e, base: (0, 0))],
        out_specs=pl.BlockSpec((1, blk, D), tok))
    return pl.pallas_call(
        functools.partial(_combine_kernel, n_tiles=nk, tiles_per_blk=per),
        grid_spec=grid_spec,
        out_shape=jax.ShapeDtypeStruct((B, T, D), F32),
        compiler_params=_params(("arbitrary",) * 3),
        name="moe_combine",
    )(base, y, pos_t, x1, mod3, final_norm)


def _rope_tables(T):
    rows = T // GRID_W
    row = np.repeat(np.arange(rows, dtype=np.float64), GRID_W)
    col = np.tile(np.arange(GRID_W, dtype=np.float64), rows)
    inv = ROPE_BASE ** (-np.arange(ROPE_PAIRS, dtype=np.float64) / ROPE_PAIRS)
    ar, ac = row[:, None] * inv, col[:, None] * inv
    ones = np.ones((T, LANES - QK_ROPE))
    cos = np.concatenate([np.cos(ar), np.cos(ar), np.cos(ac), np.cos(ac), ones], axis=1)
    sin = np.concatenate([-np.sin(ar), np.sin(ar), -np.sin(ac), np.sin(ac), 0.0 * ones], axis=1)
    cos = np.concatenate([np.ones((ROW_TILE, LANES)), cos], axis=0)
    sin = np.concatenate([np.zeros((ROW_TILE, LANES)), sin], axis=0)
    return jnp.asarray(cos, F32), jnp.asarray(sin, F32)


def _blockdiag_dense(w):
    n, bs, _ = w.shape
    rows = jnp.broadcast_to(w.transpose(1, 0, 2).reshape(1, bs, n * bs), (n, bs, n * bs)).reshape(n * bs, n * bs)
    r = jnp.arange(n * bs) // bs
    return jnp.where(r[:, None] == r[None, :], rows, 0.0).astype(BF16)


def kernel(x, c, ctx, c_ctx, w_mod, b_mod, norm1, w_in, q_norm, w_uq, kv_norm, w_ukv, conv_w, conv_b,
           w_qblk, w_kblk, w_vblk, w_gate, b_gate, ml_norm, ml_skip, w_out, norm2, w_router,
           w_e_gate, w_e_up, w_e_down, final_norm):
    B, T, D = x.shape
    assert w_mod.shape[0] == 1 and D == D_MODEL and ctx.shape[1] == ROW_TILE
    cap = CAP_FACTOR * T // N_EXPERTS

    cc = jnp.zeros((SUBLANES, D), F32).at[:B].set(c).at[B].set(c_ctx)
    mod = _modulation(cc, w_mod[0], b_mod[0].reshape(1, -1))
    mod3 = mod[:B + 1].reshape(B + 1, 1, 6 * D)

    wi = w_in[0]
    zpad = jnp.zeros((D, LANES - QK_ROPE), F32)
    w_in_p = jnp.concatenate([wi[:, :_C_KROPE + QK_ROPE], zpad, wi[:, _C_KROPE + QK_ROPE:]], axis=1).astype(BF16)
    wuq = w_uq[0].reshape(Q_LORA, MLA_HEADS, QK_NOPE + QK_ROPE)
    wuq_nope = wuq[:, :, :QK_NOPE].reshape(Q_LORA, MLA_HEADS * QK_NOPE).astype(BF16)
    wuq_pe = jnp.pad(wuq[:, :, QK_NOPE:], ((0, 0), (0, 0), (0, LANES - QK_ROPE))).reshape(
        Q_LORA, MLA_HEADS * LANES).astype(BF16)
    wukv = w_ukv[0].reshape(KV_LORA, MLA_HEADS, QK_NOPE + V_DIM)
    wukv_p = jnp.concatenate([wukv[:, :, :QK_NOPE].reshape(KV_LORA, -1),
                              wukv[:, :, QK_NOPE:].reshape(KV_LORA, -1)], axis=1).astype(BF16)
    cos_t, sin_t = _rope_tables(T)

    q, k, v, xm, z, gmla, gml = _input_projection(
        ctx, x, mod3, norm1, w_in_p, q_norm, wuq_nope, wuq_pe, kv_norm, wukv_p, cos_t, sin_t)

    conv_w8 = jnp.zeros((SUBLANES, ML_INNER), F32).at[:CONV_W].set(conv_w[0])
    wg3 = w_gate[0].reshape(3, ML_INNER, 4 * ML_HEADS).transpose(0, 2, 1).astype(BF16)
    mq, mk, mv, xc, gates = _mlstm_features(
        xm, conv_w8, conv_b, _blockdiag_dense(w_qblk[0].transpose(0, 2, 1)), _blockdiag_dense(w_kblk[0]),
        _blockdiag_dense(w_vblk[0].transpose(0, 2, 1)), wg3, b_gate[0].reshape(-1, 1))

    y_mla = _attention(q, k, v)
    hf, hb = _mlstm_scan(gates, mq, mk, mv)

    x1, h2, aff_t = _merge(hf, hb, z, xc, gmla, gml, y_mla, x, mod3, ml_norm, ml_skip,
                           w_out[0].astype(BF16), norm2, w_router[0].T)

    pos, cs = _select(aff_t, cap)
    base = jnp.concatenate([cs[:, :, ::MOE_TILE], jnp.full((B, N_EXPERTS, 1), cap, I32)], axis=2).reshape(-1)
    tiled = (B * N_EXPERTS, T // MOE_TILE, MOE_TILE)
    y = _experts(base, h2, pos.reshape(tiled), aff_t.reshape(tiled), w_e_gate[0].astype(BF16),
                 w_e_up[0].astype(BF16), w_e_down[0].astype(BF16), cap)
    return _combine(base, y, pos.transpose(0, 2, 1), x1, mod3, final_norm.reshape(1, -1))
```

```python
import functools

import numpy as np
import jax
import jax.numpy as jnp
from jax import lax
from jax.experimental import pallas as pl
from jax.experimental.pallas import tpu as pltpu

F32 = jnp.float32
BF16 = jnp.bfloat16
I32 = jnp.int32

D_MODEL = 1024
GRID_W = 64
MLA_HEADS = 8
QK_NOPE = 128
QK_ROPE = 64
V_DIM = 128
Q_LORA = 384
KV_LORA = 256
ROPE_BASE = 10000.0
ROPE_PAIRS = QK_ROPE // 4
ATTN_SCALE = (QK_NOPE + QK_ROPE) ** -0.5
Q_SCALE = ATTN_SCALE * 1.4426950408889634
ML_HEADS = 4
ML_INNER = 1024
ML_HEAD_DIM = ML_INNER // ML_HEADS
QKV_BLOCK = 4
CONV_W = 5
CHUNK = 128
N_EXPERTS = 16
EXPERT_FF = 1024
CAP_FACTOR = 2
EPS = 1e-6

LANES = 128
SUBLANES = 8
BF16_ROWS = 16
ROW_TILE = 256
HEAD_W = 256
ATTN_TQ = 256
ATTN_TK = 768
MOE_TILE = 256
MOE_WIN = MOE_TILE + BF16_ROWS
VMEM_LIMIT = 56 * 1024 * 1024

_C_QLAT = 0
_C_KVLAT = Q_LORA
_C_KROPE = Q_LORA + KV_LORA
_C_XM = _C_KROPE + LANES
_C_Z = _C_XM + ML_INNER
_C_GMLA = _C_Z + ML_INNER
_C_GML = _C_GMLA + D_MODEL
IN_PAD = _C_GML + D_MODEL

_NT = (((1,), (1,)), ((), ()))
_TN = (((0,), (0,)), ((), ()))


def _params(sem, vmem=VMEM_LIMIT):
    return pltpu.CompilerParams(dimension_semantics=sem, vmem_limit_bytes=vmem)


def _rms(x, g):
    return x * lax.rsqrt(jnp.mean(x * x, axis=-1, keepdims=True) + EPS) * g


def _sigmoid(x):
    return jax.nn.sigmoid(x)


def _mod_kernel(c_ref, w_ref, b_ref, o_ref):
    c = c_ref[...]
    s = c * _sigmoid(c)
    o_ref[...] = jnp.dot(s, w_ref[...], preferred_element_type=F32,
                         precision=lax.Precision.HIGHEST) + b_ref[...]


def _modulation(cc, w_mod, b_mod):
    n = w_mod.shape[1]
    tn = 768
    return pl.pallas_call(
        _mod_kernel,
        grid=(n // tn,),
        in_specs=[pl.BlockSpec((SUBLANES, D_MODEL), lambda j: (0, 0)),
                  pl.BlockSpec((D_MODEL, tn), lambda j: (0, j)),
                  pl.BlockSpec((1, tn), lambda j: (0, j))],
        out_specs=pl.BlockSpec((SUBLANES, tn), lambda j: (0, j)),
        out_shape=jax.ShapeDtypeStruct((SUBLANES, n), F32),
        compiler_params=_params(("arbitrary",)),
        name="modulation",
    )(cc, w_mod, b_mod)


def _rope(v, cos, sin):
    lane = lax.broadcasted_iota(I32, v.shape, 1)
    partner = jnp.where(lane % 32 < 16, pltpu.roll(v, LANES - 16, 1), pltpu.roll(v, 16, 1))
    return v * cos + partner * sin


def _inproj_kernel(ctx_ref, x_ref, sh_ref, sc_ref, n1_ref, win_ref, qn_ref, wuqn_ref, wuqp_ref,
                   kvn_ref, wukv_ref, cos_ref, sin_ref,
                   q_ref, k_ref, v_ref, xm_ref, z_ref, gmla_ref, gml_ref):
    i = pl.program_id(1)
    xin = jnp.where(i == 0, ctx_ref[0], x_ref[0])
    h = _rms(xin, n1_ref[...]) * (1.0 + sc_ref[0]) + sh_ref[0]
    big = jnp.dot(h.astype(BF16), win_ref[...], preferred_element_type=F32)
    xm_ref[0] = big[:, _C_XM:_C_Z]
    z_ref[0] = big[:, _C_Z:_C_GMLA].astype(BF16)
    gmla_ref[0] = big[:, _C_GMLA:_C_GML].astype(BF16)
    gml_ref[0] = big[:, _C_GML:IN_PAD].astype(BF16)

    cos = cos_ref[...]
    sin = sin_ref[...]
    qn = _rms(big[:, _C_QLAT:_C_KVLAT], qn_ref[...]).astype(BF16)
    q_nope = jnp.dot(qn, wuqn_ref[...], preferred_element_type=F32)
    q_pe = jnp.dot(qn, wuqp_ref[...], preferred_element_type=F32)
    kvn = _rms(big[:, _C_KVLAT:_C_KROPE], kvn_ref[...]).astype(BF16)
    kv = jnp.dot(kvn, wukv_ref[...], preferred_element_type=F32)
    k_pe = _rope(big[:, _C_KROPE:_C_XM], cos, sin).astype(BF16)
    ones_col = jnp.where(lax.broadcasted_iota(I32, (big.shape[0], HEAD_W - V_DIM), 1) == 0, 1.0, 0.0).astype(BF16)
    for hh in range(MLA_HEADS):
        lo = hh * HEAD_W
        nope = slice(hh * QK_NOPE, (hh + 1) * QK_NOPE)
        q_ref[0, :, lo:lo + QK_NOPE] = (q_nope[:, nope] * Q_SCALE).astype(BF16)
        q_ref[0, :, lo + QK_NOPE:lo + HEAD_W] = (
            _rope(q_pe[:, hh * LANES:(hh + 1) * LANES], cos, sin) * Q_SCALE).astype(BF16)
        k_ref[0, :, lo:lo + QK_NOPE] = kv[:, nope].astype(BF16)
        k_ref[0, :, lo + QK_NOPE:lo + HEAD_W] = k_pe
        v_ref[0, :, lo:lo + V_DIM] = kv[:, MLA_HEADS * QK_NOPE + hh * V_DIM:MLA_HEADS * QK_NOPE + (hh + 1) * V_DIM].astype(BF16)
        v_ref[0, :, lo + V_DIM:lo + HEAD_W] = ones_col


def _input_projection(ctx, x, mod3, norm1, w_in_p, q_norm, wuq_nope, wuq_pe, kv_norm, wukv_p, cos_t, sin_t):
    B, T, D = x.shape
    n_lat = T // ROW_TILE
    nt = n_lat + 1
    tt = T + ROW_TILE
    const2 = lambda b, i: (0, 0)
    lat = lambda b, i: (b, jnp.maximum(i - 1, 0), 0)
    allrows = lambda b, i: (b, i, 0)
    modrow = lambda col: (lambda b, i: (jnp.where(i == 0, B, b), 0, col))
    return pl.pallas_call(
        _inproj_kernel,
        grid=(B, nt),
        in_specs=[pl.BlockSpec((1, ROW_TILE, D), lambda b, i: (b, 0, 0)),
                  pl.BlockSpec((1, ROW_TILE, D), lat),
                  pl.BlockSpec((1, 1, D), modrow(0)),
                  pl.BlockSpec((1, 1, D), modrow(1)),
                  pl.BlockSpec((1, D), const2),
                  pl.BlockSpec(w_in_p.shape, const2),
                  pl.BlockSpec((1, Q_LORA), const2),
                  pl.BlockSpec(wuq_nope.shape, const2),
                  pl.BlockSpec(wuq_pe.shape, const2),
                  pl.BlockSpec((1, KV_LORA), const2),
                  pl.BlockSpec(wukv_p.shape, const2),
                  pl.BlockSpec((ROW_TILE, LANES), lambda b, i: (i, 0)),
                  pl.BlockSpec((ROW_TILE, LANES), lambda b, i: (i, 0))],
        out_specs=[pl.BlockSpec((1, ROW_TILE, MLA_HEADS * HEAD_W), lat),
                   pl.BlockSpec((1, ROW_TILE, MLA_HEADS * HEAD_W), allrows),
                   pl.BlockSpec((1, ROW_TILE, MLA_HEADS * HEAD_W), allrows),
                   pl.BlockSpec((1, ROW_TILE, ML_INNER), allrows),
                   pl.BlockSpec((1, ROW_TILE, ML_INNER), lat),
                   pl.BlockSpec((1, ROW_TILE, D), lat),
                   pl.BlockSpec((1, ROW_TILE, D), lat)],
        out_shape=[jax.ShapeDtypeStruct((B, T, MLA_HEADS * HEAD_W), BF16),
                   jax.ShapeDtypeStruct((B, tt, MLA_HEADS * HEAD_W), BF16),
                   jax.ShapeDtypeStruct((B, tt, MLA_HEADS * HEAD_W), BF16),
                   jax.ShapeDtypeStruct((B, tt, ML_INNER), F32),
                   jax.ShapeDtypeStruct((B, T, ML_INNER), BF16),
                   jax.ShapeDtypeStruct((B, T, D), BF16),
                   jax.ShapeDtypeStruct((B, T, D), BF16)],
        compiler_params=_params(("arbitrary", "arbitrary")),
        name="input_projection",
    )(ctx, x, mod3, mod3, norm1, w_in_p, q_norm, wuq_nope, wuq_pe, kv_norm, wukv_p, cos_t, sin_t)


def _feat_kernel(prev_ref, cur_ref, next_ref, cw_ref, cb_ref, wq_ref, wk_ref, wv_ref, wg_ref, bg_ref,
                 q_ref, k_ref, v_ref, xc_ref, g_ref, *, n_tiles):
    i = pl.program_id(1)
    cur = cur_ref[0]
    prev = jnp.where(i <= 1, 0.0, prev_ref[0])
    nxt = jnp.where((i == 0) | (i == n_tiles - 1), 0.0, next_ref[0])
    xx = jnp.concatenate([prev, cur, nxt], axis=0)
    rows = cur.shape[0]
    acc = jnp.broadcast_to(cb_ref[...], cur.shape)
    for w in range(CONV_W):
        lo = SUBLANES - CONV_W // 2 + w
        acc = acc + xx[lo:lo + rows] * cw_ref[w:w + 1, :]
    xc = acc * _sigmoid(acc)
    xc_ref[0] = xc.astype(BF16)
    xcb = xc.astype(BF16)
    q_t = lax.dot_general(wq_ref[...], xcb, _NT, preferred_element_type=F32)
    k = jnp.dot(xcb, wk_ref[...], preferred_element_type=F32)
    v_t = lax.dot_general(wv_ref[...], cur.astype(BF16), _NT, preferred_element_type=F32)
    qb, kb, vb = q_t.astype(BF16), k.astype(BF16), v_t.astype(BF16)
    q_ref[0] = qb
    k_ref[0] = (k * (ML_HEAD_DIM ** -0.5)).astype(BF16)
    v_ref[0] = vb
    g = (jnp.dot(wg_ref[0], qb, preferred_element_type=F32)
         + lax.dot_general(wg_ref[1], kb, _NT, preferred_element_type=F32)
         + jnp.dot(wg_ref[2], vb, preferred_element_type=F32)) + bg_ref[...]
    row = lax.broadcasted_iota(I32, (g.shape[0], CHUNK), 0)
    fwd_forget = (row >= ML_HEADS) & (row < 2 * ML_HEADS)
    bwd_forget = row >= 3 * ML_HEADS
    src = lax.broadcasted_iota(I32, (CHUNK, CHUNK), 0)
    dst = lax.broadcasted_iota(I32, (CHUNK, CHUNK), 1)
    tri_prefix = jnp.where(src <= dst, 1.0, 0.0).astype(BF16)
    tri_suffix = jnp.where(src >= dst, 1.0, 0.0).astype(BF16)
    for c in range(g.shape[1] // CHUNK):
        gc = g[:, c * CHUNK:(c + 1) * CHUNK]
        lf = jax.nn.log_sigmoid(gc)
        lf_hi = lf.astype(BF16)
        lf_lo = (lf - lf_hi.astype(F32)).astype(BF16)
        prefix = (jnp.dot(lf_hi, tri_prefix, preferred_element_type=F32)
                  + jnp.dot(lf_lo, tri_prefix, preferred_element_type=F32))
        suffix = (jnp.dot(lf_hi, tri_suffix, preferred_element_type=F32)
                  + jnp.dot(lf_lo, tri_suffix, preferred_element_type=F32))
        g_ref[0, :, c * CHUNK:(c + 1) * CHUNK] = jnp.where(fwd_forget, prefix, jnp.where(bwd_forget, suffix, gc))


def _mlstm_features(xm, conv_w8, conv_b, wq_bd_t, wk_bd, wv_bd_t, wg3, bg_col):
    B, tt, C = xm.shape
    nt = tt // ROW_TILE
    per = ROW_TILE // SUBLANES
    last8 = tt // SUBLANES - 1
    const2 = lambda b, i: (0, 0)
    rows = lambda b, i: (b, i, 0)
    cols = lambda b, i: (b, 0, i)
    ng = 4 * ML_HEADS
    return pl.pallas_call(
        functools.partial(_feat_kernel, n_tiles=nt),
        grid=(B, nt),
        in_specs=[pl.BlockSpec((1, SUBLANES, C), lambda b, i: (b, jnp.maximum(i * per - 1, 0), 0)),
                  pl.BlockSpec((1, ROW_TILE, C), rows),
                  pl.BlockSpec((1, SUBLANES, C), lambda b, i: (b, jnp.minimum((i + 1) * per, last8), 0)),
                  pl.BlockSpec((SUBLANES, C), const2),
                  pl.BlockSpec((1, C), const2),
                  pl.BlockSpec((C, C), const2),
                  pl.BlockSpec((C, C), const2),
                  pl.BlockSpec((C, C), const2),
                  pl.BlockSpec((3, ng, C), lambda b, i: (0, 0, 0)),
                  pl.BlockSpec((ng, 1), const2)],
        out_specs=[pl.BlockSpec((1, C, ROW_TILE), cols),
                   pl.BlockSpec((1, ROW_TILE, C), rows),
                   pl.BlockSpec((1, C, ROW_TILE), cols),
                   pl.BlockSpec((1, ROW_TILE, C), rows),
                   pl.BlockSpec((1, ng, ROW_TILE), cols)],
        out_shape=[jax.ShapeDtypeStruct((B, C, tt), BF16),
                   jax.ShapeDtypeStruct((B, tt, C), BF16),
                   jax.ShapeDtypeStruct((B, C, tt), BF16),
                   jax.ShapeDtypeStruct((B, tt, C), BF16),
                   jax.ShapeDtypeStruct((B, ng, tt), F32)],
        compiler_params=_params(("arbitrary", "arbitrary")),
        name="mlstm_features",
    )(xm, xm, xm, conv_w8, conv_b, wq_bd_t, wk_bd, wv_bd_t, wg3, bg_col)


def _attn_kernel(q_ref, k_ref, v_ref, o_ref, s0_ref, s1_ref, p0_ref, p1_ref, m_ref, *, n_chunks):
    j = pl.program_id(0)

    @pl.when(j == 0)
    def _():
        s1_ref[...] = jnp.zeros_like(s1_ref)
        p0_ref[...] = jnp.ones_like(p0_ref)
        m_ref[...] = jnp.zeros_like(m_ref)

    def step(parity, s_new, s_old, p_new, p_old):
        q = q_ref[0]
        m_old = m_ref[1 - parity]
        m = jnp.full((ATTN_TQ, 1), -jnp.inf, F32)
        acc = jnp.zeros((ATTN_TQ, HEAD_W), F32)
        for c in range(n_chunks):
            keys = slice(c * ATTN_TK, (c + 1) * ATTN_TK)
            acc = acc + jnp.dot(p_old[c], v_ref[0, keys, :], preferred_element_type=F32)
            p_new[c] = jnp.exp2(s_old[c] - m_old).astype(BF16)
            s = lax.dot_general(q, k_ref[0, keys, :], _NT, preferred_element_type=F32)
            s_new[c] = s
            m = jnp.maximum(m, jnp.max(s, axis=-1, keepdims=True))
        m_ref[parity] = m
        o_ref[0] = (acc[:, :V_DIM] / acc[:, V_DIM:V_DIM + 1]).astype(BF16)

    @pl.when(j % 2 == 0)
    def _():
        step(0, s0_ref, s1_ref, p1_ref, p0_ref)

    @pl.when(j % 2 == 1)
    def _():
        step(1, s1_ref, s0_ref, p0_ref, p1_ref)


def _attention(q, k, v):
    B, T, _ = q.shape
    tt = k.shape[1]
    n_chunks = tt // ATTN_TK
    n_tiles = T // ATTN_TQ
    n_total = B * MLA_HEADS * n_tiles
    lag = 2
    assert n_chunks * ATTN_TK == tt and n_tiles * ATTN_TQ == T

    def tile(t):
        t = jnp.clip(t, 0, n_total - 1)
        bh = t // n_tiles
        return bh // MLA_HEADS, t % n_tiles, bh % MLA_HEADS

    def q_idx(j):
        b, i, h = tile(j)
        return b, i, h

    def k_idx(j):
        b, _, h = tile(j)
        return b, 0, h

    def v_idx(j):
        b, _, h = tile(j - lag)
        return b, 0, h

    def o_idx(j):
        b, i, h = tile(j - lag)
        return b, i, h

    buf = lambda dt: pltpu.VMEM((n_chunks, ATTN_TQ, ATTN_TK), dt)
    return pl.pallas_call(
        functools.partial(_attn_kernel, n_chunks=n_chunks),
        grid=(n_total + lag,),
        in_specs=[pl.BlockSpec((1, ATTN_TQ, HEAD_W), q_idx),
                  pl.BlockSpec((1, tt, HEAD_W), k_idx),
                  pl.BlockSpec((1, tt, HEAD_W), v_idx)],
        out_specs=pl.BlockSpec((1, ATTN_TQ, V_DIM), o_idx),
        out_shape=jax.ShapeDtypeStruct((B, T, MLA_HEADS * V_DIM), BF16),
        scratch_shapes=[buf(F32), buf(F32), buf(BF16), buf(BF16), pltpu.VMEM((2, ATTN_TQ, 1), F32)],
        compiler_params=_params(("arbitrary",)),
        name="attention",
    )(q, k, v)


def _scan_kernel(gf_ref, gb_ref, qf_ref, kf_ref, vf_ref, qb_ref, kb_ref, vb_ref,
                 hf_ref, hb_ref, cn_ref, m_ref):
    j = pl.program_id(0)
    L = CHUNK
    dh = ML_HEAD_DIM

    @pl.when(j == 0)
    def _():
        cn_ref[...] = jnp.zeros_like(cn_ref)
        m_ref[...] = jnp.zeros_like(m_ref)

    spos = lax.broadcasted_iota(I32, (L, L), 0)
    tpos = lax.broadcasted_iota(I32, (L, L), 1)
    first_row = lax.broadcasted_iota(I32, (BF16_ROWS, L), 0) == 0
    dirs = ((gf_ref, qf_ref, kf_ref, vf_ref, hf_ref), (gb_ref, qb_ref, kb_ref, vb_ref, hb_ref))
    for bi in range(gf_ref.shape[0]):
        for d, (g_ref, q_ref, k_ref, v_ref, h_ref) in enumerate(dirs):
            reverse = d == 1
            mask = (spos >= tpos) if reverse else (spos <= tpos)
            g = g_ref[bi]
            ig4 = g[d * 2 * ML_HEADS:d * 2 * ML_HEADS + ML_HEADS]
            b4 = g[d * 2 * ML_HEADS + ML_HEADS:(d + 1) * 2 * ML_HEADS]
            a4 = ig4 - b4
            a_cols = jnp.concatenate([a4, jnp.zeros((L - ML_HEADS, L), F32)], axis=0).T
            for hh in range(ML_HEADS):
                ci = (bi * 2 + d) * ML_HEADS + hh
                b_row, a_row, a_col = b4[hh:hh + 1], a4[hh:hh + 1], a_cols[:, hh:hh + 1]
                b_last = b_row[:, 0:1] if reverse else b_row[:, L - 1:L]
                m = m_ref[ci, 0:1, 0:1]
                sl = slice(hh * dh, (hh + 1) * dh)
                q_t, k, v_t = q_ref[bi, sl, :], k_ref[bi, :, sl], v_ref[bi, sl, :]
                cn = cn_ref[ci]

                dmat = jnp.where(mask, b_row + a_col, -jnp.inf)
                inter = b_row + m
                m_t = jnp.maximum(inter, jnp.max(dmat, axis=0, keepdims=True))
                w_inter = jnp.exp(inter - m_t)
                s = jnp.dot(k, q_t, preferred_element_type=F32) * jnp.exp(dmat - m_t)
                cq = jnp.dot(cn.astype(BF16), q_t, preferred_element_type=F32)
                num = jnp.dot(v_t, s.astype(BF16), preferred_element_type=F32) + w_inter * cq[:dh]
                den = jnp.sum(s, axis=0, keepdims=True) + w_inter * cq[dh:dh + 1]
                h_ref[bi, sl, :] = (num / jnp.maximum(jnp.abs(den), jnp.exp(-m_t))).astype(BF16)

                dec = b_last + a_row
                m_new = jnp.maximum(b_last + m, jnp.max(dec, axis=-1, keepdims=True))
                wk = jnp.exp(dec - m_new)
                keep = jnp.exp(b_last + m - m_new)
                vw = jnp.concatenate([(v_t.astype(F32) * wk).astype(BF16),
                                      jnp.where(first_row, wk, 0.0).astype(BF16)], axis=0)
                cn_ref[ci] = keep * cn + jnp.dot(vw, k, preferred_element_type=F32)
                m_ref[ci] = jnp.broadcast_to(m_new, m_ref.shape[1:])


def _mlstm_scan(gates, q_t, k, v_t):
    B, tt, C = k.shape
    nch = tt // CHUNK
    ng = gates.shape[1]
    nc_ctx = ROW_TILE // CHUNK
    bidx = lambda j: jnp.where(j < nc_ctx, nc_ctx - 1 - j, nch - 1 + nc_ctx - j)
    rows_f = pl.BlockSpec((B, CHUNK, C), lambda j: (0, j, 0))
    rows_b = pl.BlockSpec((B, CHUNK, C), lambda j: (0, bidx(j), 0))
    cols_f = pl.BlockSpec((B, C, CHUNK), lambda j: (0, 0, j))
    cols_b = pl.BlockSpec((B, C, CHUNK), lambda j: (0, 0, bidx(j)))
    nchain = B * 2 * ML_HEADS
    return pl.pallas_call(
        _scan_kernel,
        grid=(nch,),
        in_specs=[pl.BlockSpec((B, ng, CHUNK), lambda j: (0, 0, j)),
                  pl.BlockSpec((B, ng, CHUNK), lambda j: (0, 0, bidx(j))),
                  cols_f, rows_f, cols_f, cols_b, rows_b, cols_b],
        out_specs=[cols_f, cols_b],
        out_shape=[jax.ShapeDtypeStruct((B, C, tt), BF16), jax.ShapeDtypeStruct((B, C, tt), BF16)],
        scratch_shapes=[pltpu.VMEM((nchain, ML_HEAD_DIM + BF16_ROWS, ML_HEAD_DIM), F32),
                        pltpu.VMEM((nchain, SUBLANES, LANES), F32)],
        compiler_params=_params(("arbitrary",)),
        name="mlstm_scan",
    )(gates, gates, q_t, k, v_t, q_t, k, v_t)


def _merge_kernel(hf_ref, hb_ref, z_ref, xc_ref, gmla_ref, gml_ref, ymla_ref, x_ref,
                  g1_ref, sh2_ref, sc2_ref, mln_ref, mls_ref, wout_ref, n2_ref, wr_ref,
                  x1_ref, h2_ref, aff_ref):
    h_t = hf_ref[0].astype(F32) + hb_ref[0].astype(F32)
    parts = []
    for hh in range(ML_HEADS):
        seg = h_t[hh * ML_HEAD_DIM:(hh + 1) * ML_HEAD_DIM]
        parts.append((seg * lax.rsqrt(jnp.mean(seg * seg, axis=0, keepdims=True) + EPS)).T)
    hn = jnp.concatenate(parts, axis=-1) * mln_ref[...]
    y_ml = _sigmoid(z_ref[0].astype(F32)) * (hn + mls_ref[...] * xc_ref[0].astype(F32))
    merged = (_sigmoid(gmla_ref[0].astype(F32)) * ymla_ref[0].astype(F32)
              + _sigmoid(gml_ref[0].astype(F32)) * y_ml)
    out = jnp.dot(merged.astype(BF16), wout_ref[...], preferred_element_type=F32)
    x1 = x_ref[0] + g1_ref[0] * out
    x1_ref[0] = x1
    h2 = _rms(x1, n2_ref[...]) * (1.0 + sc2_ref[0]) + sh2_ref[0]
    h2_hi = h2.astype(BF16)
    h2_ref[0] = h2_hi
    h2_lo = (h2 - h2_hi.astype(F32)).astype(BF16)
    wr = wr_ref[...]
    wr_hi = wr.astype(BF16)
    wr_lo = (wr - wr_hi.astype(F32)).astype(BF16)
    logits = (lax.dot_general(wr_hi, h2_hi, _NT, preferred_element_type=F32)
              + lax.dot_general(wr_hi, h2_lo, _NT, preferred_element_type=F32)
              + lax.dot_general(wr_lo, h2_hi, _NT, preferred_element_type=F32))
    e = jnp.exp(logits - jnp.max(logits, axis=0, keepdims=True))
    aff_ref[0] = e / jnp.sum(e, axis=0, keepdims=True)


def _merge(hf, hb, z, xc, gmla, gml, ymla, x, mod3, ml_norm, ml_skip, w_out, norm2, w_router_t):
    B, T, D = x.shape
    nt = T // ROW_TILE
    const2 = lambda b, i: (0, 0)
    lat = lambda b, i: (b, i, 0)
    shifted = lambda b, i: (b, i + 1, 0)
    modcol = lambda col: (lambda b, i: (b, 0, col))
    tile = lambda idx: pl.BlockSpec((1, ROW_TILE, D), idx)
    h_tile = pl.BlockSpec((1, ML_INNER, ROW_TILE), lambda b, i: (b, 0, i + 1))
    return pl.pallas_call(
        _merge_kernel,
        grid=(B, nt),
        in_specs=[h_tile, h_tile, tile(lat), tile(shifted), tile(lat), tile(lat), tile(lat),
                  tile(lat),
                  pl.BlockSpec((1, 1, D), modcol(2)), pl.BlockSpec((1, 1, D), modcol(3)),
                  pl.BlockSpec((1, 1, D), modcol(4)),
                  pl.BlockSpec((1, D), const2), pl.BlockSpec((1, D), const2),
                  pl.BlockSpec((D, D), const2), pl.BlockSpec((1, D), const2),
                  pl.BlockSpec((N_EXPERTS, D), const2)],
        out_specs=[tile(lat), tile(lat), pl.BlockSpec((1, N_EXPERTS, ROW_TILE), lambda b, i: (b, 0, i))],
        out_shape=[jax.ShapeDtypeStruct((B, T, D), F32),
                   jax.ShapeDtypeStruct((B, T, D), BF16),
                   jax.ShapeDtypeStruct((B, N_EXPERTS, T), F32)],
        compiler_params=_params(("arbitrary", "arbitrary")),
        name="merge_router",
    )(hf, hb, z, xc, gmla, gml, ymla, x, mod3, mod3, mod3, ml_norm, ml_skip, w_out, norm2, w_router_t)


def _chunked_cumsum(mask_f, tri):
    n_e, t = mask_f.shape
    off = jnp.zeros((n_e, 1), F32)
    outs = []
    for c in range(t // LANES):
        x = mask_f[:, c * LANES:(c + 1) * LANES]
        inc = jnp.dot(x.astype(BF16), tri, preferred_element_type=F32)
        outs.append(inc - x + off)
        off = off + inc[:, LANES - 1:LANES]
    return jnp.concatenate(outs, axis=1)


def _select_kernel(aff_ref, pos_ref, cs_ref, *, cap):
    aff = aff_ref[0]
    n_e = aff.shape[0]

    def count_ge(t):
        return jnp.sum(jnp.where(aff >= t, 1.0, 0.0), axis=1, keepdims=True)

    def body(carry):
        lo, hi, _ = carry
        mid = 0.5 * (lo + hi)
        ok = count_ge(mid) >= cap
        lo, hi = jnp.where(ok, mid, lo), jnp.where(ok, hi, mid)
        mid = 0.5 * (lo + hi)
        return lo, hi, jnp.max(jnp.where((mid > lo) & (mid < hi), 1.0, 0.0))

    lo, hi, _ = lax.while_loop(lambda carry: carry[2] > 0.5, body,
                               (jnp.zeros((n_e, 1), F32), jnp.full((n_e, 1), 2.0, F32), jnp.float32(1.0)))
    gt = jnp.where(aff >= hi, 1.0, 0.0)
    eq = jnp.where(aff >= lo, 1.0, 0.0) - gt
    need = cap - jnp.sum(gt, axis=1, keepdims=True)
    tri = jnp.where(lax.broadcasted_iota(I32, (LANES, LANES), 0) <= lax.broadcasted_iota(I32, (LANES, LANES), 1),
                    1.0, 0.0).astype(BF16)
    eq_rank = _chunked_cumsum(eq, tri)
    sel = gt + eq * jnp.where(eq_rank < need, 1.0, 0.0)
    cs = _chunked_cumsum(sel, tri)
    cs_ref[0] = cs.astype(I32)
    pos_ref[0] = jnp.where(sel > 0.5, cs, -1.0).astype(I32)


def _select(aff_t, cap):
    B, n_e, T = aff_t.shape
    blk = pl.BlockSpec((1, n_e, T), lambda b: (b, 0, 0))
    return pl.pallas_call(
        functools.partial(_select_kernel, cap=cap),
        grid=(B,),
        in_specs=[blk],
        out_specs=[blk, blk],
        out_shape=[jax.ShapeDtypeStruct((B, n_e, T), I32), jax.ShapeDtypeStruct((B, n_e, T), I32)],
        compiler_params=_params(("arbitrary",)),
        name="expert_select",
    )(aff_t)


def _window_start(base_ref, flat):
    start = base_ref[flat]
    return pl.multiple_of((start // BF16_ROWS) * BF16_ROWS, BF16_ROWS)


def _expert_kernel(base_ref, x_ref, pos_ref, g_ref, wg_ref, wu_ref, wd_ref, y_ref, xs_ref, gs_ref, *, n_tiles, cap):
    b, e = pl.program_id(0), pl.program_id(1)
    head = BF16_ROWS
    xs_ref[0:head, :] = jnp.zeros((head, xs_ref.shape[1]), BF16)
    gs_ref[0:head, :] = jnp.zeros((head, 1), F32)
    row_id = lax.broadcasted_iota(I32, (MOE_WIN, MOE_TILE), 0)

    def gather(k, carry):
        s16 = _window_start(base_ref, (b * N_EXPERTS + e) * (n_tiles + 1) + k)
        rel = pos_ref[0, pl.ds(k, 1), :] - s16
        hit = row_id == rel
        onehot = jnp.where(hit, 1.0, 0.0).astype(BF16)
        tok = pl.ds(pl.multiple_of(k * MOE_TILE, MOE_TILE), MOE_TILE)
        rows = jnp.dot(onehot, x_ref[0, tok, :], preferred_element_type=F32)
        first = xs_ref[pl.ds(s16, head), :].astype(F32) + rows[:head]
        xs_ref[pl.ds(s16 + head, MOE_WIN - head), :] = rows[head:].astype(BF16)
        xs_ref[pl.ds(s16, head), :] = first.astype(BF16)
        gate = jnp.sum(jnp.where(hit, g_ref[0, pl.ds(k, 1), :], 0.0), axis=1, keepdims=True)
        first_gate = gs_ref[pl.ds(s16, head), :] + gate[:head]
        gs_ref[pl.ds(s16 + head, MOE_WIN - head), :] = gate[head:]
        gs_ref[pl.ds(s16, head), :] = first_gate
        return carry

    lax.fori_loop(0, n_tiles, gather, 0, unroll=4)

    for r in range(cap // ROW_TILE):
        rows = slice(r * ROW_TILE, (r + 1) * ROW_TILE)
        xs = xs_ref[rows, :]
        a = jnp.dot(xs, wg_ref[0], preferred_element_type=F32)
        u = jnp.dot(xs, wu_ref[0], preferred_element_type=F32)
        hm = (a * _sigmoid(a) * u).astype(BF16)
        y_ref[0, 0, rows, :] = (jnp.dot(hm, wd_ref[0], preferred_element_type=F32) * gs_ref[rows, :]).astype(BF16)
    y_ref[0, 0, cap:, :] = jnp.zeros((y_ref.shape[2] - cap, y_ref.shape[3]), BF16)


def _experts(base, h2, pos3, g3, wg, wu, wd, cap):
    B, T, D = h2.shape
    nk = T // MOE_TILE
    yr = cap + MOE_WIN
    ff = wg.shape[2]
    wspec = lambda shape: pl.BlockSpec((1,) + shape, lambda b, e, base: (e, 0, 0))
    grid_spec = pltpu.PrefetchScalarGridSpec(
        num_scalar_prefetch=1,
        grid=(B, N_EXPERTS),
        in_specs=[pl.BlockSpec((1, T, D), lambda b, e, base: (b, 0, 0), pipeline_mode=pl.Buffered(1)),
                  pl.BlockSpec((1, nk, MOE_TILE), lambda b, e, base: (b * N_EXPERTS + e, 0, 0)),
                  pl.BlockSpec((1, nk, MOE_TILE), lambda b, e, base: (b * N_EXPERTS + e, 0, 0)),
                  wspec((D, ff)), wspec((D, ff)), wspec((ff, D))],
        out_specs=pl.BlockSpec((1, 1, yr, D), lambda b, e, base: (b, e, 0, 0)),
        scratch_shapes=[pltpu.VMEM((yr, D), BF16), pltpu.VMEM((yr, 1), F32)])
    return pl.pallas_call(
        functools.partial(_expert_kernel, n_tiles=nk, cap=cap),
        grid_spec=grid_spec,
        out_shape=jax.ShapeDtypeStruct((B, N_EXPERTS, yr, D), BF16),
        compiler_params=_params(("arbitrary", "arbitrary")),
        name="expert_ffn",
    )(base, h2, pos3, g3, wg, wu, wd)


def _combine_kernel(base_ref, y_ref, pos_ref, x1_ref, g2_ref, fn_ref, o_ref, *, n_tiles, tiles_per_blk):
    b, tb, e = pl.program_id(0), pl.program_id(1), pl.program_id(2)
    flat0 = (b * N_EXPERTS + e) * (n_tiles + 1) + tb * tiles_per_blk
    pick = lax.broadcasted_iota(I32, (MOE_TILE, N_EXPERTS), 1) == e
    col_id = lax.broadcasted_iota(I32, (MOE_TILE, MOE_TILE), 1).astype(F32)

    @pl.when(e == 0)
    def _():
        o_ref[...] = jnp.zeros_like(o_ref)

    def column(k):
        rows = slice(k * MOE_TILE, (k + 1) * MOE_TILE)
        pcol = jnp.sum(jnp.where(pick, pos_ref[0, rows, :].astype(F32), 0.0), axis=1, keepdims=True)
        return rows, pcol

    for k in range(tiles_per_blk):
        s16 = _window_start(base_ref, flat0 + k)
        rows, pcol = column(k)
        onehot = jnp.where(col_id == pcol - s16.astype(F32), 1.0, 0.0).astype(BF16)
        o_ref[0, rows, :] += jnp.dot(onehot, y_ref[0, 0, pl.ds(s16, MOE_TILE), :], preferred_element_type=F32)

    for k in range(tiles_per_blk):
        s16 = _window_start(base_ref, flat0 + k)

        @pl.when(base_ref[flat0 + k + 1] > s16 + MOE_TILE)
        def _():
            rows, pcol = column(k)
            tail = lax.broadcasted_iota(I32, (MOE_TILE, BF16_ROWS), 1).astype(F32) + float(MOE_TILE)
            onehot = jnp.where(tail == pcol - s16.astype(F32), 1.0, 0.0).astype(BF16)
            o_ref[0, rows, :] += jnp.dot(onehot, y_ref[0, 0, pl.ds(s16 + MOE_TILE, BF16_ROWS), :],
                                         preferred_element_type=F32)

    @pl.when(e == N_EXPERTS - 1)
    def _():
        for k in range(tiles_per_blk):
            rows = slice(k * MOE_TILE, (k + 1) * MOE_TILE)
            x2 = x1_ref[0, rows, :] + g2_ref[0] * o_ref[0, rows, :]
            o_ref[0, rows, :] = _rms(x2, fn_ref[...])


def _combine(base, y, pos_t, x1, mod3, final_norm):
    B, T, D = x1.shape
    nk = T // MOE_TILE
    per = 8
    blk = per * MOE_TILE
    yr = y.shape[2]
    tok = lambda b, tb, e, base: (b, tb, 0)
    grid_spec = pltpu.PrefetchScalarGridSpec(
        num_scalar_prefetch=1,
        grid=(B, nk // per, N_EXPERTS),
        in_specs=[pl.BlockSpec((1, 1, yr, D), lambda b, tb, e, base: (b, e, 0, 0)),
                  pl.BlockSpec((1, blk, N_EXPERTS), tok),
                  pl.BlockSpec((1, blk, D), tok),
                  pl.BlockSpec((1, 1, D), lambda b, tb, e, base: (b, 0, 5)),
                  pl.BlockSpec((1, D), lambda b, tb, e, base: (0, 0))],
        out_specs=pl.BlockSpec((1, blk, D), tok))
    return pl.pallas_call(
        functools.partial(_combine_kernel, n_tiles=nk, tiles_per_blk=per),
        grid_spec=grid_spec,
        out_shape=jax.ShapeDtypeStruct((B, T, D), F32),
        compiler_params=_params(("arbitrary",) * 3),
        name="moe_combine",
    )(base, y, pos_t, x1, mod3, final_norm)


def _rope_tables(T):
    rows = T // GRID_W
    row = np.repeat(np.arange(rows, dtype=np.float64), GRID_W)
    col = np.tile(np.arange(GRID_W, dtype=np.float64), rows)
    inv = ROPE_BASE ** (-np.arange(ROPE_PAIRS, dtype=np.float64) / ROPE_PAIRS)
    ar, ac = row[:, None] * inv, col[:, None] * inv
    ones = np.ones((T, LANES - QK_ROPE))
    cos = np.concatenate([np.cos(ar), np.cos(ar), np.cos(ac), np.cos(ac), ones], axis=1)
    sin = np.concatenate([-np.sin(ar), np.sin(ar), -np.sin(ac), np.sin(ac), 0.0 * ones], axis=1)
    cos = np.concatenate([np.ones((ROW_TILE, LANES)), cos], axis=0)
    sin = np.concatenate([np.zeros((ROW_TILE, LANES)), sin], axis=0)
    return jnp.asarray(cos, F32), jnp.asarray(sin, F32)


def _blockdiag_dense(w):
    n, bs, _ = w.shape
    rows = jnp.broadcast_to(w.transpose(1, 0, 2).reshape(1, bs, n * bs), (n, bs, n * bs)).reshape(n * bs, n * bs)
    r = jnp.arange(n * bs) // bs
    return jnp.where(r[:, None] == r[None, :], rows, 0.0).astype(BF16)


def kernel(x, c, ctx, c_ctx, w_mod, b_mod, norm1, w_in, q_norm, w_uq, kv_norm, w_ukv, conv_w, conv_b,
           w_qblk, w_kblk, w_vblk, w_gate, b_gate, ml_norm, ml_skip, w_out, norm2, w_router,
           w_e_gate, w_e_up, w_e_down, final_norm):
    B, T, D = x.shape
    assert w_mod.shape[0] == 1 and D == D_MODEL and ctx.shape[1] == ROW_TILE
    cap = CAP_FACTOR * T // N_EXPERTS

    cc = jnp.zeros((SUBLANES, D), F32).at[:B].set(c).at[B].set(c_ctx)
    mod = _modulation(cc, w_mod[0], b_mod[0].reshape(1, -1))
    mod3 = mod[:B + 1].reshape(B + 1, 1, 6 * D)

    wi = w_in[0]
    zpad = jnp.zeros((D, LANES - QK_ROPE), F32)
    w_in_p = jnp.concatenate([wi[:, :_C_KROPE + QK_ROPE], zpad, wi[:, _C_KROPE + QK_ROPE:]], axis=1).astype(BF16)
    wuq = w_uq[0].reshape(Q_LORA, MLA_HEADS, QK_NOPE + QK_ROPE)
    wuq_nope = wuq[:, :, :QK_NOPE].reshape(Q_LORA, MLA_HEADS * QK_NOPE).astype(BF16)
    wuq_pe = jnp.pad(wuq[:, :, QK_NOPE:], ((0, 0), (0, 0), (0, LANES - QK_ROPE))).reshape(
        Q_LORA, MLA_HEADS * LANES).astype(BF16)
    wukv = w_ukv[0].reshape(KV_LORA, MLA_HEADS, QK_NOPE + V_DIM)
    wukv_p = jnp.concatenate([wukv[:, :, :QK_NOPE].reshape(KV_LORA, -1),
                              wukv[:, :, QK_NOPE:].reshape(KV_LORA, -1)], axis=1).astype(BF16)
    cos_t, sin_t = _rope_tables(T)

    q, k, v, xm, z, gmla, gml = _input_projection(
        ctx, x, mod3, norm1, w_in_p, q_norm, wuq_nope, wuq_pe, kv_norm, wukv_p, cos_t, sin_t)

    conv_w8 = jnp.zeros((SUBLANES, ML_INNER), F32).at[:CONV_W].set(conv_w[0])
    wg3 = w_gate[0].reshape(3, ML_INNER, 4 * ML_HEADS).transpose(0, 2, 1).astype(BF16)
    mq, mk, mv, xc, gates = _mlstm_features(
        xm, conv_w8, conv_b, _blockdiag_dense(w_qblk[0].transpose(0, 2, 1)), _blockdiag_dense(w_kblk[0]),
        _blockdiag_dense(w_vblk[0].transpose(0, 2, 1)), wg3, b_gate[0].reshape(-1, 1))

    y_mla = _attention(q, k, v)
    hf, hb = _mlstm_scan(gates, mq, mk, mv)

    x1, h2, aff_t = _merge(hf, hb, z, xc, gmla, gml, y_mla, x, mod3, ml_norm, ml_skip,
                           w_out[0].astype(BF16), norm2, w_router[0].T)

    pos, cs = _select(aff_t, cap)
    base = jnp.concatenate([cs[:, :, ::MOE_TILE], jnp.full((B, N_EXPERTS, 1), cap, I32)], axis=2).reshape(-1)
    tiled = (B * N_EXPERTS, T // MOE_TILE, MOE_TILE)
    y = _experts(base, h2, pos.reshape(tiled), aff_t.reshape(tiled), w_e_gate[0].astype(BF16),
                 w_e_up[0].astype(BF16), w_e_down[0].astype(BF16), cap)
    return _combine(base, y, pos.transpose(0, 2, 1), x1, mod3, final_norm.reshape(1, -1))
```
